```python
import math
import jax, jax.numpy as jnp
from jax import lax
import numpy as np

D_MODEL = 1024
BATCH = 4
SEQ = 8192
DEPTH = 2

CHUNK = 64
N_META = 16
Q_BLOCK = 128
DIFF_HEADS = 4
DIFF_HEAD_DIM = 64
DIFF_V_DIM = 2 * DIFF_HEAD_DIM
FOX_HEADS = 8
FOX_HEAD_DIM = 64
DIFF_QK_W = DIFF_HEADS * 2 * DIFF_HEAD_DIM
DIFF_V_W = DIFF_HEADS * DIFF_V_DIM
FOX_W = FOX_HEADS * FOX_HEAD_DIM
EVEN_IN_W = 2 * DIFF_QK_W + DIFF_V_W + 3 * FOX_W + FOX_HEADS
EVEN_MIX_W = DIFF_V_W + FOX_W
RET_HEADS = 4
RET_QK_DIM = D_MODEL // RET_HEADS
RET_V_DIM = 2 * RET_QK_DIM
RET_QK_W = RET_HEADS * RET_QK_DIM
RET_V_W = RET_HEADS * RET_V_DIM
RET_IN_W = 2 * RET_QK_W + 2 * RET_V_W
D_FF = 4 * D_MODEL
N_EVEN = (DEPTH + 1) // 2
N_ODD = DEPTH // 2
DEEPNORM_ALPHA = (2 * DEPTH) ** 0.25
DEEPNORM_BETA = (8 * DEPTH) ** -0.25
LN_EPS = 1e-5
RMS_EPS = 1e-6

kernel_name = "hybrid_diff_fox_retention_deepnorm"

F32 = jnp.float32


def _chunk_id(pos):
    return (pos + CHUNK - N_META) // CHUNK


def _layer_norm(x, g, b):
    xf = x.astype(F32)
    mu = jnp.mean(xf, axis=-1, keepdims=True)
    var = jnp.mean(jnp.square(xf - mu), axis=-1, keepdims=True)
    y = (xf - mu) * lax.rsqrt(var + LN_EPS)
    return (y * g.astype(F32) + b.astype(F32)).astype(x.dtype)


def _rms_norm(x):
    xf = x.astype(F32)
    return xf * lax.rsqrt(jnp.mean(jnp.square(xf), axis=-1, keepdims=True) + RMS_EPS)


def _even_mixer(h, w_in, f_bias, lam_vecs, subln_g, w_out, layer_idx):
    B, L, _ = h.shape
    Lp = -(-L // Q_BLOCK) * Q_BLOCK
    nblk = Lp // Q_BLOCK
    proj = h @ w_in
    offs = [int(o) for o in np.cumsum([DIFF_QK_W, DIFF_QK_W, DIFF_V_W, FOX_W, FOX_W, FOX_W])]
    qa, ka, va, qb, kb, vb, fb = jnp.split(proj, offs, axis=-1)
    pad = ((0, 0), (0, Lp - L), (0, 0))
    qa, ka, va, qb, kb, vb = (jnp.pad(t, pad) for t in (qa, ka, va, qb, kb, vb))
    log_f = jnp.pad(jax.nn.log_sigmoid((fb + f_bias).astype(F32)), pad)
    cum_f = jnp.cumsum(log_f, axis=1).transpose(0, 2, 1)

    qa = qa.reshape(B, nblk, Q_BLOCK, DIFF_HEADS, 2, DIFF_HEAD_DIM).transpose(1, 0, 3, 4, 2, 5)
    ka = ka.reshape(B, Lp, DIFF_HEADS, 2, DIFF_HEAD_DIM).transpose(0, 2, 3, 1, 4)
    va = va.reshape(B, Lp, DIFF_HEADS, DIFF_V_DIM).transpose(0, 2, 1, 3)
    qb = qb.reshape(B, nblk, Q_BLOCK, FOX_HEADS, FOX_HEAD_DIM).transpose(1, 0, 3, 2, 4)
    kb = kb.reshape(B, Lp, FOX_HEADS, FOX_HEAD_DIM).transpose(0, 2, 1, 3)
    vb = vb.reshape(B, Lp, FOX_HEADS, FOX_HEAD_DIM).transpose(0, 2, 1, 3)
    cq = cum_f.reshape(B, FOX_HEADS, nblk, Q_BLOCK).transpose(2, 0, 1, 3)

    lam_init = 0.8 - 0.6 * math.exp(-0.3 * layer_idx)
    lv = lam_vecs.astype(F32)
    lam = jnp.exp(jnp.sum(lv[0] * lv[1])) - jnp.exp(jnp.sum(lv[2] * lv[3])) + lam_init
    slopes = 2.0 ** (-8.0 * jnp.arange(1, DIFF_HEADS + 1, dtype=F32) / DIFF_HEADS)
    scale_a = DIFF_HEAD_DIM ** -0.5
    scale_b = FOX_HEAD_DIM ** -0.5
    k_pos = jnp.arange(Lp)
    k_chunk = _chunk_id(k_pos)
    q_pos_blk = k_pos.reshape(nblk, Q_BLOCK)

    def one_block(args):
        qa_i, qb_i, cq_i, q_pos = args
        dist = jnp.abs(q_pos[:, None] - k_pos[None, :]).astype(F32)
        chunk_vis = k_chunk[None, :] <= _chunk_id(q_pos)[:, None]
        frame_vis = k_pos[None, :] <= q_pos[:, None]
        s_a = jnp.einsum('bhmqd,bhmkd->bhmqk', qa_i, ka).astype(F32) * scale_a \
            - slopes[:, None, None, None] * dist
        p_a = jax.nn.softmax(jnp.where(chunk_vis, s_a, -jnp.inf), axis=-1)
        p_diff = p_a[:, :, 0] - lam * p_a[:, :, 1]
        o_a = jnp.einsum('bhqk,bhkd->bhqd', p_diff.astype(va.dtype), va)
        s_b = jnp.einsum('bhqd,bhkd->bhqk', qb_i, kb).astype(F32) * scale_b \
            + cq_i[..., :, None] - cum_f[:, :, None, :]
        p_b = jax.nn.softmax(jnp.where(frame_vis, s_b, -jnp.inf), axis=-1)
        o_b = jnp.einsum('bhqk,bhkd->bhqd', p_b.astype(vb.dtype), vb)
        return o_a, o_b

    o_a, o_b = lax.map(one_block, (qa, qb, cq, q_pos_blk))
    o_a = o_a.transpose(1, 0, 3, 2, 4).reshape(B, Lp, DIFF_HEADS, DIFF_V_DIM)[:, :L]
    o_a = (_rms_norm(o_a) * subln_g.astype(F32) * (1.0 - lam_init)).astype(h.dtype)
    o_b = o_b.transpose(1, 0, 3, 2, 4).reshape(B, Lp, FOX_W)[:, :L]
    o = jnp.concatenate([o_a.reshape(B, L, DIFF_V_W), o_b.astype(h.dtype)], axis=-1)
    return o @ w_out


def _retention(h, w_in, w_out):
    B, L, _ = h.shape
    proj = h @ w_in
    q, k, v, g = jnp.split(proj, [RET_QK_W, 2 * RET_QK_W, 2 * RET_QK_W + RET_V_W], axis=-1)
    k = k * RET_QK_DIM ** -0.5
    lpad = (-N_META) % CHUNK
    Lr = L + lpad
    nc = Lr // CHUNK

    def to_chunks(t, d):
        t = jnp.pad(t, ((0, 0), (lpad, 0), (0, 0)))
        return t.reshape(B, nc, CHUNK, RET_HEADS, d).transpose(1, 0, 3, 2, 4)

    qc, kc, vc = to_chunks(q, RET_QK_DIM), to_chunks(k, RET_QK_DIM), to_chunks(v, RET_V_DIM)
    log_gamma = jnp.log1p(-(2.0 ** (-5.0 - jnp.arange(RET_HEADS, dtype=F32))))
    idx = jnp.arange(CHUNK, dtype=F32)
    intra = jnp.exp(log_gamma[:, None, None] * jnp.abs(idx[:, None] - idx[None, :]))
    q_dec = jnp.exp(log_gamma[:, None] * (idx + 1.0))[:, :, None]
    k_dec = jnp.exp(log_gamma[:, None] * (CHUNK - 1.0 - idx))[:, :, None]
    c_dec = jnp.exp(log_gamma * CHUNK)[:, None, None]

    def step(state, inp):
        q_i, k_i, v_i = inp
        scores = jnp.einsum('bhid,bhjd->bhij', q_i, k_i) * intra
        o = jnp.einsum('bhij,bhje->bhie', scores, v_i) \
            + jnp.einsum('bhid,bhde->bhie', q_i * q_dec, state)
        state = c_dec * state + jnp.einsum('bhjd,bhje->bhde', k_i * k_dec, v_i)
        return state, o

    s0 = jnp.zeros((B, RET_HEADS, RET_QK_DIM, RET_V_DIM), F32)
    _, o = lax.scan(step, s0, (qc, kc, vc))
    o = o.transpose(1, 0, 3, 2, 4).reshape(B, Lr, RET_HEADS, RET_V_DIM)[:, lpad:]
    o = _rms_norm(o).reshape(B, L, RET_V_W).astype(h.dtype)
    y = jax.nn.silu(g) * o
    return y @ w_out


def _sq_relu_mlp(h, w1, w2):
    return jnp.square(jax.nn.relu(h @ w1)) @ w2


def setup_inputs(seed: int = 0) -> dict:
    key = jax.random.key(seed)
    ks = jax.random.split(key, 13)

    def nrm(k, shape, scale):
        return jax.random.normal(k, shape, F32) * scale

    beta = DEEPNORM_BETA
    x = nrm(ks[0], (BATCH, SEQ, D_MODEL), 1.0)
    meta_tokens = nrm(ks[1], (N_META, D_MODEL), 1.0)
    even_col_scale = jnp.concatenate([
        jnp.ones((2 * DIFF_QK_W,), F32), jnp.full((DIFF_V_W,), beta, F32),
        jnp.ones((2 * FOX_W,), F32), jnp.full((FOX_W,), beta, F32),
        jnp.ones((FOX_HEADS,), F32)])
    even_w_in = nrm(ks[2], (N_EVEN, D_MODEL, EVEN_IN_W), D_MODEL ** -0.5) * even_col_scale
    even_f_bias = jnp.linspace(1.0, 6.0, FOX_HEADS, dtype=F32) + nrm(ks[3], (N_EVEN, FOX_HEADS), 0.1)
    diff_lambda = nrm(ks[4], (N_EVEN, 4, DIFF_HEAD_DIM), 0.1)
    diff_subln_g = 1.0 + nrm(ks[5], (N_EVEN, DIFF_V_DIM), 0.02)
    even_w_out = nrm(ks[6], (N_EVEN, EVEN_MIX_W, D_MODEL), EVEN_MIX_W ** -0.5 * beta)
    ret_col_scale = jnp.concatenate([
        jnp.ones((2 * RET_QK_W,), F32), jnp.full((RET_V_W,), beta, F32), jnp.ones((RET_V_W,), F32)])
    ret_w_in = nrm(ks[7], (N_ODD, D_MODEL, RET_IN_W), D_MODEL ** -0.5) * ret_col_scale
    ret_w_out = nrm(ks[8], (N_ODD, RET_V_W, D_MODEL), RET_V_W ** -0.5 * beta)
    ln_g = 1.0 + nrm(ks[9], (DEPTH, 2, D_MODEL), 0.02)
    ln_b = nrm(ks[10], (DEPTH, 2, D_MODEL), 0.02)
    ffn_w1 = nrm(ks[11], (DEPTH, D_MODEL, D_FF), D_MODEL ** -0.5 * beta)
    ffn_w2 = nrm(ks[12], (DEPTH, D_FF, D_MODEL), D_FF ** -0.5 * beta)
    return {"x": x, "meta_tokens": meta_tokens, "even_w_in": even_w_in, "even_f_bias": even_f_bias,
            "diff_lambda": diff_lambda, "diff_subln_g": diff_subln_g, "even_w_out": even_w_out,
            "ret_w_in": ret_w_in, "ret_w_out": ret_w_out, "ln_g": ln_g, "ln_b": ln_b,
            "ffn_w1": ffn_w1, "ffn_w2": ffn_w2}


def reference(x, meta_tokens, even_w_in, even_f_bias, diff_lambda, diff_subln_g, even_w_out,
              ret_w_in, ret_w_out, ln_g, ln_b, ffn_w1, ffn_w2):
    B = x.shape[0]
    meta = jnp.broadcast_to(meta_tokens[None].astype(x.dtype), (B, N_META, D_MODEL))
    h = jnp.concatenate([meta, x], axis=1)
    for layer in range(DEPTH):
        i = layer // 2
        if layer % 2 == 0:
            mix = _even_mixer(h, even_w_in[i], even_f_bias[i], diff_lambda[i], diff_subln_g[i],
                              even_w_out[i], layer)
        else:
            mix = _retention(h, ret_w_in[i], ret_w_out[i])
        h = _layer_norm(DEEPNORM_ALPHA * h + mix, ln_g[layer, 0], ln_b[layer, 0])
        h = _layer_norm(DEEPNORM_ALPHA * h + _sq_relu_mlp(h, ffn_w1[layer], ffn_w2[layer]),
                        ln_g[layer, 1], ln_b[layer, 1])
    return h[:, N_META:, :]
```

```python
import functools
import math

import jax
import jax.numpy as jnp
import numpy as np
from jax import lax
from jax.experimental import pallas as pl
from jax.experimental.pallas import tpu as pltpu

F32 = jnp.float32
BF16 = jnp.bfloat16

D_MODEL = 1024
BATCH = 4
SEQ = 8192
DEPTH = 2
CHUNK = 64
N_META = 16
DIFF_HEADS = 4
DIFF_HEAD_DIM = 64
DIFF_V_DIM = 128
FOX_HEADS = 8
FOX_HEAD_DIM = 64
RET_HEADS = 4
RET_QK_DIM = 256
RET_V_DIM = 512
D_FF = 4 * D_MODEL
DEEPNORM_ALPHA = (2 * DEPTH) ** 0.25
LN_EPS = 1e-5
RMS_EPS = 1e-6
LAM_INIT0 = 0.8 - 0.6 * math.exp(-0.3 * 0)

NF = BATCH * SEQ
META_BLK = 256
NT = NF + META_BLK
LANES = 128
NEG = -1e30

ATT_T = 512
RET_T = 512
VMEM_LIMIT = 56 * 1024 * 1024


def _cparams(sem):
    return pltpu.CompilerParams(dimension_semantics=sem, vmem_limit_bytes=VMEM_LIMIT)


def _row_positions(i, tm):
    r = i * tm + lax.broadcasted_iota(jnp.int32, (tm, 1), 0)
    is_frame = r < NF
    m = r - NF
    pos = jnp.where(is_frame, (r & (SEQ - 1)) + N_META, m)
    valid = jnp.logical_or(is_frame, m < N_META)
    return pos, valid


def _proj_kernel(a_ref, w_ref, c_ref, o_ref, *, tm, heads, head_w, use_pos):
    y = jnp.dot(a_ref[...], w_ref[...], preferred_element_type=F32)
    y = y + c_ref[0:1, :]
    if use_pos:
        pos, valid = _row_positions(pl.program_id(0), tm)
        hi = (pos >> 7).astype(F32)
        lo = (pos & 127).astype(F32)
        y = y + hi * c_ref[1:2, :] + lo * c_ref[2:3, :] + jnp.where(valid, 0.0, 1.0) * c_ref[3:4, :]
    if heads is None:
        o_ref[...] = y.astype(o_ref.dtype)
    else:
        for h in range(heads):
            o_ref[h] = y[:, h * head_w:(h + 1) * head_w].astype(o_ref.dtype)


def _proj(a, w, coef, *, tm, name, heads=None, head_w=None, use_pos=False, out_dtype=BF16):
    m, k = a.shape
    n = w.shape[1]
    if heads is None:
        out_shape = jax.ShapeDtypeStruct((m, n), out_dtype)
        out_spec = pl.BlockSpec((tm, n), lambda i: (i, 0))
    else:
        out_shape = jax.ShapeDtypeStruct((heads, m, head_w), out_dtype)
        out_spec = pl.BlockSpec((heads, tm, head_w), lambda i: (0, i, 0))
    return pl.pallas_call(
        functools.partial(_proj_kernel, tm=tm, heads=heads, head_w=head_w, use_pos=use_pos),
        grid=(m // tm,),
        in_specs=[pl.BlockSpec((tm, k), lambda i: (i, 0)),
                  pl.BlockSpec((k, n), lambda i: (0, 0)),
                  pl.BlockSpec((8, n), lambda i: (0, 0))],
        out_specs=out_spec,
        out_shape=out_shape,
        compiler_params=_cparams(("arbitrary",)),
        name=name,
    )(a, w, coef)


def _proj_t_kernel(wt_ref, a_ref, o_ref):
    o_ref[...] = lax.dot_general(wt_ref[...], a_ref[...], (((1,), (1,)), ((), ())),
                                 preferred_element_type=F32).astype(o_ref.dtype)


def _proj_t(wt, a, *, tm):
    n, k = wt.shape
    m = a.shape[0]
    return pl.pallas_call(
        _proj_t_kernel,
        grid=(m // tm,),
        in_specs=[pl.BlockSpec((n, k), lambda i: (0, 0)),
                  pl.BlockSpec((tm, k), lambda i: (i, 0))],
        out_specs=pl.BlockSpec((n, tm), lambda i: (0, i)),
        out_shape=jax.ShapeDtypeStruct((n, m), BF16),
        compiler_params=_cparams(("arbitrary",)),
        name="proj_ret_kt",
    )(wt, a)


def _relu2_kernel(a_ref, w_ref, o_ref):
    y = jnp.dot(a_ref[...], w_ref[...], preferred_element_type=F32)
    y = jnp.maximum(y, 0.0)
    o_ref[...] = (y * y).astype(o_ref.dtype)


def _mm_relu2(a, w, *, rows, tm, tn, name):
    k = a.shape[1]
    n = w.shape[1]
    return pl.pallas_call(
        _relu2_kernel,
        grid=(rows // tm, n // tn),
        in_specs=[pl.BlockSpec((tm, k), lambda i, j: (i, 0)),
                  pl.BlockSpec((k, tn), lambda i, j: (0, j))],
        out_specs=pl.BlockSpec((tm, tn), lambda i, j: (i, j)),
        out_shape=jax.ShapeDtypeStruct((rows, n), BF16),
        compiler_params=_cparams(("arbitrary", "arbitrary")),
        name=name,
    )(a, w)


def _res_ln_kernel(a_ref, w_ref, r_ref, gb_ref, o32_ref, o16_ref):
    y = jnp.dot(a_ref[...], w_ref[...], preferred_element_type=F32)
    z = DEEPNORM_ALPHA * r_ref[...] + y
    mu = jnp.mean(z, axis=-1, keepdims=True)
    zc = z - mu
    var = jnp.mean(zc * zc, axis=-1, keepdims=True)
    out = zc * lax.rsqrt(var + LN_EPS) * gb_ref[0:1, :] + gb_ref[1:2, :]
    o32_ref[...] = out
    o16_ref[...] = out.astype(BF16)


def _mm_res_ln(a, w, res, gb, *, rows, tm, name):
    k = a.shape[1]
    n = w.shape[1]
    return pl.pallas_call(
        _res_ln_kernel,
        grid=(rows // tm,),
        in_specs=[pl.BlockSpec((tm, k), lambda i: (i, 0)),
                  pl.BlockSpec((k, n), lambda i: (0, 0), pipeline_mode=pl.Buffered(1)),
                  pl.BlockSpec((tm, n), lambda i: (i, 0)),
                  pl.BlockSpec((8, n), lambda i: (0, 0))],
        out_specs=[pl.BlockSpec((tm, n), lambda i: (i, 0)),
                   pl.BlockSpec((tm, n), lambda i: (i, 0))],
        out_shape=[jax.ShapeDtypeStruct((rows, n), F32),
                   jax.ShapeDtypeStruct((rows, n), BF16)],
        compiler_params=_cparams(("arbitrary",)),
        name=name,
    )(a, w, res, gb)


PREP_T = 256


def _split3(x):
    hi = x.astype(BF16)
    r1 = x - hi.astype(F32)
    mid = r1.astype(BF16)
    lo = (r1 - mid.astype(F32)).astype(BF16)
    return hi, mid, lo


def _fox_prep_kernel(fb_ref, bias_ref, tri_ref, sel_ref, kb_ref, o_ref, carry_ref):
    s = pl.program_id(0)
    steps_per_batch = SEQ // PREP_T
    is_meta = s == NF // PREP_T

    @pl.when(jnp.logical_or(s % steps_per_batch == 0, is_meta))
    def _():
        carry_ref[...] = jnp.zeros_like(carry_ref)

    x = fb_ref[...] + bias_ref[0:1, :]
    logf = jnp.minimum(x, 0.0) - jnp.log1p(jnp.exp(-jnp.abs(x)))
    tri = tri_ref[...]
    cum = carry_ref[0:1, :]
    for piece in _split3(logf):
        cum = cum + jnp.dot(tri, piece, preferred_element_type=F32)
    carry_ref[0:1, :] = cum[PREP_T - 1:PREP_T, :]

    row = lax.broadcasted_iota(jnp.int32, (PREP_T, 1), 0)
    meta_g = jnp.where(row < N_META, cum[N_META - 1:N_META, :] - cum, NEG)
    g = jnp.where(is_meta, meta_g, -cum)

    aug = jnp.zeros((PREP_T, FOX_HEADS * LANES), F32)
    for p, piece in enumerate(_split3(g)):
        aug = aug + jnp.dot(piece, sel_ref[p], preferred_element_type=F32)
    for h in range(FOX_HEADS):
        o_ref[h] = (kb_ref[h].astype(F32) + aug[:, h * LANES:(h + 1) * LANES]).astype(BF16)


def _fox_prep(fb, bias_row, kb):
    tri = jnp.tril(jnp.ones((PREP_T, PREP_T), F32)).astype(BF16)
    sel = np.zeros((3, LANES, FOX_HEADS * LANES), np.float32)
    for p in range(3):
        for h in range(FOX_HEADS):
            sel[p, h, h * LANES + FOX_HEAD_DIM + p] = 1.0
    sel = jnp.asarray(sel, BF16)
    return pl.pallas_call(
        _fox_prep_kernel,
        grid=(NT // PREP_T,),
        in_specs=[pl.BlockSpec((PREP_T, LANES), lambda s: (s, 0)),
                  pl.BlockSpec((8, LANES), lambda s: (0, 0)),
                  pl.BlockSpec((PREP_T, PREP_T), lambda s: (0, 0)),
                  pl.BlockSpec((3, LANES, FOX_HEADS * LANES), lambda s: (0, 0, 0)),
                  pl.BlockSpec((FOX_HEADS, PREP_T, LANES), lambda s: (0, s, 0))],
        out_specs=pl.BlockSpec((FOX_HEADS, PREP_T, LANES), lambda s: (0, s, 0)),
        out_shape=jax.ShapeDtypeStruct(kb.shape, BF16),
        scratch_shapes=[pltpu.VMEM((8, LANES), F32)],
        compiler_params=_cparams(("arbitrary",)),
        name="fox_prep",
    )(fb, bias_row, tri, sel, kb)


N_HM = 2 * DIFF_HEADS
VA_W = 2 * LANES
VB_W = LANES
ATT_OUT_W = DIFF_HEADS * DIFF_V_DIM + FOX_HEADS * LANES


def _online_step(q, k, v, m_ref, acc_ref, bias):
    s = lax.dot_general(q, k, (((1,), (1,)), ((), ())), preferred_element_type=F32)
    if bias is not None:
        s = s + bias
    m_prev = m_ref[...]
    m_new = jnp.maximum(m_prev, jnp.max(s, axis=1, keepdims=True))
    alpha = jnp.exp(m_prev - m_new)
    p = jnp.exp(s - m_new).astype(BF16)
    acc_ref[...] = alpha * acc_ref[...] + jnp.dot(p, v, preferred_element_type=F32)
    m_ref[...] = m_new


def _attn_init(m_a, acc_a, m_b, acc_b):
    m_a[...] = jnp.full(m_a.shape, NEG, F32)
    m_b[...] = jnp.full(m_b.shape, NEG, F32)
    acc_a[...] = jnp.zeros(acc_a.shape, F32)
    acc_b[...] = jnp.zeros(acc_b.shape, F32)


def _attn_pass(qa, qb, ka, va, kb, vb, m_a, acc_a, m_b, acc_b, slopes_ref, *, masked, pos_off, tq, tk):
    if masked:
        i = lax.broadcasted_iota(jnp.int32, (tq, tk), 0)
        j = lax.broadcasted_iota(jnp.int32, (tq, tk), 1)
        ahead = jnp.maximum(j - i, 0).astype(F32)
        chunk_mask = jnp.where(((j + pos_off) >> 6) > ((i + pos_off) >> 6), NEG, 0.0)
        causal_mask = jnp.where(j > i, NEG, 0.0)

    def diff_body(h, c):
        bias = (ahead * (-2.0 * slopes_ref[h]) + chunk_mask) if masked else None
        for mp in range(2):
            hm = 2 * h + mp
            _online_step(qa[hm], ka[hm], va[h], m_a.at[hm], acc_a.at[hm], bias)
        return c

    lax.fori_loop(0, DIFF_HEADS, diff_body, 0)

    def fox_body(h, c):
        _online_step(qb[h], kb[h], vb[h], m_b.at[h], acc_b.at[h], causal_mask if masked else None)
        return c

    lax.fori_loop(0, FOX_HEADS, fox_body, 0)


def _attn_finalize(lam_ref, g_ref, acc_a, acc_b, out_ref):
    lv = lam_ref[...]
    lam = (jnp.exp(jnp.sum(lv[0:1] * lv[1:2], axis=1, keepdims=True))
           - jnp.exp(jnp.sum(lv[2:3] * lv[3:4], axis=1, keepdims=True)) + LAM_INIT0)
    gain = g_ref[0:1, :] * (1.0 - LAM_INIT0)
    for h in range(DIFF_HEADS):
        a0 = acc_a[2 * h]
        a1 = acc_a[2 * h + 1]
        o = (a0[:, :DIFF_V_DIM] / a0[:, DIFF_V_DIM:DIFF_V_DIM + 1]
             - lam * (a1[:, :DIFF_V_DIM] / a1[:, DIFF_V_DIM:DIFF_V_DIM + 1]))
        ms = jnp.mean(o * o, axis=1, keepdims=True)
        out_ref[:, h * DIFF_V_DIM:(h + 1) * DIFF_V_DIM] = (o * lax.rsqrt(ms + RMS_EPS) * gain).astype(BF16)
    base = DIFF_HEADS * DIFF_V_DIM
    for h in range(FOX_HEADS):
        a = acc_b[h]
        out_ref[:, base + h * LANES:base + (h + 1) * LANES] = (
            a / a[:, FOX_HEAD_DIM:FOX_HEAD_DIM + 1]).astype(BF16)


def _attn_kernel(qi_ref, kv_ref, slopes_ref,
                 qa, ka, va, qb, kb, vb, kam, vam, kbm, vbm, lam_ref, g_ref, prev_ref,
                 out_ref, m_a, acc_a, m_b, acc_b):
    del prev_ref
    step = pl.program_id(1)
    qi = qi_ref[step]
    kv = kv_ref[step]
    state = (m_a, acc_a, m_b, acc_b)

    @pl.when(kv == 0)
    def _():
        _attn_init(*state)
        _attn_pass(qa, qb, kam, vam, kbm, vbm, *state, slopes_ref,
                   masked=False, pos_off=0, tq=ATT_T, tk=META_BLK)

    @pl.when(kv < qi)
    def _():
        _attn_pass(qa, qb, ka, va, kb, vb, *state, slopes_ref,
                   masked=False, pos_off=0, tq=ATT_T, tk=ATT_T)

    @pl.when(kv == qi)
    def _():
        _attn_pass(qa, qb, ka, va, kb, vb, *state, slopes_ref,
                   masked=True, pos_off=CHUNK, tq=ATT_T, tk=ATT_T)
        _attn_finalize(lam_ref, g_ref, acc_a, acc_b, out_ref)


def _meta_attn_kernel(slopes_ref, qa, ka, va, qb, kb, vb, lam_ref, g_ref, out_ref, m_a, acc_a, m_b, acc_b):
    state = (m_a, acc_a, m_b, acc_b)
    _attn_init(*state)
    _attn_pass(qa, qb, ka, va, kb, vb, *state, slopes_ref,
               masked=True, pos_off=CHUNK - N_META, tq=META_BLK, tk=META_BLK)
    _attn_finalize(lam_ref, g_ref, acc_a, acc_b, out_ref)


def _attn_scratch(tq):
    return [pltpu.VMEM((N_HM, tq, 1), F32), pltpu.VMEM((N_HM, tq, VA_W), F32),
            pltpu.VMEM((FOX_HEADS, tq, 1), F32), pltpu.VMEM((FOX_HEADS, tq, VB_W), F32)]


def _attention(qa, ka, va, qb, kb, vb, lam, gain_row):
    slopes = jnp.asarray([2.0 ** (-8.0 * (h + 1) / DIFF_HEADS) for h in range(DIFF_HEADS)], F32)
    smem = pl.BlockSpec(memory_space=pltpu.SMEM)
    meta_blk = NF // META_BLK

    def meta_spec(heads, w):
        return pl.BlockSpec((heads, META_BLK, w), lambda *_: (0, meta_blk, 0))

    const2 = lambda shape: pl.BlockSpec(shape, lambda *_: (0, 0))

    out0 = pl.pallas_call(
        _meta_attn_kernel,
        grid=(1,),
        in_specs=[smem,
                  meta_spec(N_HM, LANES), meta_spec(N_HM, LANES), meta_spec(DIFF_HEADS, VA_W),
                  meta_spec(FOX_HEADS, LANES), meta_spec(FOX_HEADS, LANES), meta_spec(FOX_HEADS, VB_W),
                  const2((4, DIFF_HEAD_DIM)), const2((8, LANES))],
        out_specs=pl.BlockSpec((META_BLK, ATT_OUT_W), lambda s: (meta_blk, 0)),
        out_shape=jax.ShapeDtypeStruct((NT, ATT_OUT_W), BF16),
        scratch_shapes=_attn_scratch(META_BLK),
        compiler_params=_cparams(("arbitrary",)),
        name="attn_meta",
    )(slopes, qa, ka, va, qb, kb, vb, lam, gain_row)

    n_q = SEQ // ATT_T
    qi_tab = np.concatenate([np.full(q + 1, q, np.int32) for q in range(n_q)])
    kv_tab = np.concatenate([np.arange(q + 1, dtype=np.int32) for q in range(n_q)])

    def q_spec(heads, w):
        return pl.BlockSpec((heads, ATT_T, w), lambda b, s, qi, kv: (0, b * n_q + qi[s], 0))

    def kv_spec(heads, w):
        return pl.BlockSpec((heads, ATT_T, w), lambda b, s, qi, kv: (0, b * n_q + kv[s], 0))

    grid_spec = pltpu.PrefetchScalarGridSpec(
        num_scalar_prefetch=2,
        grid=(BATCH, len(qi_tab)),
        in_specs=[smem,
                  q_spec(N_HM, LANES), kv_spec(N_HM, LANES), kv_spec(DIFF_HEADS, VA_W),
                  q_spec(FOX_HEADS, LANES), kv_spec(FOX_HEADS, LANES), kv_spec(FOX_HEADS, VB_W),
                  meta_spec(N_HM, LANES), meta_spec(DIFF_HEADS, VA_W),
                  meta_spec(FOX_HEADS, LANES), meta_spec(FOX_HEADS, VB_W),
                  const2((4, DIFF_HEAD_DIM)), const2((8, LANES)),
                  pl.BlockSpec(memory_space=pl.ANY)],
        out_specs=pl.BlockSpec((ATT_T, ATT_OUT_W), lambda b, s, qi, kv: (b * n_q + qi[s], 0)),
        scratch_shapes=_attn_scratch(ATT_T),
    )
    return pl.pallas_call(
        _attn_kernel,
        grid_spec=grid_spec,
        out_shape=jax.ShapeDtypeStruct((NT, ATT_OUT_W), BF16),
        input_output_aliases={15: 0},
        compiler_params=_cparams(("arbitrary", "arbitrary")),
        name="attn_frames",
    )(jnp.asarray(qi_tab), jnp.asarray(kv_tab), slopes,
      qa, ka, va, qb, kb, vb, ka, va, kb, vb, lam, gain_row, out0)


def _log_gamma(h):
    return jnp.log1p(jnp.full((1, 1), -(2.0 ** (-5.0 - h)), F32))


def _retention_kernel(q_ref, kt_ref, v_ref, g_ref, ktm_ref, vm_ref, o_ref, state_ref, decay_ref):
    b = pl.program_id(0)
    t = pl.program_id(1)
    T = RET_T

    @pl.when(jnp.logical_and(b == 0, t == 0))
    def _():
        i = lax.broadcasted_iota(jnp.int32, (T, T), 0)
        j = lax.broadcasted_iota(jnp.int32, (T, T), 1)
        dist = jnp.abs(i - j).astype(F32)
        vis = (j >> 6) <= (i >> 6)
        for h in range(RET_HEADS):
            decay_ref[h] = jnp.where(vis, jnp.exp(_log_gamma(h) * dist), 0.0)

    @pl.when(t == 0)
    def _():
        m = lax.broadcasted_iota(jnp.int32, (1, META_BLK), 1)
        for h in range(RET_HEADS):
            w = jnp.exp(_log_gamma(h) * (N_META - 1 - m).astype(F32))
            kt = ktm_ref[h * RET_QK_DIM:(h + 1) * RET_QK_DIM, :].astype(F32)
            kd = jnp.where(m < N_META, kt * w, 0.0).astype(BF16)
            state_ref[h] = jnp.dot(kd, vm_ref[:, h * RET_V_DIM:(h + 1) * RET_V_DIM],
                                   preferred_element_type=F32)

    row = lax.broadcasted_iota(jnp.int32, (T, 1), 0).astype(F32)
    col = lax.broadcasted_iota(jnp.int32, (1, T), 1).astype(F32)
    for h in range(RET_HEADS):
        lg = _log_gamma(h)
        q = q_ref[:, h * RET_QK_DIM:(h + 1) * RET_QK_DIM]
        kt = kt_ref[h * RET_QK_DIM:(h + 1) * RET_QK_DIM, :]
        v = v_ref[:, h * RET_V_DIM:(h + 1) * RET_V_DIM]
        scores = jnp.dot(q, kt, preferred_element_type=F32) * decay_ref[h]
        o = jnp.dot(scores.astype(BF16), v, preferred_element_type=F32)
        qd = (q.astype(F32) * jnp.exp(lg * (row + 1.0))).astype(BF16)
        state = state_ref[h]
        o = o + jnp.dot(qd, state.astype(BF16), preferred_element_type=F32)
        kd = (kt.astype(F32) * jnp.exp(lg * (T - 1.0 - col))).astype(BF16)
        state_ref[h] = jnp.exp(lg * float(T)) * state + jnp.dot(kd, v, preferred_element_type=F32)
        o = o * lax.rsqrt(jnp.mean(o * o, axis=1, keepdims=True) + RMS_EPS)
        gate = g_ref[:, h * RET_V_DIM:(h + 1) * RET_V_DIM].astype(F32)
        o_ref[:, h * RET_V_DIM:(h + 1) * RET_V_DIM] = (gate * jax.nn.sigmoid(gate) * o).astype(BF16)


def _retention(q, kt, v, g):
    n_t = SEQ // RET_T
    qk_w = RET_HEADS * RET_QK_DIM
    v_w = RET_HEADS * RET_V_DIM
    meta_blk = NF // META_BLK
    return pl.pallas_call(
        _retention_kernel,
        grid=(BATCH, n_t),
        in_specs=[pl.BlockSpec((RET_T, qk_w), lambda b, t: (b * n_t + t, 0)),
                  pl.BlockSpec((qk_w, RET_T), lambda b, t: (0, b * n_t + t)),
                  pl.BlockSpec((RET_T, v_w), lambda b, t: (b * n_t + t, 0)),
                  pl.BlockSpec((RET_T, v_w), lambda b, t: (b * n_t + t, 0)),
                  pl.BlockSpec((qk_w, META_BLK), lambda b, t: (0, meta_blk)),
                  pl.BlockSpec((META_BLK, v_w), lambda b, t: (meta_blk, 0))],
        out_specs=pl.BlockSpec((RET_T, v_w), lambda b, t: (b * n_t + t, 0)),
        out_shape=jax.ShapeDtypeStruct((NF, v_w), BF16),
        scratch_shapes=[pltpu.VMEM((RET_HEADS, RET_QK_DIM, RET_V_DIM), F32),
                        pltpu.VMEM((RET_HEADS, RET_T, RET_T), F32)],
        compiler_params=_cparams(("arbitrary", "arbitrary")),
        name="retention",
    )(q, kt, v, g, kt, v)


def _pad_heads(w, heads, width, pad_to):
    k = w.shape[0]
    w = w.reshape(k, heads, width)
    w = jnp.pad(w, ((0, 0), (0, 0), (0, pad_to - width)))
    return w.reshape(k, heads * pad_to)


def _coef_rows(n, rows):
    c = np.zeros((8, n), np.float32)
    for r, vals in rows.items():
        for col, val in vals:
            c[r, col] = val
    return jnp.asarray(c)


def _even_weights(w_in):
    qk_w = DIFF_HEADS * 2 * DIFF_HEAD_DIM
    v_w = DIFF_HEADS * DIFF_V_DIM
    fox_w = FOX_HEADS * FOX_HEAD_DIM
    o = np.cumsum([0, qk_w, qk_w, v_w, fox_w, fox_w, fox_w, FOX_HEADS])
    sl = [w_in[:, o[i]:o[i + 1]] for i in range(7)]
    scale = DIFF_HEAD_DIM ** -0.5
    w_qa = _pad_heads(sl[0] * scale, N_HM, DIFF_HEAD_DIM, LANES).astype(BF16)
    w_ka = _pad_heads(sl[1], N_HM, DIFF_HEAD_DIM, LANES).astype(BF16)
    w_va = _pad_heads(sl[2], DIFF_HEADS, DIFF_V_DIM, VA_W).astype(BF16)
    w_qb = _pad_heads(sl[3] * scale, FOX_HEADS, FOX_HEAD_DIM, LANES).astype(BF16)
    w_kb = _pad_heads(sl[4], FOX_HEADS, FOX_HEAD_DIM, LANES).astype(BF16)
    w_vb = _pad_heads(sl[5], FOX_HEADS, FOX_HEAD_DIM, VB_W).astype(BF16)
    w_fb = jnp.pad(sl[6], ((0, 0), (0, LANES - FOX_HEADS))).astype(BF16)
    return w_qa, w_ka, w_va, w_qb, w_kb, w_vb, w_fb


def _even_coefs():
    d = DIFF_HEAD_DIM
    slopes = [2.0 ** (-8.0 * (h + 1) / DIFF_HEADS) for h in range(DIFF_HEADS)]
    n = N_HM * LANES
    c_qa = _coef_rows(n, {0: [(hm * LANES + d + p, 1.0) for hm in range(N_HM) for p in range(2)]})
    c_ka = _coef_rows(n, {
        1: [(hm * LANES + d, slopes[hm // 2] * 128.0) for hm in range(N_HM)],
        2: [(hm * LANES + d + 1, slopes[hm // 2]) for hm in range(N_HM)],
        3: [(hm * LANES + d, NEG) for hm in range(N_HM)]})
    c_va = _coef_rows(DIFF_HEADS * VA_W, {0: [(h * VA_W + DIFF_V_DIM, 1.0) for h in range(DIFF_HEADS)]})
    c_qb = _coef_rows(FOX_HEADS * LANES,
                      {0: [(h * LANES + d + p, 1.0) for h in range(FOX_HEADS) for p in range(3)]})
    c_kb = _coef_rows(FOX_HEADS * LANES, {})
    c_vb = _coef_rows(FOX_HEADS * VB_W, {0: [(h * VB_W + FOX_HEAD_DIM, 1.0) for h in range(FOX_HEADS)]})
    c_fb = _coef_rows(LANES, {})
    return c_qa, c_ka, c_va, c_qb, c_kb, c_vb, c_fb


def _rows8(*rows):
    n = rows[0].shape[-1]
    out = jnp.zeros((8, n), F32)
    for i, r in enumerate(rows):
        out = out.at[i].set(r.astype(F32))
    return out


TM_ALL = 2064
TM_LN_ALL = 768
TM_FRAMES = 1024


def kernel(x, meta_tokens, even_w_in, even_f_bias, diff_lambda, diff_subln_g, even_w_out,
           ret_w_in, ret_w_out, ln_g, ln_b, ffn_w1, ffn_w2):
    pad = jnp.zeros((META_BLK - N_META, D_MODEL), F32)
    h0 = jnp.concatenate([x.reshape(NF, D_MODEL), meta_tokens.astype(F32), pad], axis=0)
    h0_16 = h0.astype(BF16)

    w_qa, w_ka, w_va, w_qb, w_kb, w_vb, w_fb = _even_weights(even_w_in[0])
    c_qa, c_ka, c_va, c_qb, c_kb, c_vb, c_fb = _even_coefs()
    heads_proj = functools.partial(_proj, h0_16, tm=TM_ALL)
    qa = heads_proj(w_qa, c_qa, heads=N_HM, head_w=LANES, name="proj_qa")
    ka = heads_proj(w_ka, c_ka, heads=N_HM, head_w=LANES, use_pos=True, name="proj_ka")
    va = heads_proj(w_va, c_va, heads=DIFF_HEADS, head_w=VA_W, name="proj_va")
    qb = heads_proj(w_qb, c_qb, heads=FOX_HEADS, head_w=LANES, name="proj_qb")
    kb = heads_proj(w_kb, c_kb, heads=FOX_HEADS, head_w=LANES, name="proj_kb")
    vb = heads_proj(w_vb, c_vb, heads=FOX_HEADS, head_w=VB_W, name="proj_vb")
    fb = heads_proj(w_fb, c_fb, out_dtype=F32, name="proj_fb")
    f_bias_row = _rows8(jnp.pad(even_f_bias[0], (0, LANES - FOX_HEADS)))
    kb = _fox_prep(fb, f_bias_row, kb)

    attn = _attention(qa, ka, va, qb, kb, vb, diff_lambda[0].astype(F32), _rows8(diff_subln_g[0]))

    w_out = even_w_out[0]
    w_out = jnp.concatenate(
        [w_out[:DIFF_HEADS * DIFF_V_DIM],
         _pad_heads(w_out[DIFF_HEADS * DIFF_V_DIM:].T, FOX_HEADS, FOX_HEAD_DIM, LANES).T], axis=0).astype(BF16)
    h1, h1_16 = _mm_res_ln(attn, w_out, h0, _rows8(ln_g[0, 0], ln_b[0, 0]), rows=NT, tm=TM_LN_ALL,
                           name="even_out_ln")
    f1 = _mm_relu2(h1_16, ffn_w1[0].astype(BF16), rows=NT, tm=TM_ALL, tn=1024, name="ffn0_up")
    h2, h2_16 = _mm_res_ln(f1, ffn_w2[0].astype(BF16), h1, _rows8(ln_g[0, 1], ln_b[0, 1]),
                           rows=NT, tm=TM_LN_ALL, name="ffn0_down_ln")

    qk_w = RET_HEADS * RET_QK_DIM
    v_w = RET_HEADS * RET_V_DIM
    rw = ret_w_in[0]
    zero_c = lambda n: jnp.zeros((8, n), F32)
    rq = _proj(h2_16, rw[:, :qk_w].astype(BF16), zero_c(qk_w), tm=TM_ALL, name="proj_ret_q")
    rkt = _proj_t((rw[:, qk_w:2 * qk_w] * RET_QK_DIM ** -0.5).T.astype(BF16), h2_16, tm=TM_LN_ALL)
    rv = _proj(h2_16, rw[:, 2 * qk_w:2 * qk_w + v_w].astype(BF16), zero_c(v_w), tm=TM_ALL, name="proj_ret_v")
    rg = _proj(h2_16, rw[:, 2 * qk_w + v_w:].astype(BF16), zero_c(v_w), tm=TM_ALL, name="proj_ret_g")
    y = _retention(rq, rkt, rv, rg)

    h3, h3_16 = _mm_res_ln(y, ret_w_out[0].astype(BF16), h2, _rows8(ln_g[1, 0], ln_b[1, 0]),
                           rows=NF, tm=TM_FRAMES, name="ret_out_ln")
    f2 = _mm_relu2(h3_16, ffn_w1[1].astype(BF16), rows=NF, tm=2048, tn=1024, name="ffn1_up")
    out, _ = _mm_res_ln(f2, ffn_w2[1].astype(BF16), h3, _rows8(ln_g[1, 1], ln_b[1, 1]),
                        rows=NF, tm=TM_FRAMES, name="ffn1_down_ln")
    return out.reshape(BATCH, SEQ, D_MODEL)
```

```python
import functools
import math

import jax
import jax.numpy as jnp
import numpy as np
from jax import lax
from jax.experimental import pallas as pl
from jax.experimental.pallas import tpu as pltpu

F32 = jnp.float32
BF16 = jnp.bfloat16

D_MODEL = 1024
BATCH = 4
SEQ = 8192
DEPTH = 2
CHUNK = 64
N_META = 16
DIFF_HEADS = 4
DIFF_HEAD_DIM = 64
DIFF_V_DIM = 128
FOX_HEADS = 8
FOX_HEAD_DIM = 64
RET_HEADS = 4
RET_QK_DIM = 256
RET_V_DIM = 512
D_FF = 4 * D_MODEL
DEEPNORM_ALPHA = (2 * DEPTH) ** 0.25
LN_EPS = 1e-5
RMS_EPS = 1e-6
LAM_INIT0 = 0.8 - 0.6 * math.exp(-0.3 * 0)
LOG2E = math.log2(math.e)

NF = BATCH * SEQ
META_BLK = 256
NT = NF + META_BLK
LANES = 128
NEG = -1e30

ATT_T = 512
META_KEYS = 128
RET_T = 512
VMEM_LIMIT = 56 * 1024 * 1024


def _cparams(sem):
    return pltpu.CompilerParams(dimension_semantics=sem, vmem_limit_bytes=VMEM_LIMIT)


def _bf16_pieces(x, n=3):
    out = []
    r = np.float32(x)
    for _ in range(n):
        p = np.float32(np.asarray(r, dtype=BF16).astype(np.float32))
        out.append(float(p))
        r = np.float32(r - p)
    return out


LOG2E_PIECES = _bf16_pieces(LOG2E)


def _row_positions(i, tm):
    r = i * tm + lax.broadcasted_iota(jnp.int32, (tm, 1), 0)
    is_frame = r < NF
    m = r - NF
    pos = jnp.where(is_frame, (r & (SEQ - 1)) + N_META, m)
    valid = jnp.logical_or(is_frame, m < N_META)
    return pos, valid


def _proj_kernel(a_ref, w_ref, c_ref, o_ref, *, tm, heads, head_w, use_pos):
    y = jnp.dot(a_ref[...], w_ref[...], preferred_element_type=F32)
    y = y * c_ref[4:5, :] + c_ref[0:1, :]
    if use_pos:
        pos, valid = _row_positions(pl.program_id(0), tm)
        hi = (pos >> 7).astype(F32)
        lo = (pos & 127).astype(F32)
        y = y + hi * c_ref[1:2, :] + lo * c_ref[2:3, :] + jnp.where(valid, 0.0, 1.0) * c_ref[3:4, :]
    if heads is None:
        o_ref[...] = y.astype(o_ref.dtype)
    else:
        for h in range(heads):
            o_ref[h] = y[:, h * head_w:(h + 1) * head_w].astype(o_ref.dtype)


def _proj(a, w, coef, *, tm, name, heads=None, head_w=None, use_pos=False, out_dtype=BF16):
    m, k = a.shape
    n = w.shape[1]
    if heads is None:
        out_shape = jax.ShapeDtypeStruct((m, n), out_dtype)
        out_spec = pl.BlockSpec((tm, n), lambda i: (i, 0))
    else:
        out_shape = jax.ShapeDtypeStruct((heads, m, head_w), out_dtype)
        out_spec = pl.BlockSpec((heads, tm, head_w), lambda i: (0, i, 0))
    return pl.pallas_call(
        functools.partial(_proj_kernel, tm=tm, heads=heads, head_w=head_w, use_pos=use_pos),
        grid=(m // tm,),
        in_specs=[pl.BlockSpec((tm, k), lambda i: (i, 0)),
                  pl.BlockSpec((k, n), lambda i: (0, 0), pipeline_mode=pl.Buffered(1)),
                  pl.BlockSpec((8, n), lambda i: (0, 0))],
        out_specs=out_spec,
        out_shape=out_shape,
        compiler_params=_cparams(("arbitrary",)),
        name=name,
    )(a, w, coef)


def _proj_t_kernel(wt_ref, a_ref, o_ref):
    o_ref[...] = lax.dot_general(wt_ref[...], a_ref[...], (((1,), (1,)), ((), ())),
                                 preferred_element_type=F32).astype(o_ref.dtype)


def _proj_t(wt, a, *, tm):
    n, k = wt.shape
    m = a.shape[0]
    return pl.pallas_call(
        _proj_t_kernel,
        grid=(m // tm,),
        in_specs=[pl.BlockSpec((n, k), lambda i: (0, 0)),
                  pl.BlockSpec((tm, k), lambda i: (i, 0))],
        out_specs=pl.BlockSpec((n, tm), lambda i: (0, i)),
        out_shape=jax.ShapeDtypeStruct((n, m), BF16),
        compiler_params=_cparams(("arbitrary",)),
        name="proj_ret_kt",
    )(wt, a)


def _relu2_kernel(a_ref, w_ref, o_ref):
    y = jnp.dot(a_ref[...], w_ref[...], preferred_element_type=F32)
    y = jnp.maximum(y, 0.0)
    o_ref[...] = (y * y).astype(o_ref.dtype)


def _mm_relu2(a, w, *, rows, tm, tn, name):
    k = a.shape[1]
    n = w.shape[1]
    return pl.pallas_call(
        _relu2_kernel,
        grid=(rows // tm, n // tn),
        in_specs=[pl.BlockSpec((tm, k), lambda i, j: (i, 0)),
                  pl.BlockSpec((k, tn), lambda i, j: (0, j))],
        out_specs=pl.BlockSpec((tm, tn), lambda i, j: (i, j)),
        out_shape=jax.ShapeDtypeStruct((rows, n), BF16),
        compiler_params=_cparams(("arbitrary", "arbitrary")),
        name=name,
    )(a, w)


def _res_ln_kernel(a_ref, w_ref, r_ref, gb_ref, o32_ref, o16_ref):
    y = jnp.dot(a_ref[...], w_ref[...], preferred_element_type=F32)
    z = DEEPNORM_ALPHA * r_ref[...] + y
    mu = jnp.mean(z, axis=-1, keepdims=True)
    zc = z - mu
    var = jnp.mean(zc * zc, axis=-1, keepdims=True)
    out = zc * lax.rsqrt(var + LN_EPS) * gb_ref[0:1, :] + gb_ref[1:2, :]
    o32_ref[...] = out
    o16_ref[...] = out.astype(BF16)


def _mm_res_ln(a, w, res, gb, *, rows, tm, name):
    k = a.shape[1]
    n = w.shape[1]
    return pl.pallas_call(
        _res_ln_kernel,
        grid=(rows // tm,),
        in_specs=[pl.BlockSpec((tm, k), lambda i: (i, 0)),
                  pl.BlockSpec((k, n), lambda i: (0, 0), pipeline_mode=pl.Buffered(1)),
                  pl.BlockSpec((tm, n), lambda i: (i, 0)),
                  pl.BlockSpec((8, n), lambda i: (0, 0))],
        out_specs=[pl.BlockSpec((tm, n), lambda i: (i, 0)),
                   pl.BlockSpec((tm, n), lambda i: (i, 0))],
        out_shape=[jax.ShapeDtypeStruct((rows, n), F32),
                   jax.ShapeDtypeStruct((rows, n), BF16)],
        compiler_params=_cparams(("arbitrary",)),
        name=name,
    )(a, w, res, gb)


N_HM = 2 * DIFF_HEADS
N_HT = N_HM + FOX_HEADS
N_VT = DIFF_HEADS + FOX_HEADS
V_W = 2 * LANES
BIAS_LANE0 = DIFF_HEAD_DIM
DIFF_BETA_PIECES = 2
FOX_BETA_PIECES = 3
ATT_OUT_W = DIFF_HEADS * DIFF_V_DIM + FOX_HEADS * LANES
SLOPES = [2.0 ** (-8.0 * (h + 1) / DIFF_HEADS) for h in range(DIFF_HEADS)]


def _pad_heads(w, heads, width, pad_to):
    k = w.shape[0]
    w = w.reshape(k, heads, width)
    w = jnp.pad(w, ((0, 0), (0, 0), (0, pad_to - width)))
    return w.reshape(k, heads * pad_to)


def _coef_rows(n, rows):
    c = np.zeros((8, n), np.float32)
    c[4, :] = 1.0
    for r, vals in rows.items():
        for col, val in vals:
            c[r, col] = val
    return jnp.asarray(c)


def _even_weights(w_in):
    qk_w = DIFF_HEADS * 2 * DIFF_HEAD_DIM
    v_w = DIFF_HEADS * DIFF_V_DIM
    fox_w = FOX_HEADS * FOX_HEAD_DIM
    o = np.cumsum([0, qk_w, qk_w, v_w, fox_w, fox_w, fox_w, FOX_HEADS])
    sl = [w_in[:, o[i]:o[i + 1]] for i in range(7)]
    w_q = jnp.concatenate([_pad_heads(sl[0], N_HM, DIFF_HEAD_DIM, LANES),
                           _pad_heads(sl[3], FOX_HEADS, FOX_HEAD_DIM, LANES)], axis=1).astype(BF16)
    w_k = jnp.concatenate([_pad_heads(sl[1], N_HM, DIFF_HEAD_DIM, LANES),
                           _pad_heads(sl[4], FOX_HEADS, FOX_HEAD_DIM, LANES)], axis=1).astype(BF16)
    w_v = jnp.concatenate([_pad_heads(sl[2], DIFF_HEADS, DIFF_V_DIM, V_W),
                           _pad_heads(sl[5], FOX_HEADS, FOX_HEAD_DIM, V_W)], axis=1).astype(BF16)
    w_fb = jnp.pad(sl[6], ((0, 0), (0, LANES - FOX_HEADS))).astype(BF16)
    return w_q, w_k, w_v, w_fb


def _even_coefs():
    n = N_HT * LANES
    q_scale = DIFF_HEAD_DIM ** -0.5 * LOG2E
    q_const, q_scl, k_hi, k_lo, k_pad = [], [], [], [], []
    for ht in range(N_HT):
        base = ht * LANES
        q_scl += [(base + d, q_scale) for d in range(DIFF_HEAD_DIM)]
        pieces = DIFF_BETA_PIECES if ht < N_HM else FOX_BETA_PIECES
        for p in range(pieces):
            for r in range(3):
                q_const.append((base + BIAS_LANE0 + 3 * p + r, LOG2E_PIECES[r]))
        if ht < N_HM:
            slope = SLOPES[ht // 2]
            k_hi += [(base + BIAS_LANE0 + r, slope * 128.0) for r in range(3)]
            k_lo += [(base + BIAS_LANE0 + 3 + r, slope) for r in range(3)]
            k_pad.append((base + BIAS_LANE0, NEG))
    c_q = _coef_rows(n, {0: q_const, 4: q_scl})
    c_k = _coef_rows(n, {1: k_hi, 2: k_lo, 3: k_pad})
    ones = [(h * V_W + DIFF_V_DIM, 1.0) for h in range(DIFF_HEADS)]
    ones += [((DIFF_HEADS + h) * V_W + FOX_HEAD_DIM, 1.0) for h in range(FOX_HEADS)]
    c_v = _coef_rows(N_VT * V_W, {0: ones})
    c_fb = _coef_rows(LANES, {})
    return c_q, c_k, c_v, c_fb


PREP_T = 256


def _split3(x):
    hi = x.astype(BF16)
    r1 = x - hi.astype(F32)
    mid = r1.astype(BF16)
    lo = (r1 - mid.astype(F32)).astype(BF16)
    return hi, mid, lo


def _fox_prep_kernel(fb_ref, bias_ref, tri_ref, sel_ref, kb_ref, o_ref, carry_ref):
    s = pl.program_id(0)
    steps_per_batch = SEQ // PREP_T
    is_meta = s == NF // PREP_T

    @pl.when(jnp.logical_or(s % steps_per_batch == 0, is_meta))
    def _():
        carry_ref[...] = jnp.zeros_like(carry_ref)

    x = fb_ref[...] + bias_ref[0:1, :]
    logf = jnp.minimum(x, 0.0) - jnp.log1p(jnp.exp(-jnp.abs(x)))
    tri = tri_ref[...]
    cum = carry_ref[0:1, :]
    for piece in _split3(logf):
        cum = cum + jnp.dot(tri, piece, preferred_element_type=F32)
    carry_ref[0:1, :] = cum[PREP_T - 1:PREP_T, :]

    row = lax.broadcasted_iota(jnp.int32, (PREP_T, 1), 0)
    meta_g = jnp.where(row < N_META, cum[N_META - 1:N_META, :] - cum, NEG)
    g = jnp.where(is_meta, meta_g, -cum)

    aug = jnp.zeros((PREP_T, FOX_HEADS * LANES), F32)
    for p, piece in enumerate(_split3(g)):
        aug = aug + jnp.dot(piece, sel_ref[p], preferred_element_type=F32)
    for h in range(FOX_HEADS):
        o_ref[h] = (kb_ref[h].astype(F32) + aug[:, h * LANES:(h + 1) * LANES]).astype(BF16)


def _fox_prep(fb, bias_row, k_all):
    tri = jnp.tril(jnp.ones((PREP_T, PREP_T), F32)).astype(BF16)
    sel = np.zeros((FOX_BETA_PIECES, LANES, FOX_HEADS * LANES), np.float32)
    for p in range(FOX_BETA_PIECES):
        for h in range(FOX_HEADS):
            for r in range(3):
                sel[p, h, h * LANES + BIAS_LANE0 + 3 * p + r] = 1.0
    sel = jnp.asarray(sel, BF16)
    fox_blk = pl.BlockSpec((FOX_HEADS, PREP_T, LANES), lambda s: (1, s, 0))
    return pl.pallas_call(
        _fox_prep_kernel,
        grid=(NT // PREP_T,),
        in_specs=[pl.BlockSpec((PREP_T, LANES), lambda s: (s, 0)),
                  pl.BlockSpec((8, LANES), lambda s: (0, 0)),
                  pl.BlockSpec((PREP_T, PREP_T), lambda s: (0, 0)),
                  pl.BlockSpec((FOX_BETA_PIECES, LANES, FOX_HEADS * LANES), lambda s: (0, 0, 0)),
                  fox_blk],
        out_specs=fox_blk,
        out_shape=jax.ShapeDtypeStruct(k_all.shape, BF16),
        scratch_shapes=[pltpu.VMEM((8, LANES), F32)],
        input_output_aliases={4: 0},
        compiler_params=_cparams(("arbitrary",)),
        name="fox_prep",
    )(fb, bias_row, tri, sel, k_all)


def _v_index(ht):
    return jnp.where(ht < N_HM, ht >> 1, ht - DIFF_HEADS)


def _fill_masks(ahead_ref, mask_ref, *, tq, tk, pos_off):
    i = lax.broadcasted_iota(jnp.int32, (tq, tk), 0)
    j = lax.broadcasted_iota(jnp.int32, (tq, tk), 1)
    ahead_ref[...] = jnp.maximum(j - i, 0).astype(F32)
    mask_ref[0] = jnp.where(((j + pos_off) >> 6) > ((i + pos_off) >> 6), NEG, 0.0)
    mask_ref[1] = jnp.where(j > i, NEG, 0.0)


def _attn_init(m_ref, acc_ref):
    m_ref[...] = jnp.full(m_ref.shape, NEG, F32)
    acc_ref[...] = jnp.zeros(acc_ref.shape, F32)


PIPE_U = 2
N_GROUPS = N_HT // PIPE_U


def _attn_pass(q_ref, k_ref, v_ref, m_ref, acc_ref, bufs, coef_ref, ahead_ref, mask_ref, *, tk, masked):
    s_bufs, p_bufs, a_bufs = bufs
    sl = (slice(None), slice(0, tk))

    def stage_qk(ht, s_buf):
        s = lax.dot_general(q_ref[ht], k_ref[ht], (((1,), (1,)), ((), ())), preferred_element_type=F32)
        if masked:
            s = s + (ahead_ref[...] * coef_ref[ht] + mask_ref[jnp.where(ht < N_HM, 0, 1)])
        s_buf[sl] = s

    def stage_softmax(ht, s_buf, p_buf, a_buf):
        s = s_buf[sl]
        m_prev = m_ref[ht]
        m_new = jnp.maximum(m_prev, jnp.max(s, axis=1, keepdims=True))
        a_buf[...] = jnp.exp2(m_prev - m_new)
        p_buf[sl] = jnp.exp2(s - m_new).astype(BF16)
        m_ref[ht] = m_new

    def stage_pv(ht, p_buf, a_buf):
        pv = jnp.dot(p_buf[sl], v_ref[_v_index(ht)], preferred_element_type=F32)
        acc_ref[ht] = a_buf[...] * acc_ref[ht] + pv

    def qk_group(g, par):
        for u in range(PIPE_U):
            stage_qk(PIPE_U * g + u, s_bufs[par][u])

    def softmax_group(g, par):
        for u in range(PIPE_U):
            stage_softmax(PIPE_U * g + u, s_bufs[par][u], p_bufs[par][u], a_bufs[par][u])

    def pv_group(g, par):
        for u in range(PIPE_U):
            stage_pv(PIPE_U * g + u, p_bufs[par][u], a_bufs[par][u])

    def full_body(g, par):
        qk_group(g + 2, par)
        pv_group(g, par)
        softmax_group(g + 1, 1 - par)

    qk_group(0, 0)
    qk_group(1, 1)
    softmax_group(0, 0)

    def body(j, c):
        full_body(2 * j, 0)
        full_body(2 * j + 1, 1)
        return c

    lax.fori_loop(0, (N_GROUPS - 2) // 2, body, 0)
    pv_group(N_GROUPS - 2, 0)
    softmax_group(N_GROUPS - 1, 1)
    pv_group(N_GROUPS - 1, 1)


def _attn_finalize(lam_ref, g_ref, acc_ref, out_ref):
    lv = lam_ref[...]
    lam = (jnp.exp(jnp.sum(lv[0:1] * lv[1:2], axis=1, keepdims=True))
           - jnp.exp(jnp.sum(lv[2:3] * lv[3:4], axis=1, keepdims=True)) + LAM_INIT0)
    gain = g_ref[0:1, :] * (1.0 - LAM_INIT0)
    for h in range(DIFF_HEADS):
        a0 = acc_ref[2 * h]
        a1 = acc_ref[2 * h + 1]
        o = (a0[:, :DIFF_V_DIM] / a0[:, DIFF_V_DIM:DIFF_V_DIM + 1]
             - lam * (a1[:, :DIFF_V_DIM] / a1[:, DIFF_V_DIM:DIFF_V_DIM + 1]))
        ms = jnp.mean(o * o, axis=1, keepdims=True)
        out_ref[:, h * DIFF_V_DIM:(h + 1) * DIFF_V_DIM] = (o * lax.rsqrt(ms + RMS_EPS) * gain).astype(BF16)
    base = DIFF_HEADS * DIFF_V_DIM
    for h in range(FOX_HEADS):
        a = acc_ref[N_HM + h][:, :LANES]
        out_ref[:, base + h * LANES:base + (h + 1) * LANES] = (
            a / a[:, FOX_HEAD_DIM:FOX_HEAD_DIM + 1]).astype(BF16)


def _attn_kernel(qi_ref, kv_ref, coef_ref,
                 q_ref, k_ref, v_ref, km_ref, vm_ref, lam_ref, g_ref, prev_ref,
                 out_ref, m_ref, acc_ref, ahead_ref, mask_ref, *pipe_refs):
    del prev_ref
    step = pl.program_id(1)
    qi = qi_ref[step]
    kv = kv_ref[step]
    bufs = _pipe_bufs(pipe_refs)

    @pl.when(jnp.logical_and(pl.program_id(0) == 0, step == 0))
    def _():
        _fill_masks(ahead_ref, mask_ref, tq=ATT_T, tk=ATT_T, pos_off=0)

    @pl.when(kv == 0)
    def _():
        _attn_init(m_ref, acc_ref)
        _attn_pass(q_ref, km_ref, vm_ref, m_ref, acc_ref, bufs, coef_ref, ahead_ref, mask_ref,
                   tk=META_KEYS, masked=False)

    @pl.when(kv < qi)
    def _():
        _attn_pass(q_ref, k_ref, v_ref, m_ref, acc_ref, bufs, coef_ref, ahead_ref, mask_ref,
                   tk=ATT_T, masked=False)

    @pl.when(kv == qi)
    def _():
        _attn_pass(q_ref, k_ref, v_ref, m_ref, acc_ref, bufs, coef_ref, ahead_ref, mask_ref,
                   tk=ATT_T, masked=True)
        _attn_finalize(lam_ref, g_ref, acc_ref, out_ref)


def _meta_attn_kernel(coef_ref, q_ref, k_ref, v_ref, lam_ref, g_ref, out_ref,
                      m_ref, acc_ref, ahead_ref, mask_ref, *pipe_refs):
    _fill_masks(ahead_ref, mask_ref, tq=META_BLK, tk=META_BLK, pos_off=CHUNK - N_META)
    _attn_init(m_ref, acc_ref)
    _attn_pass(q_ref, k_ref, v_ref, m_ref, acc_ref, _pipe_bufs(pipe_refs), coef_ref, ahead_ref, mask_ref,
               tk=META_BLK, masked=True)
    _attn_finalize(lam_ref, g_ref, acc_ref, out_ref)


N_PIPE_BUFS = 2 * PIPE_U


def _pipe_bufs(refs):
    n = N_PIPE_BUFS
    nest = lambda flat: [list(flat[:PIPE_U]), list(flat[PIPE_U:])]
    return nest(refs[:n]), nest(refs[n:2 * n]), nest(refs[2 * n:3 * n])


def _attn_scratch(tq, tk):
    return ([pltpu.VMEM((N_HT, tq, 1), F32), pltpu.VMEM((N_HT, tq, V_W), F32),
             pltpu.VMEM((tq, tk), F32), pltpu.VMEM((2, tq, tk), F32)]
            + [pltpu.VMEM((tq, tk), F32)] * N_PIPE_BUFS
            + [pltpu.VMEM((tq, tk), BF16)] * N_PIPE_BUFS
            + [pltpu.VMEM((tq, 1), F32)] * N_PIPE_BUFS)


def _attention(q_all, k_all, v_all, lam, gain_row):
    coef = jnp.asarray([-2.0 * SLOPES[ht // 2] * LOG2E if ht < N_HM else 0.0 for ht in range(N_HT)], F32)
    smem = pl.BlockSpec(memory_space=pltpu.SMEM)
    meta_blk = NF // META_BLK

    def meta_spec(heads, w):
        return pl.BlockSpec((heads, META_BLK, w), lambda *_: (0, meta_blk, 0))

    const2 = lambda shape: pl.BlockSpec(shape, lambda *_: (0, 0))

    out0 = pl.pallas_call(
        _meta_attn_kernel,
        grid=(1,),
        in_specs=[smem, meta_spec(N_HT, LANES), meta_spec(N_HT, LANES), meta_spec(N_VT, V_W),
                  const2((4, DIFF_HEAD_DIM)), const2((8, LANES))],
        out_specs=pl.BlockSpec((META_BLK, ATT_OUT_W), lambda s: (meta_blk, 0)),
        out_shape=jax.ShapeDtypeStruct((NT, ATT_OUT_W), BF16),
        scratch_shapes=_attn_scratch(META_BLK, META_BLK),
        compiler_params=_cparams(("arbitrary",)),
        name="attn_meta",
    )(coef, q_all, k_all, v_all, lam, gain_row)

    n_q = SEQ // ATT_T
    qi_tab = np.concatenate([np.full(q + 1, q, np.int32) for q in range(n_q)])
    kv_tab = np.concatenate([np.arange(q + 1, dtype=np.int32) for q in range(n_q)])

    def q_spec(heads, w):
        return pl.BlockSpec((heads, ATT_T, w), lambda b, s, qi, kv: (0, b * n_q + qi[s], 0))

    def kv_spec(heads, w):
        return pl.BlockSpec((heads, ATT_T, w), lambda b, s, qi, kv: (0, b * n_q + kv[s], 0))

    def meta_keys_spec(heads, w):
        return pl.BlockSpec((heads, META_KEYS, w), lambda *_: (0, NF // META_KEYS, 0))

    grid_spec = pltpu.PrefetchScalarGridSpec(
        num_scalar_prefetch=2,
        grid=(BATCH, len(qi_tab)),
        in_specs=[smem, q_spec(N_HT, LANES), kv_spec(N_HT, LANES), kv_spec(N_VT, V_W),
                  meta_keys_spec(N_HT, LANES), meta_keys_spec(N_VT, V_W),
                  const2((4, DIFF_HEAD_DIM)), const2((8, LANES)),
                  pl.BlockSpec(memory_space=pl.ANY)],
        out_specs=pl.BlockSpec((ATT_T, ATT_OUT_W), lambda b, s, qi, kv: (b * n_q + qi[s], 0)),
        scratch_shapes=_attn_scratch(ATT_T, ATT_T),
    )
    return pl.pallas_call(
        _attn_kernel,
        grid_spec=grid_spec,
        out_shape=jax.ShapeDtypeStruct((NT, ATT_OUT_W), BF16),
        input_output_aliases={10: 0},
        compiler_params=_cparams(("arbitrary", "arbitrary")),
        name="attn_frames",
    )(jnp.asarray(qi_tab), jnp.asarray(kv_tab), coef,
      q_all, k_all, v_all, k_all, v_all, lam, gain_row, out0)


def _log_gamma(h):
    return jnp.log1p(jnp.full((1, 1), -(2.0 ** (-5.0 - h)), F32))


def _retention_kernel(q_ref, kt_ref, v_ref, g_ref, ktm_ref, vm_ref, o_ref, state_ref, decay_ref):
    b = pl.program_id(0)
    t = pl.program_id(1)
    T = RET_T

    @pl.when(jnp.logical_and(b == 0, t == 0))
    def _():
        i = lax.broadcasted_iota(jnp.int32, (T, T), 0)
        j = lax.broadcasted_iota(jnp.int32, (T, T), 1)
        dist = jnp.abs(i - j).astype(F32)
        vis = (j >> 6) <= (i >> 6)
        for h in range(RET_HEADS):
            decay_ref[h] = jnp.where(vis, jnp.exp(_log_gamma(h) * dist), 0.0)

    @pl.when(t == 0)
    def _():
        m = lax.broadcasted_iota(jnp.int32, (1, META_BLK), 1)
        for h in range(RET_HEADS):
            w = jnp.exp(_log_gamma(h) * (N_META - 1 - m).astype(F32))
            kt = ktm_ref[h * RET_QK_DIM:(h + 1) * RET_QK_DIM, :].astype(F32)
            kd = jnp.where(m < N_META, kt * w, 0.0).astype(BF16)
            state_ref[h] = jnp.dot(kd, vm_ref[:, h * RET_V_DIM:(h + 1) * RET_V_DIM],
                                   preferred_element_type=F32)

    row = lax.broadcasted_iota(jnp.int32, (T, 1), 0).astype(F32)
    col = lax.broadcasted_iota(jnp.int32, (1, T), 1).astype(F32)
    for h in range(RET_HEADS):
        lg = _log_gamma(h)
        q = q_ref[:, h * RET_QK_DIM:(h + 1) * RET_QK_DIM]
        kt = kt_ref[h * RET_QK_DIM:(h + 1) * RET_QK_DIM, :]
        v = v_ref[:, h * RET_V_DIM:(h + 1) * RET_V_DIM]
        scores = jnp.dot(q, kt, preferred_element_type=F32) * decay_ref[h]
        o = jnp.dot(scores.astype(BF16), v, preferred_element_type=F32)
        qd = (q.astype(F32) * jnp.exp(lg * (row + 1.0))).astype(BF16)
        state = state_ref[h]
        o = o + jnp.dot(qd, state.astype(BF16), preferred_element_type=F32)
        kd = (kt.astype(F32) * jnp.exp(lg * (T - 1.0 - col))).astype(BF16)
        state_ref[h] = jnp.exp(lg * float(T)) * state + jnp.dot(kd, v, preferred_element_type=F32)
        o = o * lax.rsqrt(jnp.mean(o * o, axis=1, keepdims=True) + RMS_EPS)
        gate = g_ref[:, h * RET_V_DIM:(h + 1) * RET_V_DIM].astype(F32)
        o_ref[:, h * RET_V_DIM:(h + 1) * RET_V_DIM] = (gate * jax.nn.sigmoid(gate) * o).astype(BF16)


def _retention(q, kt, v, g):
    n_t = SEQ // RET_T
    qk_w = RET_HEADS * RET_QK_DIM
    v_w = RET_HEADS * RET_V_DIM
    meta_blk = NF // META_BLK
    return pl.pallas_call(
        _retention_kernel,
        grid=(BATCH, n_t),
        in_specs=[pl.BlockSpec((RET_T, qk_w), lambda b, t: (b * n_t + t, 0)),
                  pl.BlockSpec((qk_w, RET_T), lambda b, t: (0, b * n_t + t)),
                  pl.BlockSpec((RET_T, v_w), lambda b, t: (b * n_t + t, 0)),
                  pl.BlockSpec((RET_T, v_w), lambda b, t: (b * n_t + t, 0)),
                  pl.BlockSpec((qk_w, META_BLK), lambda b, t: (0, meta_blk)),
                  pl.BlockSpec((META_BLK, v_w), lambda b, t: (meta_blk, 0))],
        out_specs=pl.BlockSpec((RET_T, v_w), lambda b, t: (b * n_t + t, 0)),
        out_shape=jax.ShapeDtypeStruct((NF, v_w), BF16),
        scratch_shapes=[pltpu.VMEM((RET_HEADS, RET_QK_DIM, RET_V_DIM), F32),
                        pltpu.VMEM((RET_HEADS, RET_T, RET_T), F32)],
        compiler_params=_cparams(("arbitrary", "arbitrary")),
        name="retention",
    )(q, kt, v, g, kt, v)


def _rows8(*rows):
    n = rows[0].shape[-1]
    out = jnp.zeros((8, n), F32)
    for i, r in enumerate(rows):
        out = out.at[i].set(r.astype(F32))
    return out


def _unit_coef(n):
    return _coef_rows(n, {})


TM_ALL = 2064
TM_LN_ALL = 768
TM_FRAMES = 1024


def kernel(x, meta_tokens, even_w_in, even_f_bias, diff_lambda, diff_subln_g, even_w_out,
           ret_w_in, ret_w_out, ln_g, ln_b, ffn_w1, ffn_w2):
    pad = jnp.zeros((META_BLK - N_META, D_MODEL), F32)
    h0 = jnp.concatenate([x.reshape(NF, D_MODEL), meta_tokens.astype(F32), pad], axis=0)
    h0_16 = h0.astype(BF16)

    w_q, w_k, w_v, w_fb = _even_weights(even_w_in[0])
    c_q, c_k, c_v, c_fb = _even_coefs()
    q_all = _proj(h0_16, w_q, c_q, tm=TM_LN_ALL, heads=N_HT, head_w=LANES, name="proj_q")
    k_all = _proj(h0_16, w_k, c_k, tm=TM_LN_ALL, heads=N_HT, head_w=LANES, use_pos=True, name="proj_k")
    v_all = _proj(h0_16, w_v, c_v, tm=TM_LN_ALL, heads=N_VT, head_w=V_W, name="proj_v")
    fb = _proj(h0_16, w_fb, c_fb, tm=TM_ALL, out_dtype=F32, name="proj_fb")
    f_bias_row = _rows8(jnp.pad(even_f_bias[0], (0, LANES - FOX_HEADS)))
    k_all = _fox_prep(fb, f_bias_row, k_all)

    attn = _attention(q_all, k_all, v_all, diff_lambda[0].astype(F32), _rows8(diff_subln_g[0]))

    w_out = even_w_out[0]
    w_out = jnp.concatenate(
        [w_out[:DIFF_HEADS * DIFF_V_DIM],
         _pad_heads(w_out[DIFF_HEADS * DIFF_V_DIM:].T, FOX_HEADS, FOX_HEAD_DIM, LANES).T], axis=0).astype(BF16)
    h1, h1_16 = _mm_res_ln(attn, w_out, h0, _rows8(ln_g[0, 0], ln_b[0, 0]), rows=NT, tm=TM_LN_ALL,
                           name="even_out_ln")
    f1 = _mm_relu2(h1_16, ffn_w1[0].astype(BF16), rows=NT, tm=TM_ALL, tn=1024, name="ffn0_up")
    h2, h2_16 = _mm_res_ln(f1, ffn_w2[0].astype(BF16), h1, _rows8(ln_g[0, 1], ln_b[0, 1]),
                           rows=NT, tm=TM_LN_ALL, name="ffn0_down_ln")

    qk_w = RET_HEADS * RET_QK_DIM
    v_w = RET_HEADS * RET_V_DIM
    rw = ret_w_in[0]
    rq = _proj(h2_16, rw[:, :qk_w].astype(BF16), _unit_coef(qk_w), tm=TM_ALL, name="proj_ret_q")
    rkt = _proj_t((rw[:, qk_w:2 * qk_w] * RET_QK_DIM ** -0.5).T.astype(BF16), h2_16, tm=TM_LN_ALL)
    rv = _proj(h2_16, rw[:, 2 * qk_w:2 * qk_w + v_w].astype(BF16), _unit_coef(v_w), tm=TM_ALL, name="proj_ret_v")
    rg = _proj(h2_16, rw[:, 2 * qk_w + v_w:].astype(BF16), _unit_coef(v_w), tm=TM_ALL, name="proj_ret_g")
    y = _retention(rq, rkt, rv, rg)

    h3, h3_16 = _mm_res_ln(y, ret_w_out[0].astype(BF16), h2, _rows8(ln_g[1, 0], ln_b[1, 0]),
                           rows=NF, tm=TM_FRAMES, name="ret_out_ln")
    f2 = _mm_relu2(h3_16, ffn_w1[1].astype(BF16), rows=NF, tm=2048, tn=1024, name="ffn1_up")
    out, _ = _mm_res_ln(f2, ffn_w2[1].astype(BF16), h3, _rows8(ln_g[1, 1], ln_b[1, 1]),
                        rows=NF, tm=TM_FRAMES, name="ffn1_down_ln")
    return out.reshape(BATCH, SEQ, D_MODEL)
```

```python
import functools
import math

import jax
import jax.numpy as jnp
import numpy as np
from jax import lax
from jax.experimental import pallas as pl
from jax.experimental.pallas import tpu as pltpu

F32 = jnp.float32
BF16 = jnp.bfloat16

D_MODEL = 1024
BATCH = 4
SEQ = 8192
DEPTH = 2
CHUNK = 64
N_META = 16
DIFF_HEADS = 4
DIFF_HEAD_DIM = 64
DIFF_V_DIM = 128
FOX_HEADS = 8
FOX_HEAD_DIM = 64
RET_HEADS = 4
RET_QK_DIM = 256
RET_V_DIM = 512
D_FF = 4 * D_MODEL
DEEPNORM_ALPHA = (2 * DEPTH) ** 0.25
LN_EPS = 1e-5
RMS_EPS = 1e-6
LAM_INIT0 = 0.8 - 0.6 * math.exp(-0.3 * 0)
LOG2E = math.log2(math.e)

NF = BATCH * SEQ
META_BLK = 256
NT = NF + META_BLK
LANES = 128
NEG = -1e30

ATT_T = 512
META_KEYS = 128
RET_T = 512
VMEM_LIMIT = 56 * 1024 * 1024


def _cparams(sem):
    return pltpu.CompilerParams(dimension_semantics=sem, vmem_limit_bytes=VMEM_LIMIT)


def _bf16_pieces(x, n=3):
    out = []
    r = np.float32(x)
    for _ in range(n):
        p = np.float32(np.asarray(r, dtype=BF16).astype(np.float32))
        out.append(float(p))
        r = np.float32(r - p)
    return out


LOG2E_PIECES = _bf16_pieces(LOG2E)


def _row_positions(i, tm):
    r = i * tm + lax.broadcasted_iota(jnp.int32, (tm, 1), 0)
    is_frame = r < NF
    m = r - NF
    pos = jnp.where(is_frame, (r & (SEQ - 1)) + N_META, m)
    valid = jnp.logical_or(is_frame, m < N_META)
    return pos, valid


def _proj_kernel(a_ref, w_ref, c_ref, o_ref, *, tm, heads, head_w, use_pos):
    y = jnp.dot(a_ref[...], w_ref[...], preferred_element_type=F32)
    y = y * c_ref[4:5, :] + c_ref[0:1, :]
    if use_pos:
        pos, valid = _row_positions(pl.program_id(0), tm)
        hi = (pos >> 7).astype(F32)
        lo = (pos & 127).astype(F32)
        y = y + hi * c_ref[1:2, :] + lo * c_ref[2:3, :] + jnp.where(valid, 0.0, 1.0) * c_ref[3:4, :]
    if heads is None:
        o_ref[...] = y.astype(o_ref.dtype)
    else:
        for h in range(heads):
            o_ref[h] = y[:, h * head_w:(h + 1) * head_w].astype(o_ref.dtype)


def _proj(a, w, coef, *, tm, name, heads=None, head_w=None, use_pos=False, out_dtype=BF16):
    m, k = a.shape
    n = w.shape[1]
    if heads is None:
        out_shape = jax.ShapeDtypeStruct((m, n), out_dtype)
        out_spec = pl.BlockSpec((tm, n), lambda i: (i, 0))
    else:
        out_shape = jax.ShapeDtypeStruct((heads, m, head_w), out_dtype)
        out_spec = pl.BlockSpec((heads, tm, head_w), lambda i: (0, i, 0))
    return pl.pallas_call(
        functools.partial(_proj_kernel, tm=tm, heads=heads, head_w=head_w, use_pos=use_pos),
        grid=(m // tm,),
        in_specs=[pl.BlockSpec((tm, k), lambda i: (i, 0)),
                  pl.BlockSpec((k, n), lambda i: (0, 0), pipeline_mode=pl.Buffered(1)),
                  pl.BlockSpec((8, n), lambda i: (0, 0))],
        out_specs=out_spec,
        out_shape=out_shape,
        compiler_params=_cparams(("arbitrary",)),
        name=name,
    )(a, w, coef)


def _proj_t_kernel(wt_ref, a_ref, o_ref):
    o_ref[...] = lax.dot_general(wt_ref[...], a_ref[...], (((1,), (1,)), ((), ())),
                                 preferred_element_type=F32).astype(o_ref.dtype)


def _proj_t(wt, a, *, tm):
    n, k = wt.shape
    m = a.shape[0]
    return pl.pallas_call(
        _proj_t_kernel,
        grid=(m // tm,),
        in_specs=[pl.BlockSpec((n, k), lambda i: (0, 0)),
                  pl.BlockSpec((tm, k), lambda i: (i, 0))],
        out_specs=pl.BlockSpec((n, tm), lambda i: (0, i)),
        out_shape=jax.ShapeDtypeStruct((n, m), BF16),
        compiler_params=_cparams(("arbitrary",)),
        name="proj_ret_kt",
    )(wt, a)


def _relu2_kernel(a_ref, w_ref, o_ref):
    y = jnp.dot(a_ref[...], w_ref[...], preferred_element_type=F32)
    y = jnp.maximum(y, 0.0)
    o_ref[...] = (y * y).astype(o_ref.dtype)


def _mm_relu2(a, w, *, rows, tm, tn, name):
    k = a.shape[1]
    n = w.shape[1]
    return pl.pallas_call(
        _relu2_kernel,
        grid=(rows // tm, n // tn),
        in_specs=[pl.BlockSpec((tm, k), lambda i, j: (i, 0)),
                  pl.BlockSpec((k, tn), lambda i, j: (0, j))],
        out_specs=pl.BlockSpec((tm, tn), lambda i, j: (i, j)),
        out_shape=jax.ShapeDtypeStruct((rows, n), BF16),
        compiler_params=_cparams(("arbitrary", "arbitrary")),
        name=name,
    )(a, w)


def _res_ln_kernel(a_ref, w_ref, r_ref, gb_ref, o32_ref, o16_ref):
    y = jnp.dot(a_ref[...], w_ref[...], preferred_element_type=F32)
    z = DEEPNORM_ALPHA * r_ref[...] + y
    mu = jnp.mean(z, axis=-1, keepdims=True)
    zc = z - mu
    var = jnp.mean(zc * zc, axis=-1, keepdims=True)
    out = zc * lax.rsqrt(var + LN_EPS) * gb_ref[0:1, :] + gb_ref[1:2, :]
    o32_ref[...] = out
    o16_ref[...] = out.astype(BF16)


def _mm_res_ln(a, w, res, gb, *, rows, tm, name):
    k = a.shape[1]
    n = w.shape[1]
    return pl.pallas_call(
        _res_ln_kernel,
        grid=(rows // tm,),
        in_specs=[pl.BlockSpec((tm, k), lambda i: (i, 0)),
                  pl.BlockSpec((k, n), lambda i: (0, 0), pipeline_mode=pl.Buffered(1)),
                  pl.BlockSpec((tm, n), lambda i: (i, 0)),
                  pl.BlockSpec((8, n), lambda i: (0, 0))],
        out_specs=[pl.BlockSpec((tm, n), lambda i: (i, 0)),
                   pl.BlockSpec((tm, n), lambda i: (i, 0))],
        out_shape=[jax.ShapeDtypeStruct((rows, n), F32),
                   jax.ShapeDtypeStruct((rows, n), BF16)],
        compiler_params=_cparams(("arbitrary",)),
        name=name,
    )(a, w, res, gb)


N_HM = 2 * DIFF_HEADS
N_HT = N_HM + FOX_HEADS
N_VT = DIFF_HEADS + FOX_HEADS
V_W = 2 * LANES
BIAS_LANE0 = DIFF_HEAD_DIM
DIFF_BETA_PIECES = 2
FOX_BETA_PIECES = 3
ATT_OUT_W = DIFF_HEADS * DIFF_V_DIM + FOX_HEADS * LANES
SLOPES = [2.0 ** (-8.0 * (h + 1) / DIFF_HEADS) for h in range(DIFF_HEADS)]


def _pad_heads(w, heads, width, pad_to):
    k = w.shape[0]
    w = w.reshape(k, heads, width)
    w = jnp.pad(w, ((0, 0), (0, 0), (0, pad_to - width)))
    return w.reshape(k, heads * pad_to)


def _coef_rows(n, rows):
    c = np.zeros((8, n), np.float32)
    c[4, :] = 1.0
    for r, vals in rows.items():
        for col, val in vals:
            c[r, col] = val
    return jnp.asarray(c)


EVEN_QK_W = N_HT * DIFF_HEAD_DIM
EVEN_VA_W = DIFF_HEADS * DIFF_V_DIM
EVEN_VB_W = FOX_HEADS * FOX_HEAD_DIM
EVEN_N = 2 * EVEN_QK_W + EVEN_VA_W + EVEN_VB_W + LANES
Q_SCALE = DIFF_HEAD_DIM ** -0.5 * LOG2E
ROW_QCONST, ROW_KHI, ROW_KLO, ROW_KPAD, ROW_VCONST = 0, N_HT, 2 * N_HT, 3 * N_HT, 4 * N_HT


def _even_weights(w_in):
    qk_w = DIFF_HEADS * 2 * DIFF_HEAD_DIM
    v_w = DIFF_HEADS * DIFF_V_DIM
    fox_w = FOX_HEADS * FOX_HEAD_DIM
    o = np.cumsum([0, qk_w, qk_w, v_w, fox_w, fox_w, fox_w, FOX_HEADS])
    sl = [w_in[:, o[i]:o[i + 1]] for i in range(7)]
    w_fb = jnp.pad(sl[6], ((0, 0), (0, LANES - FOX_HEADS)))
    return jnp.concatenate([sl[0], sl[3], sl[1], sl[4], sl[2], sl[5], w_fb], axis=1).astype(BF16)


def _even_consts():
    c = np.zeros((4 * N_HT + 8, LANES), np.float32)
    for ht in range(N_HT):
        pieces = DIFF_BETA_PIECES if ht < N_HM else FOX_BETA_PIECES
        for p in range(pieces):
            for r in range(3):
                c[ROW_QCONST + ht, BIAS_LANE0 + 3 * p + r] = LOG2E_PIECES[r]
        if ht < N_HM:
            slope = SLOPES[ht // 2]
            c[ROW_KHI + ht, BIAS_LANE0:BIAS_LANE0 + 3] = slope * 128.0
            c[ROW_KLO + ht, BIAS_LANE0 + 3:BIAS_LANE0 + 6] = slope
            c[ROW_KPAD + ht, BIAS_LANE0] = NEG
    c[ROW_VCONST, 0] = 1.0
    c[ROW_VCONST + 1, FOX_HEAD_DIM] = 1.0
    return jnp.asarray(c)


def _even_proj_kernel(a_ref, w_ref, c_ref, q_ref, k_ref, v_ref, fb_ref, *, tm):
    y = jnp.dot(a_ref[...], w_ref[...], preferred_element_type=F32)
    lane = lax.broadcasted_iota(jnp.int32, (1, LANES), 1)
    real = lane < DIFF_HEAD_DIM
    pos, valid = _row_positions(pl.program_id(0), tm)
    hi = (pos >> 7).astype(F32)
    lo = (pos & 127).astype(F32)
    pad = jnp.where(valid, 0.0, 1.0)

    def spread(pair, odd):
        return pltpu.roll(pair, DIFF_HEAD_DIM, 1) if odd else pair

    for j in range(N_HT // 2):
        qp = y[:, j * LANES:(j + 1) * LANES] * Q_SCALE
        kp = y[:, EVEN_QK_W + j * LANES:EVEN_QK_W + (j + 1) * LANES]
        for odd in range(2):
            ht = 2 * j + odd
            row = lambda r: c_ref[r + ht:r + ht + 1, :]
            q_ref[ht] = jnp.where(real, spread(qp, odd), row(ROW_QCONST)).astype(BF16)
            k_bias = hi * row(ROW_KHI) + lo * row(ROW_KLO) + pad * row(ROW_KPAD) if ht < N_HM else 0.0
            k_ref[ht] = jnp.where(real, spread(kp, odd), k_bias).astype(BF16)
    base = 2 * EVEN_QK_W
    ones_half = jnp.broadcast_to(c_ref[ROW_VCONST:ROW_VCONST + 1, :], (tm, LANES)).astype(BF16)
    zero_half = jnp.zeros((tm, LANES), BF16)
    for h in range(DIFF_HEADS):
        v_ref[h, :, :LANES] = y[:, base + h * LANES:base + (h + 1) * LANES].astype(BF16)
        v_ref[h, :, LANES:] = ones_half
    base += EVEN_VA_W
    for j in range(FOX_HEADS // 2):
        vp = y[:, base + j * LANES:base + (j + 1) * LANES]
        for odd in range(2):
            h = DIFF_HEADS + 2 * j + odd
            v_ref[h, :, :LANES] = jnp.where(real, spread(vp, odd),
                                            c_ref[ROW_VCONST + 1:ROW_VCONST + 2, :]).astype(BF16)
            v_ref[h, :, LANES:] = zero_half
    base += EVEN_VB_W
    fb_ref[...] = y[:, base:base + LANES]


def _even_proj(a, w, consts, *, tm):
    m, k = a.shape
    return pl.pallas_call(
        functools.partial(_even_proj_kernel, tm=tm),
        grid=(m // tm,),
        in_specs=[pl.BlockSpec((tm, k), lambda i: (i, 0)),
                  pl.BlockSpec((k, EVEN_N), lambda i: (0, 0), pipeline_mode=pl.Buffered(1)),
                  pl.BlockSpec(consts.shape, lambda i: (0, 0))],
        out_specs=[pl.BlockSpec((N_HT, tm, LANES), lambda i: (0, i, 0)),
                   pl.BlockSpec((N_HT, tm, LANES), lambda i: (0, i, 0)),
                   pl.BlockSpec((N_VT, tm, V_W), lambda i: (0, i, 0)),
                   pl.BlockSpec((tm, LANES), lambda i: (i, 0))],
        out_shape=[jax.ShapeDtypeStruct((N_HT, m, LANES), BF16),
                   jax.ShapeDtypeStruct((N_HT, m, LANES), BF16),
                   jax.ShapeDtypeStruct((N_VT, m, V_W), BF16),
                   jax.ShapeDtypeStruct((m, LANES), F32)],
        compiler_params=_cparams(("arbitrary",)),
        name="proj_even",
    )(a, w, consts)


PREP_T = 256


def _split3(x):
    hi = x.astype(BF16)
    r1 = x - hi.astype(F32)
    mid = r1.astype(BF16)
    lo = (r1 - mid.astype(F32)).astype(BF16)
    return hi, mid, lo


def _fox_prep_kernel(fb_ref, bias_ref, tri_ref, sel_ref, kb_ref, o_ref, carry_ref):
    s = pl.program_id(0)
    steps_per_batch = SEQ // PREP_T
    is_meta = s == NF // PREP_T

    @pl.when(jnp.logical_or(s % steps_per_batch == 0, is_meta))
    def _():
        carry_ref[...] = jnp.zeros_like(carry_ref)

    x = fb_ref[...] + bias_ref[0:1, :]
    logf = jnp.minimum(x, 0.0) - jnp.log1p(jnp.exp(-jnp.abs(x)))
    tri = tri_ref[...]
    cum = carry_ref[0:1, :]
    for piece in _split3(logf):
        cum = cum + jnp.dot(tri, piece, preferred_element_type=F32)
    carry_ref[0:1, :] = cum[PREP_T - 1:PREP_T, :]

    row = lax.broadcasted_iota(jnp.int32, (PREP_T, 1), 0)
    meta_g = jnp.where(row < N_META, cum[N_META - 1:N_META, :] - cum, NEG)
    g = jnp.where(is_meta, meta_g, -cum)

    aug = jnp.zeros((PREP_T, FOX_HEADS * LANES), F32)
    for p, piece in enumerate(_split3(g)):
        aug = aug + jnp.dot(piece, sel_ref[p], preferred_element_type=F32)
    for h in range(FOX_HEADS):
        o_ref[h] = (kb_ref[h].astype(F32) + aug[:, h * LANES:(h + 1) * LANES]).astype(BF16)


def _fox_prep(fb, bias_row, k_all):
    tri = jnp.tril(jnp.ones((PREP_T, PREP_T), F32)).astype(BF16)
    sel = np.zeros((FOX_BETA_PIECES, LANES, FOX_HEADS * LANES), np.float32)
    for p in range(FOX_BETA_PIECES):
        for h in range(FOX_HEADS):
            for r in range(3):
                sel[p, h, h * LANES + BIAS_LANE0 + 3 * p + r] = 1.0
    sel = jnp.asarray(sel, BF16)
    fox_blk = pl.BlockSpec((FOX_HEADS, PREP_T, LANES), lambda s: (1, s, 0))
    return pl.pallas_call(
        _fox_prep_kernel,
        grid=(NT // PREP_T,),
        in_specs=[pl.BlockSpec((PREP_T, LANES), lambda s: (s, 0)),
                  pl.BlockSpec((8, LANES), lambda s: (0, 0)),
                  pl.BlockSpec((PREP_T, PREP_T), lambda s: (0, 0)),
                  pl.BlockSpec((FOX_BETA_PIECES, LANES, FOX_HEADS * LANES), lambda s: (0, 0, 0)),
                  fox_blk],
        out_specs=fox_blk,
        out_shape=jax.ShapeDtypeStruct(k_all.shape, BF16),
        scratch_shapes=[pltpu.VMEM((8, LANES), F32)],
        input_output_aliases={4: 0},
        compiler_params=_cparams(("arbitrary",)),
        name="fox_prep",
    )(fb, bias_row, tri, sel, k_all)


def _v_index(ht):
    return jnp.where(ht < N_HM, ht >> 1, ht - DIFF_HEADS)


def _fill_masks(ahead_ref, mask_ref, *, tq, tk, pos_off):
    i = lax.broadcasted_iota(jnp.int32, (tq, tk), 0)
    j = lax.broadcasted_iota(jnp.int32, (tq, tk), 1)
    ahead_ref[...] = jnp.maximum(j - i, 0).astype(F32)
    mask_ref[0] = jnp.where(((j + pos_off) >> 6) > ((i + pos_off) >> 6), NEG, 0.0)
    mask_ref[1] = jnp.where(j > i, NEG, 0.0)


def _attn_init(m_ref, acc_ref):
    m_ref[...] = jnp.full(m_ref.shape, NEG, F32)
    acc_ref[...] = jnp.zeros(acc_ref.shape, F32)


PIPE_U = 2
N_GROUPS = N_HT // PIPE_U


def _attn_pass(q_ref, k_ref, v_ref, m_ref, acc_ref, bufs, coef_ref, ahead_ref, mask_ref, *, tk, masked):
    s_bufs, p_bufs, a_bufs = bufs
    sl = (slice(None), slice(0, tk))

    def stage_qk(ht, s_buf):
        s = lax.dot_general(q_ref[ht], k_ref[ht], (((1,), (1,)), ((), ())), preferred_element_type=F32)
        if masked:
            s = s + (ahead_ref[...] * coef_ref[ht] + mask_ref[jnp.where(ht < N_HM, 0, 1)])
        s_buf[sl] = s

    def stage_softmax(ht, s_buf, p_buf, a_buf):
        s = s_buf[sl]
        m_prev = m_ref[ht]
        m_new = jnp.maximum(m_prev, jnp.max(s, axis=1, keepdims=True))
        a_buf[...] = jnp.exp2(m_prev - m_new)
        p_buf[sl] = jnp.exp2(s - m_new).astype(BF16)
        m_ref[ht] = m_new

    def stage_pv(ht, p_buf, a_buf):
        pv = jnp.dot(p_buf[sl], v_ref[_v_index(ht)], preferred_element_type=F32)
        acc_ref[ht] = a_buf[...] * acc_ref[ht] + pv

    def qk_group(g, par):
        for u in range(PIPE_U):
            stage_qk(PIPE_U * g + u, s_bufs[par][u])

    def softmax_group(g, par):
        for u in range(PIPE_U):
            stage_softmax(PIPE_U * g + u, s_bufs[par][u], p_bufs[par][u], a_bufs[par][u])

    def pv_group(g, par):
        for u in range(PIPE_U):
            stage_pv(PIPE_U * g + u, p_bufs[par][u], a_bufs[par][u])

    def full_body(g, par):
        qk_group(g + 2, par)
        pv_group(g, par)
        softmax_group(g + 1, 1 - par)

    qk_group(0, 0)
    qk_group(1, 1)
    softmax_group(0, 0)

    def body(j, c):
        full_body(2 * j, 0)
        full_body(2 * j + 1, 1)
        return c

    lax.fori_loop(0, (N_GROUPS - 2) // 2, body, 0)
    pv_group(N_GROUPS - 2, 0)
    softmax_group(N_GROUPS - 1, 1)
    pv_group(N_GROUPS - 1, 1)


def _attn_finalize(lam_ref, g_ref, acc_ref, out_ref):
    lv = lam_ref[...]
    lam = (jnp.exp(jnp.sum(lv[0:1] * lv[1:2], axis=1, keepdims=True))
           - jnp.exp(jnp.sum(lv[2:3] * lv[3:4], axis=1, keepdims=True)) + LAM_INIT0)
    gain = g_ref[0:1, :] * (1.0 - LAM_INIT0)
    for h in range(DIFF_HEADS):
        a0 = acc_ref[2 * h]
        a1 = acc_ref[2 * h + 1]
        o = (a0[:, :DIFF_V_DIM] / a0[:, DIFF_V_DIM:DIFF_V_DIM + 1]
             - lam * (a1[:, :DIFF_V_DIM] / a1[:, DIFF_V_DIM:DIFF_V_DIM + 1]))
        ms = jnp.mean(o * o, axis=1, keepdims=True)
        out_ref[:, h * DIFF_V_DIM:(h + 1) * DIFF_V_DIM] = (o * lax.rsqrt(ms + RMS_EPS) * gain).astype(BF16)
    base = DIFF_HEADS * DIFF_V_DIM
    for h in range(FOX_HEADS):
        a = acc_ref[N_HM + h][:, :LANES]
        out_ref[:, base + h * LANES:base + (h + 1) * LANES] = (
            a / a[:, FOX_HEAD_DIM:FOX_HEAD_DIM + 1]).astype(BF16)


STEPS_PER_BATCH = (SEQ // ATT_T) * (SEQ // ATT_T + 1) // 2
ATT_STEPS = BATCH * STEPS_PER_BATCH
DIAG_TK = ATT_T + META_KEYS


def _attn_kernel(qi_ref, kv_ref, coef_ref,
                 q_ref, qn_ref, k_ref, kn_ref, v_ref, km_ref, vm_ref, lam_ref, g_ref, prev_ref,
                 out_ref, m_ref, acc_ref, ahead_ref, mask_ref, *pipe_refs):
    del prev_ref
    step = pl.program_id(0)
    nxt = jnp.minimum(step + 1, ATT_STEPS - 1)
    r, rn = step % STEPS_PER_BATCH, nxt % STEPS_PER_BATCH
    cur_diag = kv_ref[r] == qi_ref[r]
    nxt_diag = kv_ref[rn] == qi_ref[rn]
    cur_first = kv_ref[r] == 0
    nxt_first = kv_ref[rn] == 0
    s_bufs, p_bufs, a_bufs = _pipe_bufs(pipe_refs)
    nt = (((1,), (1,)), ((), ()))
    T, MK = ATT_T, META_KEYS

    def stage_qk(ht, s_buf, q_r, k_r, diag):
        q = q_r[ht]
        s = lax.dot_general(q, k_r[ht], nt, preferred_element_type=F32)
        if diag:
            s = s + (ahead_ref[...] * coef_ref[ht] + mask_ref[jnp.where(ht < N_HM, 0, 1)])
            s_buf[:, T:T + MK] = lax.dot_general(q, km_ref[ht], nt, preferred_element_type=F32)
        s_buf[:, :T] = s

    def stage_softmax(ht, s_buf, p_buf, a_buf, diag, fresh=None):
        w = DIAG_TK if diag else T
        s = s_buf[:, :w]
        m_prev = m_ref[ht]
        if fresh is not None:
            m_prev = jnp.where(fresh, NEG, m_prev)
        m_new = jnp.maximum(m_prev, jnp.max(s, axis=1, keepdims=True))
        a_buf[...] = jnp.exp2(m_prev - m_new)
        p_buf[:, :w] = jnp.exp2(s - m_new).astype(BF16)
        m_ref[ht] = m_new

    def stage_pv(ht, p_buf, a_buf, diag):
        vi = _v_index(ht)
        pv = jnp.dot(p_buf[:, :T], v_ref[vi], preferred_element_type=F32)
        if diag:
            pv = pv + jnp.dot(p_buf[:, T:T + MK], vm_ref[vi], preferred_element_type=F32)
        acc_ref[ht] = a_buf[...] * acc_ref[ht] + pv

    def body(g, par, cd, nd):
        for u in range(PIPE_U):
            if isinstance(g, int) and g + 2 >= N_GROUPS:
                stage_qk(PIPE_U * (g + 2 - N_GROUPS) + u, s_bufs[par][u], qn_ref, kn_ref, nd)
            else:
                stage_qk(PIPE_U * (g + 2) + u, s_bufs[par][u], q_ref, k_ref, cd)
        for u in range(PIPE_U):
            stage_pv(PIPE_U * g + u, p_bufs[par][u], a_bufs[par][u], cd)
        for u in range(PIPE_U):
            bufs = (s_bufs[1 - par][u], p_bufs[1 - par][u], a_bufs[1 - par][u])
            if isinstance(g, int) and g + 1 >= N_GROUPS:
                stage_softmax(PIPE_U * (g + 1 - N_GROUPS) + u, *bufs, nd, fresh=nxt_first)
            else:
                stage_softmax(PIPE_U * (g + 1) + u, *bufs, cd)

    @pl.when(cur_first)
    def _():
        acc_ref[...] = jnp.zeros(acc_ref.shape, F32)
        m_ref[PIPE_U:] = jnp.full((N_HT - PIPE_U,) + m_ref.shape[1:], NEG, F32)

    @pl.when(step == 0)
    def _():
        _fill_masks(ahead_ref, mask_ref, tq=T, tk=T, pos_off=0)
        m_ref[:PIPE_U] = jnp.full((PIPE_U,) + m_ref.shape[1:], NEG, F32)
        for g in range(2):
            for u in range(PIPE_U):
                stage_qk(PIPE_U * g + u, s_bufs[g][u], q_ref, k_ref, True)
        for u in range(PIPE_U):
            stage_softmax(u, s_bufs[0][u], p_bufs[0][u], a_bufs[0][u], True)

    for cd in (False, True):
        for nd in (False, True):
            @pl.when(jnp.logical_and(cur_diag if cd else jnp.logical_not(cur_diag),
                                     nxt_diag if nd else jnp.logical_not(nxt_diag)))
            def _(cd=cd, nd=nd):
                def loop_body(j, c):
                    body(2 * j, 0, cd, nd)
                    body(2 * j + 1, 1, cd, nd)
                    return c

                lax.fori_loop(0, (N_GROUPS - 2) // 2, loop_body, 0)
                body(N_GROUPS - 2, 0, cd, nd)
                body(N_GROUPS - 1, 1, cd, nd)
                if cd:
                    _attn_finalize(lam_ref, g_ref, acc_ref, out_ref)


def _meta_attn_kernel(coef_ref, q_ref, k_ref, v_ref, lam_ref, g_ref, out_ref,
                      m_ref, acc_ref, ahead_ref, mask_ref, *pipe_refs):
    _fill_masks(ahead_ref, mask_ref, tq=META_BLK, tk=META_BLK, pos_off=CHUNK - N_META)
    _attn_init(m_ref, acc_ref)
    _attn_pass(q_ref, k_ref, v_ref, m_ref, acc_ref, _pipe_bufs(pipe_refs), coef_ref, ahead_ref, mask_ref,
               tk=META_BLK, masked=True)
    _attn_finalize(lam_ref, g_ref, acc_ref, out_ref)


N_PIPE_BUFS = 2 * PIPE_U


def _pipe_bufs(refs):
    n = N_PIPE_BUFS
    nest = lambda flat: [list(flat[:PIPE_U]), list(flat[PIPE_U:])]
    return nest(refs[:n]), nest(refs[n:2 * n]), nest(refs[2 * n:3 * n])


def _attn_scratch(tq, tk, buf_w):
    return ([pltpu.VMEM((N_HT, tq, 1), F32), pltpu.VMEM((N_HT, tq, V_W), F32),
             pltpu.VMEM((tq, tk), F32), pltpu.VMEM((2, tq, tk), F32)]
            + [pltpu.VMEM((tq, buf_w), F32)] * N_PIPE_BUFS
            + [pltpu.VMEM((tq, buf_w), BF16)] * N_PIPE_BUFS
            + [pltpu.VMEM((tq, 1), F32)] * N_PIPE_BUFS)


def _attention(q_all, k_all, v_all, lam, gain_row):
    coef = jnp.asarray([-2.0 * SLOPES[ht // 2] * LOG2E if ht < N_HM else 0.0 for ht in range(N_HT)], F32)
    smem = pl.BlockSpec(memory_space=pltpu.SMEM)
    meta_blk = NF // META_BLK

    def meta_spec(heads, w):
        return pl.BlockSpec((heads, META_BLK, w), lambda *_: (0, meta_blk, 0))

    const2 = lambda shape: pl.BlockSpec(shape, lambda *_: (0, 0))

    out0 = pl.pallas_call(
        _meta_attn_kernel,
        grid=(1,),
        in_specs=[smem, meta_spec(N_HT, LANES), meta_spec(N_HT, LANES), meta_spec(N_VT, V_W),
                  const2((4, DIFF_HEAD_DIM)), const2((8, LANES))],
        out_specs=pl.BlockSpec((META_BLK, ATT_OUT_W), lambda s: (meta_blk, 0)),
        out_shape=jax.ShapeDtypeStruct((NT, ATT_OUT_W), BF16),
        scratch_shapes=_attn_scratch(META_BLK, META_BLK, META_BLK),
        compiler_params=_cparams(("arbitrary",)),
        name="attn_meta",
    )(coef, q_all, k_all, v_all, lam, gain_row)

    n_q = SEQ // ATT_T
    qi_tab = np.concatenate([np.full(q + 1, q, np.int32) for q in range(n_q)])
    kv_tab = np.concatenate([np.arange(q + 1, dtype=np.int32) for q in range(n_q)])

    assert len(qi_tab) == STEPS_PER_BATCH

    def row_block(tab_of, ahead):
        def index(s, qi, kv):
            s = jnp.minimum(s + ahead, ATT_STEPS - 1)
            return (s // STEPS_PER_BATCH) * n_q + tab_of(qi, kv)[s % STEPS_PER_BATCH]
        return index

    q_blk, q_nxt = row_block(lambda qi, kv: qi, 0), row_block(lambda qi, kv: qi, 1)
    k_blk, k_nxt = row_block(lambda qi, kv: kv, 0), row_block(lambda qi, kv: kv, 1)

    def tile_spec(heads, w, blk):
        return pl.BlockSpec((heads, ATT_T, w), lambda s, qi, kv: (0, blk(s, qi, kv), 0))

    def meta_keys_spec(heads, w):
        return pl.BlockSpec((heads, META_KEYS, w), lambda *_: (0, NF // META_KEYS, 0))

    grid_spec = pltpu.PrefetchScalarGridSpec(
        num_scalar_prefetch=2,
        grid=(ATT_STEPS,),
        in_specs=[smem, tile_spec(N_HT, LANES, q_blk), tile_spec(N_HT, LANES, q_nxt),
                  tile_spec(N_HT, LANES, k_blk), tile_spec(N_HT, LANES, k_nxt), tile_spec(N_VT, V_W, k_blk),
                  meta_keys_spec(N_HT, LANES), meta_keys_spec(N_VT, V_W),
                  const2((4, DIFF_HEAD_DIM)), const2((8, LANES)),
                  pl.BlockSpec(memory_space=pl.ANY)],
        out_specs=pl.BlockSpec((ATT_T, ATT_OUT_W), lambda s, qi, kv: (q_blk(s, qi, kv), 0)),
        scratch_shapes=_attn_scratch(ATT_T, ATT_T, DIAG_TK),
    )
    return pl.pallas_call(
        _attn_kernel,
        grid_spec=grid_spec,
        out_shape=jax.ShapeDtypeStruct((NT, ATT_OUT_W), BF16),
        input_output_aliases={12: 0},
        compiler_params=_cparams(("arbitrary",)),
        name="attn_frames",
    )(jnp.asarray(qi_tab), jnp.asarray(kv_tab), coef,
      q_all, q_all, k_all, k_all, v_all, k_all, v_all, lam, gain_row, out0)


def _log_gamma(h):
    return jnp.log1p(jnp.full((1, 1), -(2.0 ** (-5.0 - h)), F32))


def _retention_kernel(q_ref, kt_ref, v_ref, g_ref, ktm_ref, vm_ref, o_ref, state_ref, decay_ref):
    b = pl.program_id(0)
    t = pl.program_id(1)
    T = RET_T

    @pl.when(jnp.logical_and(b == 0, t == 0))
    def _():
        i = lax.broadcasted_iota(jnp.int32, (T, T), 0)
        j = lax.broadcasted_iota(jnp.int32, (T, T), 1)
        dist = jnp.abs(i - j).astype(F32)
        vis = (j >> 6) <= (i >> 6)
        for h in range(RET_HEADS):
            decay_ref[h] = jnp.where(vis, jnp.exp(_log_gamma(h) * dist), 0.0)

    @pl.when(t == 0)
    def _():
        m = lax.broadcasted_iota(jnp.int32, (1, META_BLK), 1)
        for h in range(RET_HEADS):
            w = jnp.exp(_log_gamma(h) * (N_META - 1 - m).astype(F32))
            kt = ktm_ref[h * RET_QK_DIM:(h + 1) * RET_QK_DIM, :].astype(F32)
            kd = jnp.where(m < N_META, kt * w, 0.0).astype(BF16)
            state_ref[h] = jnp.dot(kd, vm_ref[:, h * RET_V_DIM:(h + 1) * RET_V_DIM],
                                   preferred_element_type=F32)

    row = lax.broadcasted_iota(jnp.int32, (T, 1), 0).astype(F32)
    col = lax.broadcasted_iota(jnp.int32, (1, T), 1).astype(F32)
    for h in range(RET_HEADS):
        lg = _log_gamma(h)
        q = q_ref[:, h * RET_QK_DIM:(h + 1) * RET_QK_DIM]
        kt = kt_ref[h * RET_QK_DIM:(h + 1) * RET_QK_DIM, :]
        v = v_ref[:, h * RET_V_DIM:(h + 1) * RET_V_DIM]
        scores = jnp.dot(q, kt, preferred_element_type=F32) * decay_ref[h]
        o = jnp.dot(scores.astype(BF16), v, preferred_element_type=F32)
        qd = (q.astype(F32) * jnp.exp(lg * (row + 1.0))).astype(BF16)
        state = state_ref[h]
        o = o + jnp.dot(qd, state.astype(BF16), preferred_element_type=F32)
        kd = (kt.astype(F32) * jnp.exp(lg * (T - 1.0 - col))).astype(BF16)
        state_ref[h] = jnp.exp(lg * float(T)) * state + jnp.dot(kd, v, preferred_element_type=F32)
        o = o * lax.rsqrt(jnp.mean(o * o, axis=1, keepdims=True) + RMS_EPS)
        gate = g_ref[:, h * RET_V_DIM:(h + 1) * RET_V_DIM].astype(F32)
        o_ref[:, h * RET_V_DIM:(h + 1) * RET_V_DIM] = (gate * jax.nn.sigmoid(gate) * o).astype(BF16)


def _retention(q, kt, v, g):
    n_t = SEQ // RET_T
    qk_w = RET_HEADS * RET_QK_DIM
    v_w = RET_HEADS * RET_V_DIM
    meta_blk = NF // META_BLK
    return pl.pallas_call(
        _retention_kernel,
        grid=(BATCH, n_t),
        in_specs=[pl.BlockSpec((RET_T, qk_w), lambda b, t: (b * n_t + t, 0)),
                  pl.BlockSpec((qk_w, RET_T), lambda b, t: (0, b * n_t + t)),
                  pl.BlockSpec((RET_T, v_w), lambda b, t: (b * n_t + t, 0)),
                  pl.BlockSpec((RET_T, v_w), lambda b, t: (b * n_t + t, 0)),
                  pl.BlockSpec((qk_w, META_BLK), lambda b, t: (0, meta_blk)),
                  pl.BlockSpec((META_BLK, v_w), lambda b, t: (meta_blk, 0))],
        out_specs=pl.BlockSpec((RET_T, v_w), lambda b, t: (b * n_t + t, 0)),
        out_shape=jax.ShapeDtypeStruct((NF, v_w), BF16),
        scratch_shapes=[pltpu.VMEM((RET_HEADS, RET_QK_DIM, RET_V_DIM), F32),
                        pltpu.VMEM((RET_HEADS, RET_T, RET_T), F32)],
        compiler_params=_cparams(("arbitrary", "arbitrary")),
        name="retention",
    )(q, kt, v, g, kt, v)


def _rows8(*rows):
    n = rows[0].shape[-1]
    out = jnp.zeros((8, n), F32)
    for i, r in enumerate(rows):
        out = out.at[i].set(r.astype(F32))
    return out


def _unit_coef(n):
    return _coef_rows(n, {})


TM_ALL = 2064
TM_LN_ALL = 768
TM_FRAMES = 1024


def kernel(x, meta_tokens, even_w_in, even_f_bias, diff_lambda, diff_subln_g, even_w_out,
           ret_w_in, ret_w_out, ln_g, ln_b, ffn_w1, ffn_w2):
    pad = jnp.zeros((META_BLK - N_META, D_MODEL), F32)
    h0 = jnp.concatenate([x.reshape(NF, D_MODEL), meta_tokens.astype(F32), pad], axis=0)
    h0_16 = h0.astype(BF16)

    q_all, k_all, v_all, fb = _even_proj(h0_16, _even_weights(even_w_in[0]), _even_consts(), tm=TM_LN_ALL)
    f_bias_row = _rows8(jnp.pad(even_f_bias[0], (0, LANES - FOX_HEADS)))
    k_all = _fox_prep(fb, f_bias_row, k_all)

    attn = _attention(q_all, k_all, v_all, diff_lambda[0].astype(F32), _rows8(diff_subln_g[0]))

    w_out = even_w_out[0]
    w_out = jnp.concatenate(
        [w_out[:DIFF_HEADS * DIFF_V_DIM],
         _pad_heads(w_out[DIFF_HEADS * DIFF_V_DIM:].T, FOX_HEADS, FOX_HEAD_DIM, LANES).T], axis=0).astype(BF16)
    h1, h1_16 = _mm_res_ln(attn, w_out, h0, _rows8(ln_g[0, 0], ln_b[0, 0]), rows=NT, tm=TM_LN_ALL,
                           name="even_out_ln")
    f1 = _mm_relu2(h1_16, ffn_w1[0].astype(BF16), rows=NT, tm=TM_ALL, tn=1024, name="ffn0_up")
    h2, h2_16 = _mm_res_ln(f1, ffn_w2[0].astype(BF16), h1, _rows8(ln_g[0, 1], ln_b[0, 1]),
                           rows=NT, tm=TM_LN_ALL, name="ffn0_down_ln")

    qk_w = RET_HEADS * RET_QK_DIM
    v_w = RET_HEADS * RET_V_DIM
    rw = ret_w_in[0]
    rq = _proj(h2_16, rw[:, :qk_w].astype(BF16), _unit_coef(qk_w), tm=TM_ALL, name="proj_ret_q")
    rkt = _proj_t((rw[:, qk_w:2 * qk_w] * RET_QK_DIM ** -0.5).T.astype(BF16), h2_16, tm=TM_LN_ALL)
    rv = _proj(h2_16, rw[:, 2 * qk_w:2 * qk_w + v_w].astype(BF16), _unit_coef(v_w), tm=TM_ALL, name="proj_ret_v")
    rg = _proj(h2_16, rw[:, 2 * qk_w + v_w:].astype(BF16), _unit_coef(v_w), tm=TM_ALL, name="proj_ret_g")
    y = _retention(rq, rkt, rv, rg)

    h3, h3_16 = _mm_res_ln(y, ret_w_out[0].astype(BF16), h2, _rows8(ln_g[1, 0], ln_b[1, 0]),
                           rows=NF, tm=TM_FRAMES, name="ret_out_ln")
    f2 = _mm_relu2(h3_16, ffn_w1[1].astype(BF16), rows=NF, tm=2048, tn=1024, name="ffn1_up")
    out, _ = _mm_res_ln(f2, ffn_w2[1].astype(BF16), h3, _rows8(ln_g[1, 1], ln_b[1, 1]),
                        rows=NF, tm=TM_FRAMES, name="ffn1_down_ln")
    return out.reshape(BATCH, SEQ, D_MODEL)
```

```python
import functools
import math

import jax
import jax.numpy as jnp
import numpy as np
from jax import lax
from jax.experimental import pallas as pl
from jax.experimental.pallas import tpu as pltpu

F32 = jnp.float32
BF16 = jnp.bfloat16

D_MODEL = 1024
BATCH = 4
SEQ = 8192
DEPTH = 2
CHUNK = 64
N_META = 16
DIFF_HEADS = 4
DIFF_HEAD_DIM = 64
DIFF_V_DIM = 128
FOX_HEADS = 8
FOX_HEAD_DIM = 64
RET_HEADS = 4
RET_QK_DIM = 256
RET_V_DIM = 512
D_FF = 4 * D_MODEL
DEEPNORM_ALPHA = (2 * DEPTH) ** 0.25
LN_EPS = 1e-5
RMS_EPS = 1e-6
LAM_INIT0 = 0.8 - 0.6 * math.exp(-0.3 * 0)
LOG2E = math.log2(math.e)

NF = BATCH * SEQ
META_BLK = 256
NT = NF + META_BLK
LANES = 128
NEG = -1e30

ATT_T = 512
META_KEYS = 128
RET_T = 512
VMEM_LIMIT = 56 * 1024 * 1024


def _cparams(sem):
    return pltpu.CompilerParams(dimension_semantics=sem, vmem_limit_bytes=VMEM_LIMIT)


def _bf16_pieces(x, n=3):
    out = []
    r = np.float32(x)
    for _ in range(n):
        p = np.float32(np.asarray(r, dtype=BF16).astype(np.float32))
        out.append(float(p))
        r = np.float32(r - p)
    return out


LOG2E_PIECES = _bf16_pieces(LOG2E)


def _row_positions(i, tm):
    r = i * tm + lax.broadcasted_iota(jnp.int32, (tm, 1), 0)
    is_frame = r < NF
    m = r - NF
    pos = jnp.where(is_frame, (r & (SEQ - 1)) + N_META, m)
    valid = jnp.logical_or(is_frame, m < N_META)
    return pos, valid


def _proj_kernel(a_ref, w_ref, c_ref, o_ref, *, tm, heads, head_w, use_pos):
    y = jnp.dot(a_ref[...], w_ref[...], preferred_element_type=F32)
    y = y * c_ref[4:5, :] + c_ref[0:1, :]
    if use_pos:
        pos, valid = _row_positions(pl.program_id(0), tm)
        hi = (pos >> 7).astype(F32)
        lo = (pos & 127).astype(F32)
        y = y + hi * c_ref[1:2, :] + lo * c_ref[2:3, :] + jnp.where(valid, 0.0, 1.0) * c_ref[3:4, :]
    if heads is None:
        o_ref[...] = y.astype(o_ref.dtype)
    else:
        for h in range(heads):
            o_ref[h] = y[:, h * head_w:(h + 1) * head_w].astype(o_ref.dtype)


def _proj(a, w, coef, *, tm, name, heads=None, head_w=None, use_pos=False, out_dtype=BF16):
    m, k = a.shape
    n = w.shape[1]
    if heads is None:
        out_shape = jax.ShapeDtypeStruct((m, n), out_dtype)
        out_spec = pl.BlockSpec((tm, n), lambda i: (i, 0))
    else:
        out_shape = jax.ShapeDtypeStruct((heads, m, head_w), out_dtype)
        out_spec = pl.BlockSpec((heads, tm, head_w), lambda i: (0, i, 0))
    return pl.pallas_call(
        functools.partial(_proj_kernel, tm=tm, heads=heads, head_w=head_w, use_pos=use_pos),
        grid=(m // tm,),
        in_specs=[pl.BlockSpec((tm, k), lambda i: (i, 0)),
                  pl.BlockSpec((k, n), lambda i: (0, 0), pipeline_mode=pl.Buffered(1)),
                  pl.BlockSpec((8, n), lambda i: (0, 0))],
        out_specs=out_spec,
        out_shape=out_shape,
        compiler_params=_cparams(("arbitrary",)),
        name=name,
    )(a, w, coef)


def _proj_t_kernel(wt_ref, a_ref, o_ref):
    o_ref[...] = lax.dot_general(wt_ref[...], a_ref[...], (((1,), (1,)), ((), ())),
                                 preferred_element_type=F32).astype(o_ref.dtype)


def _proj_t(wt, a, *, tm):
    n, k = wt.shape
    m = a.shape[0]
    return pl.pallas_call(
        _proj_t_kernel,
        grid=(m // tm,),
        in_specs=[pl.BlockSpec((n, k), lambda i: (0, 0)),
                  pl.BlockSpec((tm, k), lambda i: (i, 0))],
        out_specs=pl.BlockSpec((n, tm), lambda i: (0, i)),
        out_shape=jax.ShapeDtypeStruct((n, m), BF16),
        compiler_params=_cparams(("arbitrary",)),
        name="proj_ret_kt",
    )(wt, a)


def _relu2_kernel(a_ref, w_ref, o_ref):
    y = jnp.dot(a_ref[...], w_ref[...], preferred_element_type=F32)
    y = jnp.maximum(y, 0.0)
    o_ref[...] = (y * y).astype(o_ref.dtype)


def _mm_relu2(a, w, *, rows, tm, tn, name):
    k = a.shape[1]
    n = w.shape[1]
    return pl.pallas_call(
        _relu2_kernel,
        grid=(rows // tm, n // tn),
        in_specs=[pl.BlockSpec((tm, k), lambda i, j: (i, 0)),
                  pl.BlockSpec((k, tn), lambda i, j: (0, j))],
        out_specs=pl.BlockSpec((tm, tn), lambda i, j: (i, j)),
        out_shape=jax.ShapeDtypeStruct((rows, n), BF16),
        compiler_params=_cparams(("arbitrary", "arbitrary")),
        name=name,
    )(a, w)


def _res_ln_kernel(a_ref, w_ref, r_ref, gb_ref, o32_ref, o16_ref):
    y = jnp.dot(a_ref[...], w_ref[...], preferred_element_type=F32)
    z = DEEPNORM_ALPHA * r_ref[...] + y
    mu = jnp.mean(z, axis=-1, keepdims=True)
    zc = z - mu
    var = jnp.mean(zc * zc, axis=-1, keepdims=True)
    out = zc * lax.rsqrt(var + LN_EPS) * gb_ref[0:1, :] + gb_ref[1:2, :]
    o32_ref[...] = out
    o16_ref[...] = out.astype(BF16)


def _mm_res_ln(a, w, res, gb, *, rows, tm, name):
    k = a.shape[1]
    n = w.shape[1]
    return pl.pallas_call(
        _res_ln_kernel,
        grid=(rows // tm,),
        in_specs=[pl.BlockSpec((tm, k), lambda i: (i, 0)),
                  pl.BlockSpec((k, n), lambda i: (0, 0), pipeline_mode=pl.Buffered(1)),
                  pl.BlockSpec((tm, n), lambda i: (i, 0)),
                  pl.BlockSpec((8, n), lambda i: (0, 0))],
        out_specs=[pl.BlockSpec((tm, n), lambda i: (i, 0)),
                   pl.BlockSpec((tm, n), lambda i: (i, 0))],
        out_shape=[jax.ShapeDtypeStruct((rows, n), F32),
                   jax.ShapeDtypeStruct((rows, n), BF16)],
        compiler_params=_cparams(("arbitrary",)),
        name=name,
    )(a, w, res, gb)


N_HM = 2 * DIFF_HEADS
N_HT = N_HM + FOX_HEADS
VA_W = 2 * LANES
VB_W = LANES
BIAS_LANE0 = DIFF_HEAD_DIM
DIFF_BETA_PIECES = 2
FOX_BETA_PIECES = 3
ATT_OUT_W = DIFF_HEADS * DIFF_V_DIM + FOX_HEADS * LANES
SLOPES = [2.0 ** (-8.0 * (h + 1) / DIFF_HEADS) for h in range(DIFF_HEADS)]


def _pad_heads(w, heads, width, pad_to):
    k = w.shape[0]
    w = w.reshape(k, heads, width)
    w = jnp.pad(w, ((0, 0), (0, 0), (0, pad_to - width)))
    return w.reshape(k, heads * pad_to)


def _coef_rows(n, rows):
    c = np.zeros((8, n), np.float32)
    c[4, :] = 1.0
    for r, vals in rows.items():
        for col, val in vals:
            c[r, col] = val
    return jnp.asarray(c)


EVEN_QK_W = N_HT * DIFF_HEAD_DIM
EVEN_VA_W = DIFF_HEADS * DIFF_V_DIM
EVEN_VB_W = FOX_HEADS * FOX_HEAD_DIM
EVEN_N = 2 * EVEN_QK_W + EVEN_VA_W + EVEN_VB_W + LANES
Q_SCALE = DIFF_HEAD_DIM ** -0.5 * LOG2E
ROW_QCONST, ROW_KHI, ROW_KLO, ROW_KPAD, ROW_VCONST = 0, N_HT, 2 * N_HT, 3 * N_HT, 4 * N_HT


def _even_weights(w_in):
    qk_w = DIFF_HEADS * 2 * DIFF_HEAD_DIM
    v_w = DIFF_HEADS * DIFF_V_DIM
    fox_w = FOX_HEADS * FOX_HEAD_DIM
    o = np.cumsum([0, qk_w, qk_w, v_w, fox_w, fox_w, fox_w, FOX_HEADS])
    sl = [w_in[:, o[i]:o[i + 1]] for i in range(7)]
    w_fb = jnp.pad(sl[6], ((0, 0), (0, LANES - FOX_HEADS)))
    return jnp.concatenate([sl[0], sl[3], sl[1], sl[4], sl[2], sl[5], w_fb], axis=1).astype(BF16)


def _even_consts():
    c = np.zeros((4 * N_HT + 8, LANES), np.float32)
    for ht in range(N_HT):
        pieces = DIFF_BETA_PIECES if ht < N_HM else FOX_BETA_PIECES
        for p in range(pieces):
            for r in range(3):
                c[ROW_QCONST + ht, BIAS_LANE0 + 3 * p + r] = LOG2E_PIECES[r]
        if ht < N_HM:
            slope = SLOPES[ht // 2]
            c[ROW_KHI + ht, BIAS_LANE0:BIAS_LANE0 + 3] = slope * 128.0
            c[ROW_KLO + ht, BIAS_LANE0 + 3:BIAS_LANE0 + 6] = slope
            c[ROW_KPAD + ht, BIAS_LANE0] = NEG
    c[ROW_VCONST, 0] = 1.0
    c[ROW_VCONST + 1, FOX_HEAD_DIM] = 1.0
    return jnp.asarray(c)


def _even_proj_kernel(a_ref, w_ref, c_ref, q_ref, k_ref, va_ref, vb_ref, fb_ref, *, tm):
    y = jnp.dot(a_ref[...].astype(BF16), w_ref[...], preferred_element_type=F32)
    lane = lax.broadcasted_iota(jnp.int32, (1, LANES), 1)
    real = lane < DIFF_HEAD_DIM
    pos, valid = _row_positions(pl.program_id(0), tm)
    hi = (pos >> 7).astype(F32)
    lo = (pos & 127).astype(F32)
    pad = jnp.where(valid, 0.0, 1.0)

    def spread(pair, odd):
        return pltpu.roll(pair, DIFF_HEAD_DIM, 1) if odd else pair

    for j in range(N_HT // 2):
        qp = y[:, j * LANES:(j + 1) * LANES] * Q_SCALE
        kp = y[:, EVEN_QK_W + j * LANES:EVEN_QK_W + (j + 1) * LANES]
        for odd in range(2):
            ht = 2 * j + odd
            row = lambda r: c_ref[r + ht:r + ht + 1, :]
            q_ref[ht] = jnp.where(real, spread(qp, odd), row(ROW_QCONST)).astype(BF16)
            k_bias = hi * row(ROW_KHI) + lo * row(ROW_KLO) + pad * row(ROW_KPAD) if ht < N_HM else 0.0
            k_ref[ht] = jnp.where(real, spread(kp, odd), k_bias).astype(BF16)
    base = 2 * EVEN_QK_W
    ones_half = jnp.broadcast_to(c_ref[ROW_VCONST:ROW_VCONST + 1, :], (tm, LANES)).astype(BF16)
    for h in range(DIFF_HEADS):
        va_ref[h, :, :LANES] = y[:, base + h * LANES:base + (h + 1) * LANES].astype(BF16)
        va_ref[h, :, LANES:] = ones_half
    base += EVEN_VA_W
    for j in range(FOX_HEADS // 2):
        vp = y[:, base + j * LANES:base + (j + 1) * LANES]
        for odd in range(2):
            vb_ref[2 * j + odd] = jnp.where(real, spread(vp, odd),
                                            c_ref[ROW_VCONST + 1:ROW_VCONST + 2, :]).astype(BF16)
    base += EVEN_VB_W
    fb_ref[...] = y[:, base:base + LANES]


def _even_proj(a, w, consts, *, tm):
    m, k = a.shape
    return pl.pallas_call(
        functools.partial(_even_proj_kernel, tm=tm),
        grid=(m // tm,),
        in_specs=[pl.BlockSpec((tm, k), lambda i: (i, 0)),
                  pl.BlockSpec((k, EVEN_N), lambda i: (0, 0), pipeline_mode=pl.Buffered(1)),
                  pl.BlockSpec(consts.shape, lambda i: (0, 0))],
        out_specs=[pl.BlockSpec((N_HT, tm, LANES), lambda i: (0, i, 0)),
                   pl.BlockSpec((N_HT, tm, LANES), lambda i: (0, i, 0)),
                   pl.BlockSpec((DIFF_HEADS, tm, VA_W), lambda i: (0, i, 0)),
                   pl.BlockSpec((FOX_HEADS, tm, VB_W), lambda i: (0, i, 0)),
                   pl.BlockSpec((tm, LANES), lambda i: (i, 0))],
        out_shape=[jax.ShapeDtypeStruct((N_HT, m, LANES), BF16),
                   jax.ShapeDtypeStruct((N_HT, m, LANES), BF16),
                   jax.ShapeDtypeStruct((DIFF_HEADS, m, VA_W), BF16),
                   jax.ShapeDtypeStruct((FOX_HEADS, m, VB_W), BF16),
                   jax.ShapeDtypeStruct((m, LANES), F32)],
        compiler_params=_cparams(("arbitrary",)),
        name="proj_even",
    )(a, w, consts)


PREP_T = 256


def _split3(x):
    hi = x.astype(BF16)
    r1 = x - hi.astype(F32)
    mid = r1.astype(BF16)
    lo = (r1 - mid.astype(F32)).astype(BF16)
    return hi, mid, lo


def _fox_prep_kernel(fb_ref, bias_ref, tri_ref, sel_ref, kb_ref, o_ref, carry_ref):
    s = pl.program_id(0)
    steps_per_batch = SEQ // PREP_T
    is_meta = s == NF // PREP_T

    @pl.when(jnp.logical_or(s % steps_per_batch == 0, is_meta))
    def _():
        carry_ref[...] = jnp.zeros_like(carry_ref)

    x = fb_ref[...] + bias_ref[0:1, :]
    logf = jnp.minimum(x, 0.0) - jnp.log1p(jnp.exp(-jnp.abs(x)))
    tri = tri_ref[...]
    cum = carry_ref[0:1, :]
    for piece in _split3(logf):
        cum = cum + jnp.dot(tri, piece, preferred_element_type=F32)
    carry_ref[0:1, :] = cum[PREP_T - 1:PREP_T, :]

    row = lax.broadcasted_iota(jnp.int32, (PREP_T, 1), 0)
    meta_g = jnp.where(row < N_META, cum[N_META - 1:N_META, :] - cum, NEG)
    g = jnp.where(is_meta, meta_g, -cum)

    aug = jnp.zeros((PREP_T, FOX_HEADS * LANES), F32)
    for p, piece in enumerate(_split3(g)):
        aug = aug + jnp.dot(piece, sel_ref[p], preferred_element_type=F32)
    for h in range(FOX_HEADS):
        o_ref[h] = (kb_ref[h].astype(F32) + aug[:, h * LANES:(h + 1) * LANES]).astype(BF16)


def _fox_prep(fb, bias_row, k_all):
    tri = jnp.tril(jnp.ones((PREP_T, PREP_T), F32)).astype(BF16)
    sel = np.zeros((FOX_BETA_PIECES, LANES, FOX_HEADS * LANES), np.float32)
    for p in range(FOX_BETA_PIECES):
        for h in range(FOX_HEADS):
            for r in range(3):
                sel[p, h, h * LANES + BIAS_LANE0 + 3 * p + r] = 1.0
    sel = jnp.asarray(sel, BF16)
    fox_blk = pl.BlockSpec((FOX_HEADS, PREP_T, LANES), lambda s: (1, s, 0))
    return pl.pallas_call(
        _fox_prep_kernel,
        grid=(NT // PREP_T,),
        in_specs=[pl.BlockSpec((PREP_T, LANES), lambda s: (s, 0)),
                  pl.BlockSpec((8, LANES), lambda s: (0, 0)),
                  pl.BlockSpec((PREP_T, PREP_T), lambda s: (0, 0)),
                  pl.BlockSpec((FOX_BETA_PIECES, LANES, FOX_HEADS * LANES), lambda s: (0, 0, 0)),
                  fox_blk],
        out_specs=fox_blk,
        out_shape=jax.ShapeDtypeStruct(k_all.shape, BF16),
        scratch_shapes=[pltpu.VMEM((8, LANES), F32)],
        input_output_aliases={4: 0},
        compiler_params=_cparams(("arbitrary",)),
        name="fox_prep",
    )(fb, bias_row, tri, sel, k_all)


def _fill_masks(ahead_ref, mask_ref, *, tq, tk, pos_off):
    i = lax.broadcasted_iota(jnp.int32, (tq, tk), 0)
    j = lax.broadcasted_iota(jnp.int32, (tq, tk), 1)
    ahead_ref[...] = jnp.maximum(j - i, 0).astype(F32)
    mask_ref[0] = jnp.where(((j + pos_off) >> 6) > ((i + pos_off) >> 6), NEG, 0.0)
    mask_ref[1] = jnp.where(j > i, NEG, 0.0)


def _softmax_update(s, m_prev):
    m_new = jnp.maximum(m_prev, jnp.max(s, axis=1, keepdims=True))
    alpha = jnp.exp2(m_prev - m_new)
    p = jnp.exp2(s - m_new).astype(BF16)
    return m_new, alpha, p


PIPE_U = 2
N_GROUPS = N_HT // PIPE_U
NT_DIMS = (((1,), (1,)), ((), ()))


def _head_tile(u, g):
    return g + N_HM * u


def _diag_bias(u, g, coef_ref, ahead_ref, mask_ref):
    if u == 0:
        return ahead_ref[...] * coef_ref[g] + mask_ref[0]
    return mask_ref[1]


def _acc_update(u, g, acc_a, acc_b, alpha, pv):
    if u == 0:
        acc_a[g] = alpha * acc_a[g] + pv
    else:
        acc_b[g] = alpha * acc_b[g] + pv


def _attn_finalize(lam_ref, g_ref, acc_a, acc_b, out_ref):
    lv = lam_ref[...]
    lam = (jnp.exp(jnp.sum(lv[0:1] * lv[1:2], axis=1, keepdims=True))
           - jnp.exp(jnp.sum(lv[2:3] * lv[3:4], axis=1, keepdims=True)) + LAM_INIT0)
    gain = g_ref[0:1, :] * (1.0 - LAM_INIT0)
    for h in range(DIFF_HEADS):
        a0 = acc_a[2 * h]
        a1 = acc_a[2 * h + 1]
        o = (a0[:, :DIFF_V_DIM] / a0[:, DIFF_V_DIM:DIFF_V_DIM + 1]
             - lam * (a1[:, :DIFF_V_DIM] / a1[:, DIFF_V_DIM:DIFF_V_DIM + 1]))
        ms = jnp.mean(o * o, axis=1, keepdims=True)
        out_ref[:, h * DIFF_V_DIM:(h + 1) * DIFF_V_DIM] = (o * lax.rsqrt(ms + RMS_EPS) * gain).astype(BF16)
    base = DIFF_HEADS * DIFF_V_DIM
    for h in range(FOX_HEADS):
        a = acc_b[h]
        out_ref[:, base + h * LANES:base + (h + 1) * LANES] = (
            a / a[:, FOX_HEAD_DIM:FOX_HEAD_DIM + 1]).astype(BF16)


STEPS_PER_BATCH = (SEQ // ATT_T) * (SEQ // ATT_T + 1) // 2
ATT_STEPS = BATCH * STEPS_PER_BATCH
DIAG_TK = ATT_T + META_KEYS


def _attn_kernel(qi_ref, kv_ref, coef_ref,
                 q_ref, qna_ref, qnb_ref, k_ref, kna_ref, knb_ref, va_ref, vb_ref, km_ref, vma_ref, vmb_ref,
                 lam_ref, g_ref, prev_ref,
                 out_ref, m_ref, acc_a, acc_b, ahead_ref, mask_ref, *pipe_refs):
    del prev_ref
    step = pl.program_id(0)
    nxt = jnp.minimum(step + 1, ATT_STEPS - 1)
    r, rn = step % STEPS_PER_BATCH, nxt % STEPS_PER_BATCH
    cur_diag = kv_ref[r] == qi_ref[r]
    nxt_diag = kv_ref[rn] == qi_ref[rn]
    cur_first = kv_ref[r] == 0
    nxt_first = kv_ref[rn] == 0
    s_bufs, p_bufs, a_bufs = _pipe_bufs(pipe_refs)
    T, MK = ATT_T, META_KEYS
    v_refs, vm_refs = (va_ref, vb_ref), (vma_ref, vmb_ref)
    cur_rows = lambda ref: (lambda u, g: ref[_head_tile(u, g)])
    nxt_rows = lambda refs: (lambda u, g: refs[u][g])

    def stage_qk(u, g, s_buf, q_of, k_of, diag):
        q = q_of(u, g)
        s = lax.dot_general(q, k_of(u, g), NT_DIMS, preferred_element_type=F32)
        if diag:
            s = s + _diag_bias(u, g, coef_ref, ahead_ref, mask_ref)
            s_buf[:, T:T + MK] = lax.dot_general(q, km_ref[_head_tile(u, g)], NT_DIMS,
                                                 preferred_element_type=F32)
        s_buf[:, :T] = s

    def stage_softmax(u, g, s_buf, p_buf, a_buf, diag, fresh=None):
        w = DIAG_TK if diag else T
        ht = _head_tile(u, g)
        m_prev = m_ref[ht]
        if fresh is not None:
            m_prev = jnp.where(fresh, NEG, m_prev)
        m_ref[ht], a_buf[...], p_buf[:, :w] = _softmax_update(s_buf[:, :w], m_prev)

    def stage_pv(u, g, p_buf, a_buf, diag):
        vi = g >> 1 if u == 0 else g
        pv = jnp.dot(p_buf[:, :T], v_refs[u][vi], preferred_element_type=F32)
        if diag:
            pv = pv + jnp.dot(p_buf[:, T:T + MK], vm_refs[u][vi], preferred_element_type=F32)
        _acc_update(u, g, acc_a, acc_b, a_buf[...], pv)

    def body(g, par, cd, nd):
        for u in range(PIPE_U):
            if isinstance(g, int) and g + 2 >= N_GROUPS:
                stage_qk(u, g + 2 - N_GROUPS, s_bufs[par][u],
                         nxt_rows((qna_ref, qnb_ref)), nxt_rows((kna_ref, knb_ref)), nd)
            else:
                stage_qk(u, g + 2, s_bufs[par][u], cur_rows(q_ref), cur_rows(k_ref), cd)
        for u in range(PIPE_U):
            stage_pv(u, g, p_bufs[par][u], a_bufs[par][u], cd)
        for u in range(PIPE_U):
            bufs = (s_bufs[1 - par][u], p_bufs[1 - par][u], a_bufs[1 - par][u])
            if isinstance(g, int) and g + 1 >= N_GROUPS:
                stage_softmax(u, g + 1 - N_GROUPS, *bufs, nd, fresh=nxt_first)
            else:
                stage_softmax(u, g + 1, *bufs, cd)

    @pl.when(cur_first)
    def _():
        acc_a[...] = jnp.zeros(acc_a.shape, F32)
        acc_b[...] = jnp.zeros(acc_b.shape, F32)
        fresh_m = jnp.full((N_HM - 1,) + m_ref.shape[1:], NEG, F32)
        m_ref[1:N_HM] = fresh_m
        m_ref[N_HM + 1:] = fresh_m

    @pl.when(step == 0)
    def _():
        _fill_masks(ahead_ref, mask_ref, tq=T, tk=T, pos_off=0)
        for u in range(PIPE_U):
            m_ref[_head_tile(u, 0)] = jnp.full(m_ref.shape[1:], NEG, F32)
        for g in range(2):
            for u in range(PIPE_U):
                stage_qk(u, g, s_bufs[g][u], cur_rows(q_ref), cur_rows(k_ref), True)
        for u in range(PIPE_U):
            stage_softmax(u, 0, s_bufs[0][u], p_bufs[0][u], a_bufs[0][u], True)

    for cd in (False, True):
        for nd in (False, True):
            @pl.when(jnp.logical_and(cur_diag if cd else jnp.logical_not(cur_diag),
                                     nxt_diag if nd else jnp.logical_not(nxt_diag)))
            def _(cd=cd, nd=nd):
                def loop_body(j, c):
                    body(2 * j, 0, cd, nd)
                    body(2 * j + 1, 1, cd, nd)
                    return c

                lax.fori_loop(0, (N_GROUPS - 2) // 2, loop_body, 0)
                body(N_GROUPS - 2, 0, cd, nd)
                body(N_GROUPS - 1, 1, cd, nd)
                if cd:
                    _attn_finalize(lam_ref, g_ref, acc_a, acc_b, out_ref)


def _meta_attn_kernel(coef_ref, q_ref, k_ref, va_ref, vb_ref, lam_ref, g_ref, out_ref,
                      m_ref, acc_a, acc_b, ahead_ref, mask_ref):
    _fill_masks(ahead_ref, mask_ref, tq=META_BLK, tk=META_BLK, pos_off=CHUNK - N_META)
    m_ref[...] = jnp.full(m_ref.shape, NEG, F32)
    acc_a[...] = jnp.zeros(acc_a.shape, F32)
    acc_b[...] = jnp.zeros(acc_b.shape, F32)

    def body(g, c):
        for u in range(PIPE_U):
            ht = _head_tile(u, g)
            s = lax.dot_general(q_ref[ht], k_ref[ht], NT_DIMS, preferred_element_type=F32)
            s = s + _diag_bias(u, g, coef_ref, ahead_ref, mask_ref)
            m_ref[ht], alpha, p = _softmax_update(s, m_ref[ht])
            v = va_ref[g >> 1] if u == 0 else vb_ref[g]
            _acc_update(u, g, acc_a, acc_b, alpha, jnp.dot(p, v, preferred_element_type=F32))
        return c

    lax.fori_loop(0, N_GROUPS, body, 0)
    _attn_finalize(lam_ref, g_ref, acc_a, acc_b, out_ref)


N_PIPE_BUFS = 2 * PIPE_U


def _pipe_bufs(refs):
    n = N_PIPE_BUFS
    nest = lambda flat: [list(flat[:PIPE_U]), list(flat[PIPE_U:])]
    return nest(refs[:n]), nest(refs[n:2 * n]), nest(refs[2 * n:3 * n])


def _attn_scratch(tq, tk, buf_w=None):
    state = [pltpu.VMEM((N_HT, tq, 1), F32), pltpu.VMEM((N_HM, tq, VA_W), F32),
             pltpu.VMEM((FOX_HEADS, tq, VB_W), F32),
             pltpu.VMEM((tq, tk), F32), pltpu.VMEM((2, tq, tk), F32)]
    if buf_w is None:
        return state
    return (state + [pltpu.VMEM((tq, buf_w), F32)] * N_PIPE_BUFS
            + [pltpu.VMEM((tq, buf_w), BF16)] * N_PIPE_BUFS
            + [pltpu.VMEM((tq, 1), F32)] * N_PIPE_BUFS)


def _attention(q_all, k_all, va, vb, lam, gain_row):
    coef = jnp.asarray([-2.0 * SLOPES[hm // 2] * LOG2E for hm in range(N_HM)], F32)
    smem = pl.BlockSpec(memory_space=pltpu.SMEM)
    meta_blk = NF // META_BLK

    def meta_spec(heads, w):
        return pl.BlockSpec((heads, META_BLK, w), lambda *_: (0, meta_blk, 0))

    const2 = lambda shape: pl.BlockSpec(shape, lambda *_: (0, 0))

    out0 = pl.pallas_call(
        _meta_attn_kernel,
        grid=(1,),
        in_specs=[smem, meta_spec(N_HT, LANES), meta_spec(N_HT, LANES),
                  meta_spec(DIFF_HEADS, VA_W), meta_spec(FOX_HEADS, VB_W),
                  const2((4, DIFF_HEAD_DIM)), const2((8, LANES))],
        out_specs=pl.BlockSpec((META_BLK, ATT_OUT_W), lambda s: (meta_blk, 0)),
        out_shape=jax.ShapeDtypeStruct((NT, ATT_OUT_W), BF16),
        scratch_shapes=_attn_scratch(META_BLK, META_BLK),
        compiler_params=_cparams(("arbitrary",)),
        name="attn_meta",
    )(coef, q_all, k_all, va, vb, lam, gain_row)

    n_q = SEQ // ATT_T
    qi_tab = np.concatenate([np.full(q + 1, q, np.int32) for q in range(n_q)])
    kv_tab = np.concatenate([np.arange(q + 1, dtype=np.int32) for q in range(n_q)])

    assert len(qi_tab) == STEPS_PER_BATCH

    def row_block(tab_of, ahead):
        def index(s, qi, kv):
            s = jnp.minimum(s + ahead, ATT_STEPS - 1)
            return (s // STEPS_PER_BATCH) * n_q + tab_of(qi, kv)[s % STEPS_PER_BATCH]
        return index

    q_blk, q_nxt = row_block(lambda qi, kv: qi, 0), row_block(lambda qi, kv: qi, 1)
    k_blk, k_nxt = row_block(lambda qi, kv: kv, 0), row_block(lambda qi, kv: kv, 1)

    def tile_spec(heads, w, blk, head_blk=0):
        return pl.BlockSpec((heads, ATT_T, w), lambda s, qi, kv: (head_blk, blk(s, qi, kv), 0))

    def lookahead_specs(blk):
        n = 2
        return [tile_spec(n, LANES, blk), tile_spec(n, LANES, blk, head_blk=N_HM // n)]

    def meta_keys_spec(heads, w):
        return pl.BlockSpec((heads, META_KEYS, w), lambda *_: (0, NF // META_KEYS, 0))

    grid_spec = pltpu.PrefetchScalarGridSpec(
        num_scalar_prefetch=2,
        grid=(ATT_STEPS,),
        in_specs=[smem, tile_spec(N_HT, LANES, q_blk), *lookahead_specs(q_nxt),
                  tile_spec(N_HT, LANES, k_blk), *lookahead_specs(k_nxt),
                  tile_spec(DIFF_HEADS, VA_W, k_blk), tile_spec(FOX_HEADS, VB_W, k_blk),
                  meta_keys_spec(N_HT, LANES), meta_keys_spec(DIFF_HEADS, VA_W), meta_keys_spec(FOX_HEADS, VB_W),
                  const2((4, DIFF_HEAD_DIM)), const2((8, LANES)),
                  pl.BlockSpec(memory_space=pl.ANY)],
        out_specs=pl.BlockSpec((ATT_T, ATT_OUT_W), lambda s, qi, kv: (q_blk(s, qi, kv), 0)),
        scratch_shapes=_attn_scratch(ATT_T, ATT_T, DIAG_TK),
    )
    return pl.pallas_call(
        _attn_kernel,
        grid_spec=grid_spec,
        out_shape=jax.ShapeDtypeStruct((NT, ATT_OUT_W), BF16),
        input_output_aliases={16: 0},
        compiler_params=_cparams(("arbitrary",)),
        name="attn_frames",
    )(jnp.asarray(qi_tab), jnp.asarray(kv_tab), coef,
      q_all, q_all, q_all, k_all, k_all, k_all, va, vb, k_all, va, vb, lam, gain_row, out0)


def _log_gamma(h):
    return jnp.log1p(jnp.full((1, 1), -(2.0 ** (-5.0 - h)), F32))


def _retention_kernel(q_ref, kt_ref, v_ref, g_ref, ktm_ref, vm_ref, o_ref, state_ref, decay_ref):
    b = pl.program_id(0)
    t = pl.program_id(1)
    T = RET_T

    @pl.when(jnp.logical_and(b == 0, t == 0))
    def _():
        i = lax.broadcasted_iota(jnp.int32, (T, T), 0)
        j = lax.broadcasted_iota(jnp.int32, (T, T), 1)
        dist = jnp.abs(i - j).astype(F32)
        vis = (j >> 6) <= (i >> 6)
        for h in range(RET_HEADS):
            decay_ref[h] = jnp.where(vis, jnp.exp(_log_gamma(h) * dist), 0.0)

    @pl.when(t == 0)
    def _():
        m = lax.broadcasted_iota(jnp.int32, (1, META_BLK), 1)
        for h in range(RET_HEADS):
            w = jnp.exp(_log_gamma(h) * (N_META - 1 - m).astype(F32))
            kt = ktm_ref[h * RET_QK_DIM:(h + 1) * RET_QK_DIM, :].astype(F32)
            kd = jnp.where(m < N_META, kt * w, 0.0).astype(BF16)
            state_ref[h] = jnp.dot(kd, vm_ref[:, h * RET_V_DIM:(h + 1) * RET_V_DIM],
                                   preferred_element_type=F32)

    row = lax.broadcasted_iota(jnp.int32, (T, 1), 0).astype(F32)
    col = lax.broadcasted_iota(jnp.int32, (1, T), 1).astype(F32)
    for h in range(RET_HEADS):
        lg = _log_gamma(h)
        q = q_ref[:, h * RET_QK_DIM:(h + 1) * RET_QK_DIM]
        kt = kt_ref[h * RET_QK_DIM:(h + 1) * RET_QK_DIM, :]
        v = v_ref[:, h * RET_V_DIM:(h + 1) * RET_V_DIM]
        scores = jnp.dot(q, kt, preferred_element_type=F32) * decay_ref[h]
        o = jnp.dot(scores.astype(BF16), v, preferred_element_type=F32)
        qd = (q.astype(F32) * jnp.exp(lg * (row + 1.0))).astype(BF16)
        state = state_ref[h]
        o = o + jnp.dot(qd, state.astype(BF16), preferred_element_type=F32)
        kd = (kt.astype(F32) * jnp.exp(lg * (T - 1.0 - col))).astype(BF16)
        state_ref[h] = jnp.exp(lg * float(T)) * state + jnp.dot(kd, v, preferred_element_type=F32)
        o = o * lax.rsqrt(jnp.mean(o * o, axis=1, keepdims=True) + RMS_EPS)
        gate = g_ref[:, h * RET_V_DIM:(h + 1) * RET_V_DIM].astype(F32)
        o_ref[:, h * RET_V_DIM:(h + 1) * RET_V_DIM] = (gate * jax.nn.sigmoid(gate) * o).astype(BF16)


def _retention(q, kt, v, g):
    n_t = SEQ // RET_T
    qk_w = RET_HEADS * RET_QK_DIM
    v_w = RET_HEADS * RET_V_DIM
    meta_blk = NF // META_BLK
    return pl.pallas_call(
        _retention_kernel,
        grid=(BATCH, n_t),
        in_specs=[pl.BlockSpec((RET_T, qk_w), lambda b, t: (b * n_t + t, 0)),
                  pl.BlockSpec((qk_w, RET_T), lambda b, t: (0, b * n_t + t)),
                  pl.BlockSpec((RET_T, v_w), lambda b, t: (b * n_t + t, 0)),
                  pl.BlockSpec((RET_T, v_w), lambda b, t: (b * n_t + t, 0)),
                  pl.BlockSpec((qk_w, META_BLK), lambda b, t: (0, meta_blk)),
                  pl.BlockSpec((META_BLK, v_w), lambda b, t: (meta_blk, 0))],
        out_specs=pl.BlockSpec((RET_T, v_w), lambda b, t: (b * n_t + t, 0)),
        out_shape=jax.ShapeDtypeStruct((NF, v_w), BF16),
        scratch_shapes=[pltpu.VMEM((RET_HEADS, RET_QK_DIM, RET_V_DIM), F32),
                        pltpu.VMEM((RET_HEADS, RET_T, RET_T), F32)],
        compiler_params=_cparams(("arbitrary", "arbitrary")),
        name="retention",
    )(q, kt, v, g, kt, v)


def _rows8(*rows):
    n = rows[0].shape[-1]
    out = jnp.zeros((8, n), F32)
    for i, r in enumerate(rows):
        out = out.at[i].set(r.astype(F32))
    return out


def _unit_coef(n):
    return _coef_rows(n, {})


TM_ALL = 2064
TM_LN_ALL = 768
TM_FRAMES = 1024


def kernel(x, meta_tokens, even_w_in, even_f_bias, diff_lambda, diff_subln_g, even_w_out,
           ret_w_in, ret_w_out, ln_g, ln_b, ffn_w1, ffn_w2):
    pad = jnp.zeros((META_BLK - N_META, D_MODEL), F32)
    h0 = jnp.concatenate([x.reshape(NF, D_MODEL), meta_tokens.astype(F32), pad], axis=0)

    q_all, k_all, va, vb, fb = _even_proj(h0, _even_weights(even_w_in[0]), _even_consts(), tm=TM_LN_ALL)
    f_bias_row = _rows8(jnp.pad(even_f_bias[0], (0, LANES - FOX_HEADS)))
    k_all = _fox_prep(fb, f_bias_row, k_all)

    attn = _attention(q_all, k_all, va, vb, diff_lambda[0].astype(F32), _rows8(diff_subln_g[0]))

    w_out = even_w_out[0]
    w_out = jnp.concatenate(
        [w_out[:DIFF_HEADS * DIFF_V_DIM],
         _pad_heads(w_out[DIFF_HEADS * DIFF_V_DIM:].T, FOX_HEADS, FOX_HEAD_DIM, LANES).T], axis=0).astype(BF16)
    h1, h1_16 = _mm_res_ln(attn, w_out, h0, _rows8(ln_g[0, 0], ln_b[0, 0]), rows=NT, tm=TM_LN_ALL,
                           name="even_out_ln")
    f1 = _mm_relu2(h1_16, ffn_w1[0].astype(BF16), rows=NT, tm=TM_ALL, tn=1024, name="ffn0_up")
    h2, h2_16 = _mm_res_ln(f1, ffn_w2[0].astype(BF16), h1, _rows8(ln_g[0, 1], ln_b[0, 1]),
                           rows=NT, tm=TM_LN_ALL, name="ffn0_down_ln")

    qk_w = RET_HEADS * RET_QK_DIM
    v_w = RET_HEADS * RET_V_DIM
    rw = ret_w_in[0]
    rq = _proj(h2_16, rw[:, :qk_w].astype(BF16), _unit_coef(qk_w), tm=TM_ALL, name="proj_ret_q")
    rkt = _proj_t((rw[:, qk_w:2 * qk_w] * RET_QK_DIM ** -0.5).T.astype(BF16), h2_16, tm=TM_LN_ALL)
    rv = _proj(h2_16, rw[:, 2 * qk_w:2 * qk_w + v_w].astype(BF16), _unit_coef(v_w), tm=TM_ALL, name="proj_ret_v")
    rg = _proj(h2_16, rw[:, 2 * qk_w + v_w:].astype(BF16), _unit_coef(v_w), tm=TM_ALL, name="proj_ret_g")
    y = _retention(rq, rkt, rv, rg)

    h3, h3_16 = _mm_res_ln(y, ret_w_out[0].astype(BF16), h2, _rows8(ln_g[1, 0], ln_b[1, 0]),
                           rows=NF, tm=TM_FRAMES, name="ret_out_ln")
    f2 = _mm_relu2(h3_16, ffn_w1[1].astype(BF16), rows=NF, tm=2048, tn=1024, name="ffn1_up")
    out, _ = _mm_res_ln(f2, ffn_w2[1].astype(BF16), h3, _rows8(ln_g[1, 1], ln_b[1, 1]),
                        rows=NF, tm=TM_FRAMES, name="ffn1_down_ln")
    return out.reshape(BATCH, SEQ, D_MODEL)
```

```python
import functools
import math

import jax
import jax.numpy as jnp
import numpy as np
from jax import lax
from jax.experimental import pallas as pl
from jax.experimental.pallas import tpu as pltpu

F32 = jnp.float32
BF16 = jnp.bfloat16

D_MODEL = 1024
BATCH = 4
SEQ = 8192
DEPTH = 2
CHUNK = 64
N_META = 16
DIFF_HEADS = 4
DIFF_HEAD_DIM = 64
DIFF_V_DIM = 128
FOX_HEADS = 8
FOX_HEAD_DIM = 64
RET_HEADS = 4
RET_QK_DIM = 256
RET_V_DIM = 512
D_FF = 4 * D_MODEL
DEEPNORM_ALPHA = (2 * DEPTH) ** 0.25
LN_EPS = 1e-5
RMS_EPS = 1e-6
LAM_INIT0 = 0.8 - 0.6 * math.exp(-0.3 * 0)
LOG2E = math.log2(math.e)

NF = BATCH * SEQ
META_BLK = 256
NT = NF + META_BLK
LANES = 128
NEG = -1e30

ATT_T = 512
META_KEYS = 128
RET_T = 512
VMEM_LIMIT = 56 * 1024 * 1024


def _cparams(sem):
    return pltpu.CompilerParams(dimension_semantics=sem, vmem_limit_bytes=VMEM_LIMIT)


def _bf16_pieces(x, n=3):
    out = []
    r = np.float32(x)
    for _ in range(n):
        p = np.float32(np.asarray(r, dtype=BF16).astype(np.float32))
        out.append(float(p))
        r = np.float32(r - p)
    return out


LOG2E_PIECES = _bf16_pieces(LOG2E)


def _row_positions(i, tm, row0=0):
    r = row0 + i * tm + lax.broadcasted_iota(jnp.int32, (tm, 1), 0)
    is_frame = r < NF
    m = r - NF
    pos = jnp.where(is_frame, (r & (SEQ - 1)) + N_META, m)
    valid = jnp.logical_or(is_frame, m < N_META)
    return pos, valid


def _proj_kernel(a_ref, w_ref, c_ref, o_ref, *, tm, heads, head_w, use_pos):
    y = jnp.dot(a_ref[...], w_ref[...], preferred_element_type=F32)
    y = y * c_ref[4:5, :] + c_ref[0:1, :]
    if use_pos:
        pos, valid = _row_positions(pl.program_id(0), tm)
        hi = (pos >> 7).astype(F32)
        lo = (pos & 127).astype(F32)
        y = y + hi * c_ref[1:2, :] + lo * c_ref[2:3, :] + jnp.where(valid, 0.0, 1.0) * c_ref[3:4, :]
    if heads is None:
        o_ref[...] = y.astype(o_ref.dtype)
    else:
        for h in range(heads):
            o_ref[h] = y[:, h * head_w:(h + 1) * head_w].astype(o_ref.dtype)


def _proj(a, w, coef, *, tm, name, heads=None, head_w=None, use_pos=False, out_dtype=BF16):
    m, k = a.shape
    n = w.shape[1]
    if heads is None:
        out_shape = jax.ShapeDtypeStruct((m, n), out_dtype)
        out_spec = pl.BlockSpec((tm, n), lambda i: (i, 0))
    else:
        out_shape = jax.ShapeDtypeStruct((heads, m, head_w), out_dtype)
        out_spec = pl.BlockSpec((heads, tm, head_w), lambda i: (0, i, 0))
    return pl.pallas_call(
        functools.partial(_proj_kernel, tm=tm, heads=heads, head_w=head_w, use_pos=use_pos),
        grid=(m // tm,),
        in_specs=[pl.BlockSpec((tm, k), lambda i: (i, 0)),
                  pl.BlockSpec((k, n), lambda i: (0, 0), pipeline_mode=pl.Buffered(1)),
                  pl.BlockSpec((8, n), lambda i: (0, 0))],
        out_specs=out_spec,
        out_shape=out_shape,
        compiler_params=_cparams(("arbitrary",)),
        name=name,
    )(a, w, coef)


def _proj_t_kernel(wt_ref, a_ref, o_ref):
    o_ref[...] = lax.dot_general(wt_ref[...], a_ref[...], (((1,), (1,)), ((), ())),
                                 preferred_element_type=F32).astype(o_ref.dtype)


def _proj_t(wt, a, *, tm):
    n, k = wt.shape
    m = a.shape[0]
    return pl.pallas_call(
        _proj_t_kernel,
        grid=(m // tm,),
        in_specs=[pl.BlockSpec((n, k), lambda i: (0, 0)),
                  pl.BlockSpec((tm, k), lambda i: (i, 0))],
        out_specs=pl.BlockSpec((n, tm), lambda i: (0, i)),
        out_shape=jax.ShapeDtypeStruct((n, m), BF16),
        compiler_params=_cparams(("arbitrary",)),
        name="proj_ret_kt",
    )(wt, a)


def _relu2_kernel(a_ref, w_ref, o_ref):
    y = jnp.dot(a_ref[...], w_ref[...], preferred_element_type=F32)
    y = jnp.maximum(y, 0.0)
    o_ref[...] = (y * y).astype(o_ref.dtype)


def _mm_relu2(a, w, *, rows, tm, tn, name):
    k = a.shape[1]
    n = w.shape[1]
    return pl.pallas_call(
        _relu2_kernel,
        grid=(rows // tm, n // tn),
        in_specs=[pl.BlockSpec((tm, k), lambda i, j: (i, 0)),
                  pl.BlockSpec((k, tn), lambda i, j: (0, j))],
        out_specs=pl.BlockSpec((tm, tn), lambda i, j: (i, j)),
        out_shape=jax.ShapeDtypeStruct((rows, n), BF16),
        compiler_params=_cparams(("arbitrary", "arbitrary")),
        name=name,
    )(a, w)


def _res_ln_kernel(a_ref, w_ref, r_ref, gb_ref, *refs):
    o32_ref, o16_ref = refs[-2:]
    y = jnp.dot(a_ref[...], w_ref[...], preferred_element_type=F32)
    z = DEEPNORM_ALPHA * r_ref[...] + y
    mu = jnp.mean(z, axis=-1, keepdims=True)
    zc = z - mu
    var = jnp.mean(zc * zc, axis=-1, keepdims=True)
    out = zc * lax.rsqrt(var + LN_EPS) * gb_ref[0:1, :] + gb_ref[1:2, :]
    o32_ref[...] = out
    o16_ref[...] = out.astype(BF16)


def _mm_res_ln(a, w, res, gb, *, rows, tm, name, a_row0=0, out_rows=None, into=None):
    k = a.shape[1]
    n = w.shape[1]
    blk0 = a_row0 // tm
    out_rows = rows if out_rows is None else out_rows
    n_alias = 0 if into is None else len(into)
    return pl.pallas_call(
        _res_ln_kernel,
        grid=(rows // tm,),
        in_specs=[pl.BlockSpec((tm, k), lambda i: (blk0 + i, 0)),
                  pl.BlockSpec((k, n), lambda i: (0, 0), pipeline_mode=pl.Buffered(1)),
                  pl.BlockSpec((tm, n), lambda i: (i, 0)),
                  pl.BlockSpec((8, n), lambda i: (0, 0))] + [pl.BlockSpec(memory_space=pl.ANY)] * n_alias,
        out_specs=[pl.BlockSpec((tm, n), lambda i: (blk0 + i, 0)),
                   pl.BlockSpec((tm, n), lambda i: (blk0 + i, 0))],
        out_shape=[jax.ShapeDtypeStruct((out_rows, n), F32),
                   jax.ShapeDtypeStruct((out_rows, n), BF16)],
        input_output_aliases={4 + j: j for j in range(n_alias)},
        compiler_params=_cparams(("arbitrary",)),
        name=name,
    )(a, w, res, gb, *(into or ()))


N_HM = 2 * DIFF_HEADS
N_HT = N_HM + FOX_HEADS
VA_W = 2 * LANES
VB_W = LANES
BIAS_LANE0 = DIFF_HEAD_DIM
DIFF_BETA_PIECES = 2
FOX_BETA_PIECES = 3
ATT_OUT_W = DIFF_HEADS * DIFF_V_DIM + FOX_HEADS * FOX_HEAD_DIM
SLOPES = [2.0 ** (-8.0 * (h + 1) / DIFF_HEADS) for h in range(DIFF_HEADS)]


def _pad_heads(w, heads, width, pad_to):
    k = w.shape[0]
    w = w.reshape(k, heads, width)
    w = jnp.pad(w, ((0, 0), (0, 0), (0, pad_to - width)))
    return w.reshape(k, heads * pad_to)


def _coef_rows(n, rows):
    c = np.zeros((8, n), np.float32)
    c[4, :] = 1.0
    for r, vals in rows.items():
        for col, val in vals:
            c[r, col] = val
    return jnp.asarray(c)


EVEN_QK_W = N_HT * DIFF_HEAD_DIM
EVEN_VA_W = DIFF_HEADS * DIFF_V_DIM
EVEN_VB_W = FOX_HEADS * FOX_HEAD_DIM
EVEN_N = 2 * EVEN_QK_W + EVEN_VA_W + EVEN_VB_W + LANES
Q_SCALE = DIFF_HEAD_DIM ** -0.5 * LOG2E
ROW_QCONST, ROW_KHI, ROW_KLO, ROW_KPAD, ROW_VCONST = 0, N_HT, 2 * N_HT, 3 * N_HT, 4 * N_HT


def _even_weights(w_in):
    qk_w = DIFF_HEADS * 2 * DIFF_HEAD_DIM
    v_w = DIFF_HEADS * DIFF_V_DIM
    fox_w = FOX_HEADS * FOX_HEAD_DIM
    o = np.cumsum([0, qk_w, qk_w, v_w, fox_w, fox_w, fox_w, FOX_HEADS])
    sl = [w_in[:, o[i]:o[i + 1]] for i in range(7)]
    w_fb = jnp.pad(sl[6], ((0, 0), (0, LANES - FOX_HEADS)))
    return jnp.concatenate([sl[0], sl[3], sl[1], sl[4], sl[2], sl[5], w_fb], axis=1).astype(BF16)


def _even_consts():
    c = np.zeros((4 * N_HT + 8, LANES), np.float32)
    for ht in range(N_HT):
        pieces = DIFF_BETA_PIECES if ht < N_HM else FOX_BETA_PIECES
        for p in range(pieces):
            for r in range(3):
                c[ROW_QCONST + ht, BIAS_LANE0 + 3 * p + r] = LOG2E_PIECES[r]
        if ht < N_HM:
            slope = SLOPES[ht // 2]
            c[ROW_KHI + ht, BIAS_LANE0:BIAS_LANE0 + 3] = slope * 128.0
            c[ROW_KLO + ht, BIAS_LANE0 + 3:BIAS_LANE0 + 6] = slope
            c[ROW_KPAD + ht, BIAS_LANE0] = NEG
    c[ROW_VCONST, 0] = 1.0
    c[ROW_VCONST + 1, FOX_HEAD_DIM] = 1.0
    return jnp.asarray(c)


def _even_proj_kernel(a_ref, w_ref, c_ref, *refs, tm, row0):
    q_ref, k_ref, va_ref, vb_ref, fb_ref = refs[-5:]
    y = jnp.dot(a_ref[...].astype(BF16), w_ref[...], preferred_element_type=F32)
    lane = lax.broadcasted_iota(jnp.int32, (1, LANES), 1)
    real = lane < DIFF_HEAD_DIM
    pos, valid = _row_positions(pl.program_id(0), tm, row0)
    hi = (pos >> 7).astype(F32)
    lo = (pos & 127).astype(F32)
    pad = jnp.where(valid, 0.0, 1.0)

    def spread(pair, odd):
        return pltpu.roll(pair, DIFF_HEAD_DIM, 1) if odd else pair

    for j in range(N_HT // 2):
        qp = y[:, j * LANES:(j + 1) * LANES] * Q_SCALE
        kp = y[:, EVEN_QK_W + j * LANES:EVEN_QK_W + (j + 1) * LANES]
        for odd in range(2):
            ht = 2 * j + odd
            row = lambda r: c_ref[r + ht:r + ht + 1, :]
            q_ref[ht] = jnp.where(real, spread(qp, odd), row(ROW_QCONST)).astype(BF16)
            k_bias = hi * row(ROW_KHI) + lo * row(ROW_KLO) + pad * row(ROW_KPAD) if ht < N_HM else 0.0
            k_ref[ht] = jnp.where(real, spread(kp, odd), k_bias).astype(BF16)
    base = 2 * EVEN_QK_W
    ones_half = jnp.broadcast_to(c_ref[ROW_VCONST:ROW_VCONST + 1, :], (tm, LANES)).astype(BF16)
    for h in range(DIFF_HEADS):
        va_ref[h, :, :LANES] = y[:, base + h * LANES:base + (h + 1) * LANES].astype(BF16)
        va_ref[h, :, LANES:] = ones_half
    base += EVEN_VA_W
    for j in range(FOX_HEADS // 2):
        vp = y[:, base + j * LANES:base + (j + 1) * LANES]
        for odd in range(2):
            vb_ref[2 * j + odd] = jnp.where(real, spread(vp, odd),
                                            c_ref[ROW_VCONST + 1:ROW_VCONST + 2, :]).astype(BF16)
    base += EVEN_VB_W
    fb_ref[...] = y[:, base:base + LANES]


def _even_proj(a, w, consts, *, tm, row0, into=None, name):
    m, k = a.shape
    blk0 = row0 // tm
    any_spec = pl.BlockSpec(memory_space=pl.ANY)
    n_alias = 0 if into is None else len(into)
    return pl.pallas_call(
        functools.partial(_even_proj_kernel, tm=tm, row0=row0),
        grid=(m // tm,),
        in_specs=[pl.BlockSpec((tm, k), lambda i: (i, 0)),
                  pl.BlockSpec((k, EVEN_N), lambda i: (0, 0), pipeline_mode=pl.Buffered(1)),
                  pl.BlockSpec(consts.shape, lambda i: (0, 0))] + [any_spec] * n_alias,
        out_specs=[pl.BlockSpec((N_HT, tm, LANES), lambda i: (0, blk0 + i, 0)),
                   pl.BlockSpec((N_HT, tm, LANES), lambda i: (0, blk0 + i, 0)),
                   pl.BlockSpec((DIFF_HEADS, tm, VA_W), lambda i: (0, blk0 + i, 0)),
                   pl.BlockSpec((FOX_HEADS, tm, VB_W), lambda i: (0, blk0 + i, 0)),
                   pl.BlockSpec((tm, LANES), lambda i: (blk0 + i, 0))],
        out_shape=[jax.ShapeDtypeStruct((N_HT, NT, LANES), BF16),
                   jax.ShapeDtypeStruct((N_HT, NT, LANES), BF16),
                   jax.ShapeDtypeStruct((DIFF_HEADS, NT, VA_W), BF16),
                   jax.ShapeDtypeStruct((FOX_HEADS, NT, VB_W), BF16),
                   jax.ShapeDtypeStruct((NT, LANES), F32)],
        input_output_aliases={3 + j: j for j in range(n_alias)},
        compiler_params=_cparams(("arbitrary",)),
        name=name,
    )(a, w, consts, *(into or ()))


PREP_T = 256


PIECE_LANES = 8


def _pack3(x):
    hi = x.astype(BF16).astype(F32)
    r1 = x - hi
    mid = r1.astype(BF16).astype(F32)
    lo = (r1 - mid).astype(BF16).astype(F32)
    return (hi + pltpu.roll(mid, PIECE_LANES, 1) + pltpu.roll(lo, 2 * PIECE_LANES, 1)).astype(BF16)


def _unpack3_sum(y):
    return y + pltpu.roll(y, LANES - PIECE_LANES, 1) + pltpu.roll(y, LANES - 2 * PIECE_LANES, 1)


def _fox_prep_kernel(fb_ref, bias_ref, tri_ref, sel_ref, kb_ref, o_ref, carry_ref, *, tile, meta):
    s = pl.program_id(0)

    @pl.when(s % (SEQ // tile) == 0)
    def _():
        carry_ref[...] = jnp.zeros_like(carry_ref)

    heads = lax.broadcasted_iota(jnp.int32, (1, LANES), 1) < FOX_HEADS
    x = fb_ref[...] + bias_ref[0:1, :]
    logf = jnp.minimum(x, 0.0) - jnp.log1p(jnp.exp(-jnp.abs(x)))
    part = jnp.dot(tri_ref[...], _pack3(jnp.where(heads, logf, 0.0)), preferred_element_type=F32)
    cum = carry_ref[0:1, :] + _unpack3_sum(part)
    carry_ref[0:1, :] = cum[tile - 1:tile, :]

    if meta:
        row = lax.broadcasted_iota(jnp.int32, (tile, 1), 0)
        g = jnp.where(row < N_META, cum[N_META - 1:N_META, :] - cum, NEG)
    else:
        g = -cum

    aug = jnp.dot(_pack3(jnp.where(heads, g, 0.0)), sel_ref[...], preferred_element_type=F32)
    for h in range(FOX_HEADS):
        o_ref[h] = (kb_ref[h].astype(F32) + aug[:, h * LANES:(h + 1) * LANES]).astype(BF16)


def _fox_prep(fb, bias_row, k_all):
    sel = np.zeros((LANES, FOX_HEADS * LANES), np.float32)
    for p in range(FOX_BETA_PIECES):
        for h in range(FOX_HEADS):
            for r in range(3):
                sel[PIECE_LANES * p + h, h * LANES + BIAS_LANE0 + 3 * p + r] = 1.0
    sel = jnp.asarray(sel, BF16)

    def run(k_in, tile, steps, blk0, meta, name):
        tri = jnp.tril(jnp.ones((tile, tile), F32)).astype(BF16)
        fox_blk = pl.BlockSpec((FOX_HEADS, tile, LANES), lambda s: (1, blk0 + s, 0))
        return pl.pallas_call(
            functools.partial(_fox_prep_kernel, tile=tile, meta=meta),
            grid=(steps,),
            in_specs=[pl.BlockSpec((tile, LANES), lambda s: (blk0 + s, 0)),
                      pl.BlockSpec((8, LANES), lambda s: (0, 0)),
                      pl.BlockSpec((tile, tile), lambda s: (0, 0)),
                      pl.BlockSpec((LANES, FOX_HEADS * LANES), lambda s: (0, 0)),
                      fox_blk],
            out_specs=fox_blk,
            out_shape=jax.ShapeDtypeStruct(k_all.shape, BF16),
            scratch_shapes=[pltpu.VMEM((8, LANES), F32)],
            input_output_aliases={4: 0},
            compiler_params=_cparams(("arbitrary",)),
            name=name,
        )(fb, bias_row, tri, sel, k_in)

    k_all = run(k_all, PREP_T, NF // PREP_T, 0, False, "fox_prep")
    return run(k_all, META_BLK, 1, NF // META_BLK, True, "fox_prep_meta")


def _fill_masks(ahead_ref, mask_ref, *, tq, tk, pos_off):
    i = lax.broadcasted_iota(jnp.int32, (tq, tk), 0)
    j = lax.broadcasted_iota(jnp.int32, (tq, tk), 1)
    ahead_ref[...] = jnp.maximum(j - i, 0).astype(F32)
    mask_ref[0] = jnp.where(((j + pos_off) >> 6) > ((i + pos_off) >> 6), NEG, 0.0)
    mask_ref[1] = jnp.where(j > i, NEG, 0.0)


def _softmax_update(s, m_prev):
    m_new = jnp.maximum(m_prev, jnp.max(s, axis=1, keepdims=True))
    alpha = jnp.exp2(m_prev - m_new)
    p = jnp.exp2(s - m_new).astype(BF16)
    return m_new, alpha, p


PIPE_U = 2
N_GROUPS = N_HT // PIPE_U
NT_DIMS = (((1,), (1,)), ((), ()))


def _head_tile(u, g):
    return g + N_HM * u


def _diag_bias(u, g, coef_ref, ahead_ref, mask_ref):
    if u == 0:
        return ahead_ref[...] * coef_ref[g] + mask_ref[0]
    return mask_ref[1]


def _acc_update(u, g, acc_a, acc_b, alpha, pv):
    if u == 0:
        acc_a[g] = alpha * acc_a[g] + pv
    else:
        acc_b[g] = alpha * acc_b[g] + pv


def _attn_finalize(lam_ref, g_ref, acc_a, acc_b, out_ref):
    lv = lam_ref[...]
    lam = (jnp.exp(jnp.sum(lv[0:1] * lv[1:2], axis=1, keepdims=True))
           - jnp.exp(jnp.sum(lv[2:3] * lv[3:4], axis=1, keepdims=True)) + LAM_INIT0)
    gain = g_ref[0:1, :] * (1.0 - LAM_INIT0)
    for h in range(DIFF_HEADS):
        a0 = acc_a[2 * h]
        a1 = acc_a[2 * h + 1]
        o = (a0[:, :DIFF_V_DIM] / a0[:, DIFF_V_DIM:DIFF_V_DIM + 1]
             - lam * (a1[:, :DIFF_V_DIM] / a1[:, DIFF_V_DIM:DIFF_V_DIM + 1]))
        ms = jnp.mean(o * o, axis=1, keepdims=True)
        out_ref[:, h * DIFF_V_DIM:(h + 1) * DIFF_V_DIM] = (o * lax.rsqrt(ms + RMS_EPS) * gain).astype(BF16)
    base = DIFF_HEADS * DIFF_V_DIM
    low = lax.broadcasted_iota(jnp.int32, (1, LANES), 1) < FOX_HEAD_DIM
    for j in range(FOX_HEADS // 2):
        even, odd = acc_b[2 * j], acc_b[2 * j + 1]
        even = even / even[:, FOX_HEAD_DIM:FOX_HEAD_DIM + 1]
        odd = odd / odd[:, FOX_HEAD_DIM:FOX_HEAD_DIM + 1]
        out_ref[:, base + j * LANES:base + (j + 1) * LANES] = jnp.where(
            low, even, pltpu.roll(odd, FOX_HEAD_DIM, 1)).astype(BF16)


STEPS_PER_BATCH = (SEQ // ATT_T) * (SEQ // ATT_T + 1) // 2
ATT_STEPS = BATCH * STEPS_PER_BATCH
DIAG_TK = ATT_T + META_KEYS


def _attn_kernel(qi_ref, kv_ref, coef_ref,
                 q_ref, qna_ref, qnb_ref, k_ref, kna_ref, knb_ref, va_ref, vb_ref, km_ref, vma_ref, vmb_ref,
                 lam_ref, g_ref, prev_ref,
                 out_ref, m_ref, acc_a, acc_b, ahead_ref, mask_ref, *pipe_refs):
    del prev_ref
    step = pl.program_id(0)
    nxt = jnp.minimum(step + 1, ATT_STEPS - 1)
    r, rn = step % STEPS_PER_BATCH, nxt % STEPS_PER_BATCH
    cur_diag = kv_ref[r] == qi_ref[r]
    nxt_diag = kv_ref[rn] == qi_ref[rn]
    cur_first = kv_ref[r] == 0
    nxt_first = kv_ref[rn] == 0
    s_bufs, p_bufs, a_bufs = _pipe_bufs(pipe_refs)
    T, MK = ATT_T, META_KEYS
    v_refs, vm_refs = (va_ref, vb_ref), (vma_ref, vmb_ref)
    cur_rows = lambda ref: (lambda u, g: ref[_head_tile(u, g)])
    nxt_rows = lambda refs: (lambda u, g: refs[u][g])

    def stage_qk(u, g, s_buf, q_of, k_of, diag):
        q = q_of(u, g)
        s = lax.dot_general(q, k_of(u, g), NT_DIMS, preferred_element_type=F32)
        if diag:
            s = s + _diag_bias(u, g, coef_ref, ahead_ref, mask_ref)
            s_buf[:, T:T + MK] = lax.dot_general(q, km_ref[_head_tile(u, g)], NT_DIMS,
                                                 preferred_element_type=F32)
        s_buf[:, :T] = s

    def stage_softmax(u, g, s_buf, p_buf, a_buf, diag, fresh=None):
        w = DIAG_TK if diag else T
        ht = _head_tile(u, g)
        m_prev = m_ref[ht]
        if fresh is not None:
            m_prev = jnp.where(fresh, NEG, m_prev)
        m_ref[ht], a_buf[...], p_buf[:, :w] = _softmax_update(s_buf[:, :w], m_prev)

    def stage_pv(u, g, p_buf, a_buf, diag):
        vi = g >> 1 if u == 0 else g
        pv = jnp.dot(p_buf[:, :T], v_refs[u][vi], preferred_element_type=F32)
        if diag:
            pv = pv + jnp.dot(p_buf[:, T:T + MK], vm_refs[u][vi], preferred_element_type=F32)
        _acc_update(u, g, acc_a, acc_b, a_buf[...], pv)

    def body(g, par, cd, nd):
        for u in range(PIPE_U):
            if isinstance(g, int) and g + 2 >= N_GROUPS:
                stage_qk(u, g + 2 - N_GROUPS, s_bufs[par][u],
                         nxt_rows((qna_ref, qnb_ref)), nxt_rows((kna_ref, knb_ref)), nd)
            else:
                stage_qk(u, g + 2, s_bufs[par][u], cur_rows(q_ref), cur_rows(k_ref), cd)
        for u in range(PIPE_U):
            stage_pv(u, g, p_bufs[par][u], a_bufs[par][u], cd)
        for u in range(PIPE_U):
            bufs = (s_bufs[1 - par][u], p_bufs[1 - par][u], a_bufs[1 - par][u])
            if isinstance(g, int) and g + 1 >= N_GROUPS:
                stage_softmax(u, g + 1 - N_GROUPS, *bufs, nd, fresh=nxt_first)
            else:
                stage_softmax(u, g + 1, *bufs, cd)

    @pl.when(cur_first)
    def _():
        acc_a[...] = jnp.zeros(acc_a.shape, F32)
        acc_b[...] = jnp.zeros(acc_b.shape, F32)
        fresh_m = jnp.full((N_HM - 1,) + m_ref.shape[1:], NEG, F32)
        m_ref[1:N_HM] = fresh_m
        m_ref[N_HM + 1:] = fresh_m

    @pl.when(step == 0)
    def _():
        _fill_masks(ahead_ref, mask_ref, tq=T, tk=T, pos_off=0)
        for u in range(PIPE_U):
            m_ref[_head_tile(u, 0)] = jnp.full(m_ref.shape[1:], NEG, F32)
        for g in range(2):
            for u in range(PIPE_U):
                stage_qk(u, g, s_bufs[g][u], cur_rows(q_ref), cur_rows(k_ref), True)
        for u in range(PIPE_U):
            stage_softmax(u, 0, s_bufs[0][u], p_bufs[0][u], a_bufs[0][u], True)

    for cd in (False, True):
        for nd in (False, True):
            @pl.when(jnp.logical_and(cur_diag if cd else jnp.logical_not(cur_diag),
                                     nxt_diag if nd else jnp.logical_not(nxt_diag)))
            def _(cd=cd, nd=nd):
                def loop_body(j, c):
                    body(2 * j, 0, cd, nd)
                    body(2 * j + 1, 1, cd, nd)
                    return c

                lax.fori_loop(0, (N_GROUPS - 2) // 2, loop_body, 0)
                body(N_GROUPS - 2, 0, cd, nd)
                body(N_GROUPS - 1, 1, cd, nd)
                if cd:
                    _attn_finalize(lam_ref, g_ref, acc_a, acc_b, out_ref)


def _meta_attn_kernel(coef_ref, q_ref, k_ref, va_ref, vb_ref, lam_ref, g_ref, out_ref,
                      m_ref, acc_a, acc_b, ahead_ref, mask_ref):
    _fill_masks(ahead_ref, mask_ref, tq=META_BLK, tk=META_BLK, pos_off=CHUNK - N_META)
    m_ref[...] = jnp.full(m_ref.shape, NEG, F32)
    acc_a[...] = jnp.zeros(acc_a.shape, F32)
    acc_b[...] = jnp.zeros(acc_b.shape, F32)

    def body(g, c):
        for u in range(PIPE_U):
            ht = _head_tile(u, g)
            s = lax.dot_general(q_ref[ht], k_ref[ht], NT_DIMS, preferred_element_type=F32)
            s = s + _diag_bias(u, g, coef_ref, ahead_ref, mask_ref)
            m_ref[ht], alpha, p = _softmax_update(s, m_ref[ht])
            v = va_ref[g >> 1] if u == 0 else vb_ref[g]
            _acc_update(u, g, acc_a, acc_b, alpha, jnp.dot(p, v, preferred_element_type=F32))
        return c

    lax.fori_loop(0, N_GROUPS, body, 0)
    _attn_finalize(lam_ref, g_ref, acc_a, acc_b, out_ref)


N_PIPE_BUFS = 2 * PIPE_U


def _pipe_bufs(refs):
    n = N_PIPE_BUFS
    nest = lambda flat: [list(flat[:PIPE_U]), list(flat[PIPE_U:])]
    return nest(refs[:n]), nest(refs[n:2 * n]), nest(refs[2 * n:3 * n])


def _attn_scratch(tq, tk, buf_w=None):
    state = [pltpu.VMEM((N_HT, tq, 1), F32), pltpu.VMEM((N_HM, tq, VA_W), F32),
             pltpu.VMEM((FOX_HEADS, tq, VB_W), F32),
             pltpu.VMEM((tq, tk), F32), pltpu.VMEM((2, tq, tk), F32)]
    if buf_w is None:
        return state
    return (state + [pltpu.VMEM((tq, buf_w), F32)] * N_PIPE_BUFS
            + [pltpu.VMEM((tq, buf_w), BF16)] * N_PIPE_BUFS
            + [pltpu.VMEM((tq, 1), F32)] * N_PIPE_BUFS)


def _attention(q_all, k_all, va, vb, lam, gain_row):
    coef = jnp.asarray([-2.0 * SLOPES[hm // 2] * LOG2E for hm in range(N_HM)], F32)
    smem = pl.BlockSpec(memory_space=pltpu.SMEM)
    meta_blk = NF // META_BLK

    def meta_spec(heads, w):
        return pl.BlockSpec((heads, META_BLK, w), lambda *_: (0, meta_blk, 0))

    const2 = lambda shape: pl.BlockSpec(shape, lambda *_: (0, 0))

    out0 = pl.pallas_call(
        _meta_attn_kernel,
        grid=(1,),
        in_specs=[smem, meta_spec(N_HT, LANES), meta_spec(N_HT, LANES),
                  meta_spec(DIFF_HEADS, VA_W), meta_spec(FOX_HEADS, VB_W),
                  const2((4, DIFF_HEAD_DIM)), const2((8, LANES))],
        out_specs=pl.BlockSpec((META_BLK, ATT_OUT_W), lambda s: (meta_blk, 0)),
        out_shape=jax.ShapeDtypeStruct((NT, ATT_OUT_W), BF16),
        scratch_shapes=_attn_scratch(META_BLK, META_BLK),
        compiler_params=_cparams(("arbitrary",)),
        name="attn_meta",
    )(coef, q_all, k_all, va, vb, lam, gain_row)

    n_q = SEQ // ATT_T
    qi_tab = np.concatenate([np.full(q + 1, q, np.int32) for q in range(n_q)])
    kv_tab = np.concatenate([np.arange(q + 1, dtype=np.int32) for q in range(n_q)])

    assert len(qi_tab) == STEPS_PER_BATCH

    def row_block(tab_of, ahead):
        def index(s, qi, kv):
            s = jnp.minimum(s + ahead, ATT_STEPS - 1)
            return (s // STEPS_PER_BATCH) * n_q + tab_of(qi, kv)[s % STEPS_PER_BATCH]
        return index

    q_blk, q_nxt = row_block(lambda qi, kv: qi, 0), row_block(lambda qi, kv: qi, 1)
    k_blk, k_nxt = row_block(lambda qi, kv: kv, 0), row_block(lambda qi, kv: kv, 1)

    def tile_spec(heads, w, blk, head_blk=0):
        return pl.BlockSpec((heads, ATT_T, w), lambda s, qi, kv: (head_blk, blk(s, qi, kv), 0))

    def lookahead_specs(blk):
        n = 2
        return [tile_spec(n, LANES, blk), tile_spec(n, LANES, blk, head_blk=N_HM // n)]

    def meta_keys_spec(heads, w):
        return pl.BlockSpec((heads, META_KEYS, w), lambda *_: (0, NF // META_KEYS, 0))

    grid_spec = pltpu.PrefetchScalarGridSpec(
        num_scalar_prefetch=2,
        grid=(ATT_STEPS,),
        in_specs=[smem, tile_spec(N_HT, LANES, q_blk), *lookahead_specs(q_nxt),
                  tile_spec(N_HT, LANES, k_blk), *lookahead_specs(k_nxt),
                  tile_spec(DIFF_HEADS, VA_W, k_blk), tile_spec(FOX_HEADS, VB_W, k_blk),
                  meta_keys_spec(N_HT, LANES), meta_keys_spec(DIFF_HEADS, VA_W), meta_keys_spec(FOX_HEADS, VB_W),
                  const2((4, DIFF_HEAD_DIM)), const2((8, LANES)),
                  pl.BlockSpec(memory_space=pl.ANY)],
        out_specs=pl.BlockSpec((ATT_T, ATT_OUT_W), lambda s, qi, kv: (q_blk(s, qi, kv), 0)),
        scratch_shapes=_attn_scratch(ATT_T, ATT_T, DIAG_TK),
    )
    return pl.pallas_call(
        _attn_kernel,
        grid_spec=grid_spec,
        out_shape=jax.ShapeDtypeStruct((NT, ATT_OUT_W), BF16),
        input_output_aliases={16: 0},
        compiler_params=_cparams(("arbitrary",)),
        name="attn_frames",
    )(jnp.asarray(qi_tab), jnp.asarray(kv_tab), coef,
      q_all, q_all, q_all, k_all, k_all, k_all, va, vb, k_all, va, vb, lam, gain_row, out0)


def _log_gamma(h):
    return jnp.log1p(jnp.full((1, 1), -(2.0 ** (-5.0 - h)), F32))


def _retention_kernel(q_ref, kt_ref, v_ref, g_ref, ktm_ref, vm_ref, o_ref, state_ref, decay_ref):
    b = pl.program_id(0)
    t = pl.program_id(1)
    T = RET_T

    @pl.when(jnp.logical_and(b == 0, t == 0))
    def _():
        i = lax.broadcasted_iota(jnp.int32, (T, T), 0)
        j = lax.broadcasted_iota(jnp.int32, (T, T), 1)
        dist = jnp.abs(i - j).astype(F32)
        vis = (j >> 6) <= (i >> 6)
        for h in range(RET_HEADS):
            decay_ref[h] = jnp.where(vis, jnp.exp(_log_gamma(h) * dist), 0.0)

    @pl.when(t == 0)
    def _():
        m = lax.broadcasted_iota(jnp.int32, (1, META_BLK), 1)
        for h in range(RET_HEADS):
            w = jnp.exp(_log_gamma(h) * (N_META - 1 - m).astype(F32))
            kt = ktm_ref[h * RET_QK_DIM:(h + 1) * RET_QK_DIM, :].astype(F32)
            kd = jnp.where(m < N_META, kt * w, 0.0).astype(BF16)
            state_ref[h] = jnp.dot(kd, vm_ref[:, h * RET_V_DIM:(h + 1) * RET_V_DIM],
                                   preferred_element_type=F32)

    row = lax.broadcasted_iota(jnp.int32, (T, 1), 0).astype(F32)
    col = lax.broadcasted_iota(jnp.int32, (1, T), 1).astype(F32)
    for h in range(RET_HEADS):
        lg = _log_gamma(h)
        q = q_ref[:, h * RET_QK_DIM:(h + 1) * RET_QK_DIM]
        kt = kt_ref[h * RET_QK_DIM:(h + 1) * RET_QK_DIM, :]
        v = v_ref[:, h * RET_V_DIM:(h + 1) * RET_V_DIM]
        scores = jnp.dot(q, kt, preferred_element_type=F32) * decay_ref[h]
        o = jnp.dot(scores.astype(BF16), v, preferred_element_type=F32)
        qd = (q.astype(F32) * jnp.exp(lg * (row + 1.0))).astype(BF16)
        state = state_ref[h]
        o = o + jnp.dot(qd, state.astype(BF16), preferred_element_type=F32)
        kd = (kt.astype(F32) * jnp.exp(lg * (T - 1.0 - col))).astype(BF16)
        state_ref[h] = jnp.exp(lg * float(T)) * state + jnp.dot(kd, v, preferred_element_type=F32)
        o = o * lax.rsqrt(jnp.mean(o * o, axis=1, keepdims=True) + RMS_EPS)
        gate = g_ref[:, h * RET_V_DIM:(h + 1) * RET_V_DIM].astype(F32)
        o_ref[:, h * RET_V_DIM:(h + 1) * RET_V_DIM] = (gate * jax.nn.sigmoid(gate) * o).astype(BF16)


def _retention(q, kt, v, g):
    n_t = SEQ // RET_T
    qk_w = RET_HEADS * RET_QK_DIM
    v_w = RET_HEADS * RET_V_DIM
    meta_blk = NF // META_BLK
    return pl.pallas_call(
        _retention_kernel,
        grid=(BATCH, n_t),
        in_specs=[pl.BlockSpec((RET_T, qk_w), lambda b, t: (b * n_t + t, 0)),
                  pl.BlockSpec((qk_w, RET_T), lambda b, t: (0, b * n_t + t)),
                  pl.BlockSpec((RET_T, v_w), lambda b, t: (b * n_t + t, 0)),
                  pl.BlockSpec((RET_T, v_w), lambda b, t: (b * n_t + t, 0)),
                  pl.BlockSpec((qk_w, META_BLK), lambda b, t: (0, meta_blk)),
                  pl.BlockSpec((META_BLK, v_w), lambda b, t: (meta_blk, 0))],
        out_specs=pl.BlockSpec((RET_T, v_w), lambda b, t: (b * n_t + t, 0)),
        out_shape=jax.ShapeDtypeStruct((NF, v_w), BF16),
        scratch_shapes=[pltpu.VMEM((RET_HEADS, RET_QK_DIM, RET_V_DIM), F32),
                        pltpu.VMEM((RET_HEADS, RET_T, RET_T), F32)],
        compiler_params=_cparams(("arbitrary", "arbitrary")),
        name="retention",
    )(q, kt, v, g, kt, v)


def _rows8(*rows):
    n = rows[0].shape[-1]
    out = jnp.zeros((8, n), F32)
    for i, r in enumerate(rows):
        out = out.at[i].set(r.astype(F32))
    return out


def _unit_coef(n):
    return _coef_rows(n, {})


TM_ALL = 2064
TM_LN_ALL = 768
TM_FRAMES = 1024


def kernel(x, meta_tokens, even_w_in, even_f_bias, diff_lambda, diff_subln_g, even_w_out,
           ret_w_in, ret_w_out, ln_g, ln_b, ffn_w1, ffn_w2):
    frames = x.reshape(NF, D_MODEL)
    meta = jnp.pad(meta_tokens.astype(F32), ((0, META_BLK - N_META), (0, 0)))

    w_even, c_even = _even_weights(even_w_in[0]), _even_consts()
    proj = _even_proj(frames, w_even, c_even, tm=512, row0=0, name="proj_even")
    q_all, k_all, va, vb, fb = _even_proj(meta, w_even, c_even, tm=META_BLK, row0=NF, into=proj,
                                          name="proj_even_meta")
    f_bias_row = _rows8(jnp.pad(even_f_bias[0], (0, LANES - FOX_HEADS)))
    k_all = _fox_prep(fb, f_bias_row, k_all)

    attn = _attention(q_all, k_all, va, vb, diff_lambda[0].astype(F32), _rows8(diff_subln_g[0]))

    w_out, gb = even_w_out[0].astype(BF16), _rows8(ln_g[0, 0], ln_b[0, 0])
    h1s = _mm_res_ln(attn, w_out, frames, gb, rows=NF, tm=TM_FRAMES, out_rows=NT, name="even_out_ln")
    h1, h1_16 = _mm_res_ln(attn, w_out, meta, gb, rows=META_BLK, tm=META_BLK, a_row0=NF, out_rows=NT,
                           into=h1s, name="even_out_ln_meta")
    f1 = _mm_relu2(h1_16, ffn_w1[0].astype(BF16), rows=NT, tm=TM_ALL, tn=1024, name="ffn0_up")
    h2, h2_16 = _mm_res_ln(f1, ffn_w2[0].astype(BF16), h1, _rows8(ln_g[0, 1], ln_b[0, 1]),
                           rows=NT, tm=TM_LN_ALL, name="ffn0_down_ln")

    qk_w = RET_HEADS * RET_QK_DIM
    v_w = RET_HEADS * RET_V_DIM
    rw = ret_w_in[0]
    rq = _proj(h2_16, rw[:, :qk_w].astype(BF16), _unit_coef(qk_w), tm=TM_ALL, name="proj_ret_q")
    rkt = _proj_t((rw[:, qk_w:2 * qk_w] * RET_QK_DIM ** -0.5).T.astype(BF16), h2_16, tm=TM_LN_ALL)
    rv = _proj(h2_16, rw[:, 2 * qk_w:2 * qk_w + v_w].astype(BF16), _unit_coef(v_w), tm=TM_ALL, name="proj_ret_v")
    rg = _proj(h2_16, rw[:, 2 * qk_w + v_w:].astype(BF16), _unit_coef(v_w), tm=TM_ALL, name="proj_ret_g")
    y = _retention(rq, rkt, rv, rg)

    h3, h3_16 = _mm_res_ln(y, ret_w_out[0].astype(BF16), h2, _rows8(ln_g[1, 0], ln_b[1, 0]),
                           rows=NF, tm=TM_FRAMES, name="ret_out_ln")
    f2 = _mm_relu2(h3_16, ffn_w1[1].astype(BF16), rows=NF, tm=2048, tn=1024, name="ffn1_up")
    out, _ = _mm_res_ln(f2, ffn_w2[1].astype(BF16), h3, _rows8(ln_g[1, 1], ln_b[1, 1]),
                        rows=NF, tm=TM_FRAMES, name="ffn1_down_ln")
    return out.reshape(BATCH, SEQ, D_MODEL)
```

```python
import functools
import math

import jax
import jax.numpy as jnp
import numpy as np
from jax import lax
from jax.experimental import pallas as pl
from jax.experimental.pallas import tpu as pltpu

F32 = jnp.float32
BF16 = jnp.bfloat16

D_MODEL = 1024
BATCH = 4
SEQ = 8192
DEPTH = 2
CHUNK = 64
N_META = 16
DIFF_HEADS = 4
DIFF_HEAD_DIM = 64
DIFF_V_DIM = 128
FOX_HEADS = 8
FOX_HEAD_DIM = 64
RET_HEADS = 4
RET_QK_DIM = 256
RET_V_DIM = 512
D_FF = 4 * D_MODEL
DEEPNORM_ALPHA = (2 * DEPTH) ** 0.25
LN_EPS = 1e-5
RMS_EPS = 1e-6
LAM_INIT0 = 0.8 - 0.6 * math.exp(-0.3 * 0)
LOG2E = math.log2(math.e)

NF = BATCH * SEQ
META_BLK = 256
NT = NF + META_BLK
LANES = 128
NEG = -1e30

ATT_T = 512
META_KEYS = 128
RET_T = 512
VMEM_LIMIT = 56 * 1024 * 1024


def _cparams(sem):
    return pltpu.CompilerParams(dimension_semantics=sem, vmem_limit_bytes=VMEM_LIMIT)


def _bf16_pieces(x, n=3):
    out = []
    r = np.float32(x)
    for _ in range(n):
        p = np.float32(np.asarray(r, dtype=BF16).astype(np.float32))
        out.append(float(p))
        r = np.float32(r - p)
    return out


LOG2E_PIECES = _bf16_pieces(LOG2E)


def _row_positions(i, tm, row0=0):
    r = row0 + i * tm + lax.broadcasted_iota(jnp.int32, (tm, 1), 0)
    is_frame = r < NF
    m = r - NF
    pos = jnp.where(is_frame, (r & (SEQ - 1)) + N_META, m)
    valid = jnp.logical_or(is_frame, m < N_META)
    return pos, valid


def _proj_kernel(a_ref, w_ref, c_ref, o_ref, *, tm, heads, head_w, use_pos):
    y = jnp.dot(a_ref[...], w_ref[...], preferred_element_type=F32)
    y = y * c_ref[4:5, :] + c_ref[0:1, :]
    if use_pos:
        pos, valid = _row_positions(pl.program_id(0), tm)
        hi = (pos >> 7).astype(F32)
        lo = (pos & 127).astype(F32)
        y = y + hi * c_ref[1:2, :] + lo * c_ref[2:3, :] + jnp.where(valid, 0.0, 1.0) * c_ref[3:4, :]
    if heads is None:
        o_ref[...] = y.astype(o_ref.dtype)
    else:
        for h in range(heads):
            o_ref[h] = y[:, h * head_w:(h + 1) * head_w].astype(o_ref.dtype)


def _proj(a, w, coef, *, tm, name, heads=None, head_w=None, use_pos=False, out_dtype=BF16):
    m, k = a.shape
    n = w.shape[1]
    if heads is None:
        out_shape = jax.ShapeDtypeStruct((m, n), out_dtype)
        out_spec = pl.BlockSpec((tm, n), lambda i: (i, 0))
    else:
        out_shape = jax.ShapeDtypeStruct((heads, m, head_w), out_dtype)
        out_spec = pl.BlockSpec((heads, tm, head_w), lambda i: (0, i, 0))
    return pl.pallas_call(
        functools.partial(_proj_kernel, tm=tm, heads=heads, head_w=head_w, use_pos=use_pos),
        grid=(m // tm,),
        in_specs=[pl.BlockSpec((tm, k), lambda i: (i, 0)),
                  pl.BlockSpec((k, n), lambda i: (0, 0), pipeline_mode=pl.Buffered(1)),
                  pl.BlockSpec((8, n), lambda i: (0, 0))],
        out_specs=out_spec,
        out_shape=out_shape,
        compiler_params=_cparams(("arbitrary",)),
        name=name,
    )(a, w, coef)


def _proj_t_kernel(wt_ref, a_ref, o_ref):
    o_ref[...] = lax.dot_general(wt_ref[...], a_ref[...], (((1,), (1,)), ((), ())),
                                 preferred_element_type=F32).astype(o_ref.dtype)


def _proj_t(wt, a, *, tm):
    n, k = wt.shape
    m = a.shape[0]
    return pl.pallas_call(
        _proj_t_kernel,
        grid=(m // tm,),
        in_specs=[pl.BlockSpec((n, k), lambda i: (0, 0)),
                  pl.BlockSpec((tm, k), lambda i: (i, 0))],
        out_specs=pl.BlockSpec((n, tm), lambda i: (0, i)),
        out_shape=jax.ShapeDtypeStruct((n, m), BF16),
        compiler_params=_cparams(("arbitrary",)),
        name="proj_ret_kt",
    )(wt, a)


def _relu2_kernel(a_ref, w_ref, o_ref):
    y = jnp.dot(a_ref[...], w_ref[...], preferred_element_type=F32)
    y = jnp.maximum(y, 0.0)
    o_ref[...] = (y * y).astype(o_ref.dtype)


def _mm_relu2(a, w, *, rows, tm, tn, name):
    k = a.shape[1]
    n = w.shape[1]
    return pl.pallas_call(
        _relu2_kernel,
        grid=(rows // tm, n // tn),
        in_specs=[pl.BlockSpec((tm, k), lambda i, j: (i, 0)),
                  pl.BlockSpec((k, tn), lambda i, j: (0, j))],
        out_specs=pl.BlockSpec((tm, tn), lambda i, j: (i, j)),
        out_shape=jax.ShapeDtypeStruct((rows, n), BF16),
        compiler_params=_cparams(("arbitrary", "arbitrary")),
        name=name,
    )(a, w)


def _res_ln_kernel(a_ref, w_ref, r_ref, gb_ref, *refs):
    o32_ref, o16_ref = refs[-2:]
    y = jnp.dot(a_ref[...], w_ref[...], preferred_element_type=F32)
    z = DEEPNORM_ALPHA * r_ref[...] + y
    mu = jnp.mean(z, axis=-1, keepdims=True)
    zc = z - mu
    var = jnp.mean(zc * zc, axis=-1, keepdims=True)
    out = zc * lax.rsqrt(var + LN_EPS) * gb_ref[0:1, :] + gb_ref[1:2, :]
    o32_ref[...] = out
    o16_ref[...] = out.astype(BF16)


def _mm_res_ln(a, w, res, gb, *, rows, tm, name, out_row0=0, out_rows=None, into=None, cover_tail=False):
    k = a.shape[1]
    n = w.shape[1]
    blk0 = out_row0 // tm
    steps = rows // tm
    out_rows = rows if out_rows is None else out_rows
    n_alias = 0 if into is None else len(into)
    src = lambda i: (jnp.minimum(i, steps - 1), 0)
    return pl.pallas_call(
        _res_ln_kernel,
        grid=(steps + int(cover_tail),),
        in_specs=[pl.BlockSpec((tm, k), src),
                  pl.BlockSpec((k, n), lambda i: (0, 0), pipeline_mode=pl.Buffered(1)),
                  pl.BlockSpec((tm, n), src),
                  pl.BlockSpec((8, n), lambda i: (0, 0))] + [pl.BlockSpec(memory_space=pl.ANY)] * n_alias,
        out_specs=[pl.BlockSpec((tm, n), lambda i: (blk0 + i, 0)),
                   pl.BlockSpec((tm, n), lambda i: (blk0 + i, 0))],
        out_shape=[jax.ShapeDtypeStruct((out_rows, n), F32),
                   jax.ShapeDtypeStruct((out_rows, n), BF16)],
        input_output_aliases={4 + j: j for j in range(n_alias)},
        compiler_params=_cparams(("arbitrary",)),
        name=name,
    )(a, w, res, gb, *(into or ()))


N_HM = 2 * DIFF_HEADS
N_HT = N_HM + FOX_HEADS
VA_W = 2 * LANES
VB_W = LANES
BIAS_LANE0 = DIFF_HEAD_DIM
DIFF_BETA_PIECES = 2
FOX_BETA_PIECES = 3
ATT_OUT_W = DIFF_HEADS * DIFF_V_DIM + FOX_HEADS * FOX_HEAD_DIM
SLOPES = [2.0 ** (-8.0 * (h + 1) / DIFF_HEADS) for h in range(DIFF_HEADS)]


def _pad_heads(w, heads, width, pad_to):
    k = w.shape[0]
    w = w.reshape(k, heads, width)
    w = jnp.pad(w, ((0, 0), (0, 0), (0, pad_to - width)))
    return w.reshape(k, heads * pad_to)


def _coef_rows(n, rows):
    c = np.zeros((8, n), np.float32)
    c[4, :] = 1.0
    for r, vals in rows.items():
        for col, val in vals:
            c[r, col] = val
    return jnp.asarray(c)


EVEN_QK_W = N_HT * DIFF_HEAD_DIM
EVEN_VA_W = DIFF_HEADS * DIFF_V_DIM
EVEN_VB_W = FOX_HEADS * FOX_HEAD_DIM
EVEN_N = 2 * EVEN_QK_W + EVEN_VA_W + EVEN_VB_W + LANES
Q_SCALE = DIFF_HEAD_DIM ** -0.5 * LOG2E
ROW_QCONST, ROW_KHI, ROW_KLO, ROW_KPAD, ROW_VCONST = 0, N_HT, 2 * N_HT, 3 * N_HT, 4 * N_HT


def _even_weights(w_in):
    qk_w = DIFF_HEADS * 2 * DIFF_HEAD_DIM
    v_w = DIFF_HEADS * DIFF_V_DIM
    fox_w = FOX_HEADS * FOX_HEAD_DIM
    o = np.cumsum([0, qk_w, qk_w, v_w, fox_w, fox_w, fox_w, FOX_HEADS])
    sl = [w_in[:, o[i]:o[i + 1]] for i in range(7)]
    w_fb = jnp.pad(sl[6], ((0, 0), (0, LANES - FOX_HEADS)))
    return jnp.concatenate([sl[0], sl[3], sl[1], sl[4], sl[2], sl[5], w_fb], axis=1).astype(BF16)


def _even_consts():
    c = np.zeros((4 * N_HT + 8, LANES), np.float32)
    for ht in range(N_HT):
        pieces = DIFF_BETA_PIECES if ht < N_HM else FOX_BETA_PIECES
        for p in range(pieces):
            for r in range(3):
                c[ROW_QCONST + ht, BIAS_LANE0 + 3 * p + r] = LOG2E_PIECES[r]
        if ht < N_HM:
            slope = SLOPES[ht // 2]
            c[ROW_KHI + ht, BIAS_LANE0:BIAS_LANE0 + 3] = slope * 128.0
            c[ROW_KLO + ht, BIAS_LANE0 + 3:BIAS_LANE0 + 6] = slope
            c[ROW_KPAD + ht, BIAS_LANE0] = NEG
    c[ROW_VCONST, 0] = 1.0
    c[ROW_VCONST + 1, FOX_HEAD_DIM] = 1.0
    return jnp.asarray(c)


PIECE_LANES = 8


def _pack3(x):
    hi = x.astype(BF16).astype(F32)
    r1 = x - hi
    mid = r1.astype(BF16).astype(F32)
    lo = (r1 - mid).astype(BF16).astype(F32)
    return (hi + pltpu.roll(mid, PIECE_LANES, 1) + pltpu.roll(lo, 2 * PIECE_LANES, 1)).astype(BF16)


def _unpack3_sum(y):
    return y + pltpu.roll(y, LANES - PIECE_LANES, 1) + pltpu.roll(y, LANES - 2 * PIECE_LANES, 1)


def _forget_bias_lanes(logits, fbias_ref, tri_ref, sel_ref, carry_ref, *, tm, meta):
    @pl.when(pl.program_id(0) % (SEQ // tm) == 0)
    def _():
        carry_ref[...] = jnp.zeros_like(carry_ref)

    heads = lax.broadcasted_iota(jnp.int32, (1, LANES), 1) < FOX_HEADS
    x = logits + fbias_ref[0:1, :]
    logf = jnp.minimum(x, 0.0) - jnp.log1p(jnp.exp(-jnp.abs(x)))
    part = jnp.dot(tri_ref[...], _pack3(jnp.where(heads, logf, 0.0)), preferred_element_type=F32)
    cum = carry_ref[0:1, :] + _unpack3_sum(part)
    carry_ref[0:1, :] = cum[tm - 1:tm, :]
    if meta:
        row = lax.broadcasted_iota(jnp.int32, (tm, 1), 0)
        g = jnp.where(row < N_META, cum[N_META - 1:N_META, :] - cum, NEG)
    else:
        g = -cum
    return jnp.dot(_pack3(jnp.where(heads, g, 0.0)), sel_ref[...], preferred_element_type=F32)


def _even_proj_kernel(a_ref, w_ref, c_ref, fbias_ref, tri_ref, sel_ref, *refs, tm, row0, meta):
    q_ref, k_ref, va_ref, vb_ref, carry_ref = refs[-5:]
    a = a_ref[...].astype(BF16)
    logits = jnp.dot(a, w_ref[:, EVEN_N - LANES:], preferred_element_type=F32)
    fox_bias = _forget_bias_lanes(logits, fbias_ref, tri_ref, sel_ref, carry_ref, tm=tm, meta=meta)
    y = jnp.dot(a, w_ref[:, :EVEN_N - LANES], preferred_element_type=F32)
    lane = lax.broadcasted_iota(jnp.int32, (1, LANES), 1)
    real = lane < DIFF_HEAD_DIM
    pos, valid = _row_positions(pl.program_id(0), tm, row0)
    hi = (pos >> 7).astype(F32)
    lo = (pos & 127).astype(F32)
    pad = jnp.where(valid, 0.0, 1.0)

    def spread(pair, odd):
        return pltpu.roll(pair, DIFF_HEAD_DIM, 1) if odd else pair

    for j in range(N_HT // 2):
        qp = y[:, j * LANES:(j + 1) * LANES] * Q_SCALE
        kp = y[:, EVEN_QK_W + j * LANES:EVEN_QK_W + (j + 1) * LANES]
        for odd in range(2):
            ht = 2 * j + odd
            row = lambda r: c_ref[r + ht:r + ht + 1, :]
            q_ref[ht] = jnp.where(real, spread(qp, odd), row(ROW_QCONST)).astype(BF16)
            if ht < N_HM:
                k_bias = hi * row(ROW_KHI) + lo * row(ROW_KLO) + pad * row(ROW_KPAD)
            else:
                k_bias = fox_bias[:, (ht - N_HM) * LANES:(ht - N_HM + 1) * LANES]
            k_ref[ht] = jnp.where(real, spread(kp, odd), k_bias).astype(BF16)
    base = 2 * EVEN_QK_W
    ones_half = jnp.broadcast_to(c_ref[ROW_VCONST:ROW_VCONST + 1, :], (tm, LANES)).astype(BF16)
    for h in range(DIFF_HEADS):
        va_ref[h, :, :LANES] = y[:, base + h * LANES:base + (h + 1) * LANES].astype(BF16)
        va_ref[h, :, LANES:] = ones_half
    base += EVEN_VA_W
    for j in range(FOX_HEADS // 2):
        vp = y[:, base + j * LANES:base + (j + 1) * LANES]
        for odd in range(2):
            vb_ref[2 * j + odd] = jnp.where(real, spread(vp, odd),
                                            c_ref[ROW_VCONST + 1:ROW_VCONST + 2, :]).astype(BF16)


def _even_proj(a, w, consts, fbias_row, *, tm, row0, meta, into=None, cover_tail=False, name):
    m, k = a.shape
    blk0 = row0 // tm
    steps = m // tm
    n_alias = 0 if into is None else len(into)
    tri = jnp.tril(jnp.ones((tm, tm), F32)).astype(BF16)
    sel = np.zeros((LANES, FOX_HEADS * LANES), np.float32)
    for p in range(FOX_BETA_PIECES):
        for h in range(FOX_HEADS):
            for r in range(3):
                sel[PIECE_LANES * p + h, h * LANES + BIAS_LANE0 + 3 * p + r] = 1.0
    sel = jnp.asarray(sel, BF16)
    const2 = lambda arr: pl.BlockSpec(arr.shape, lambda i: (0, 0))
    n_in = 6
    return pl.pallas_call(
        functools.partial(_even_proj_kernel, tm=tm, row0=row0, meta=meta),
        grid=(steps + int(cover_tail),),
        in_specs=[pl.BlockSpec((tm, k), lambda i: (jnp.minimum(i, steps - 1), 0)),
                  pl.BlockSpec((k, EVEN_N), lambda i: (0, 0), pipeline_mode=pl.Buffered(1)),
                  const2(consts), const2(fbias_row), const2(tri), const2(sel)]
                 + [pl.BlockSpec(memory_space=pl.ANY)] * n_alias,
        out_specs=[pl.BlockSpec((N_HT, tm, LANES), lambda i: (0, blk0 + i, 0)),
                   pl.BlockSpec((N_HT, tm, LANES), lambda i: (0, blk0 + i, 0)),
                   pl.BlockSpec((DIFF_HEADS, tm, VA_W), lambda i: (0, blk0 + i, 0)),
                   pl.BlockSpec((FOX_HEADS, tm, VB_W), lambda i: (0, blk0 + i, 0))],
        out_shape=[jax.ShapeDtypeStruct((N_HT, NT, LANES), BF16),
                   jax.ShapeDtypeStruct((N_HT, NT, LANES), BF16),
                   jax.ShapeDtypeStruct((DIFF_HEADS, NT, VA_W), BF16),
                   jax.ShapeDtypeStruct((FOX_HEADS, NT, VB_W), BF16)],
        scratch_shapes=[pltpu.VMEM((8, LANES), F32)],
        input_output_aliases={n_in + j: j for j in range(n_alias)},
        compiler_params=_cparams(("arbitrary",)),
        name=name,
    )(a, w, consts, fbias_row, tri, sel, *(into or ()))


def _fill_masks(ahead_ref, mask_ref, *, tq, tk, pos_off):
    i = lax.broadcasted_iota(jnp.int32, (tq, tk), 0)
    j = lax.broadcasted_iota(jnp.int32, (tq, tk), 1)
    ahead_ref[...] = jnp.maximum(j - i, 0).astype(F32)
    mask_ref[0] = jnp.where(((j + pos_off) >> 6) > ((i + pos_off) >> 6), NEG, 0.0)
    mask_ref[1] = jnp.where(j > i, NEG, 0.0)


def _softmax_update(s, m_prev):
    m_new = jnp.maximum(m_prev, jnp.max(s, axis=1, keepdims=True))
    alpha = jnp.exp2(m_prev - m_new)
    p = jnp.exp2(s - m_new).astype(BF16)
    return m_new, alpha, p


PIPE_U = 2
N_GROUPS = N_HT // PIPE_U
NT_DIMS = (((1,), (1,)), ((), ()))


def _head_tile(u, g):
    return g + N_HM * u


def _diag_bias(u, g, coef_ref, ahead_ref, mask_ref):
    if u == 0:
        return ahead_ref[...] * coef_ref[g] + mask_ref[0]
    return mask_ref[1]


def _acc_update(u, g, acc_a, acc_b, alpha, pv):
    if u == 0:
        acc_a[g] = alpha * acc_a[g] + pv
    else:
        acc_b[g] = alpha * acc_b[g] + pv


def _attn_finalize(lam_ref, g_ref, acc_a, acc_b, out_ref):
    lv = lam_ref[...]
    lam = (jnp.exp(jnp.sum(lv[0:1] * lv[1:2], axis=1, keepdims=True))
           - jnp.exp(jnp.sum(lv[2:3] * lv[3:4], axis=1, keepdims=True)) + LAM_INIT0)
    gain = g_ref[0:1, :] * (1.0 - LAM_INIT0)
    for h in range(DIFF_HEADS):
        a0 = acc_a[2 * h]
        a1 = acc_a[2 * h + 1]
        o = (a0[:, :DIFF_V_DIM] / a0[:, DIFF_V_DIM:DIFF_V_DIM + 1]
             - lam * (a1[:, :DIFF_V_DIM] / a1[:, DIFF_V_DIM:DIFF_V_DIM + 1]))
        ms = jnp.mean(o * o, axis=1, keepdims=True)
        out_ref[:, h * DIFF_V_DIM:(h + 1) * DIFF_V_DIM] = (o * lax.rsqrt(ms + RMS_EPS) * gain).astype(BF16)
    base = DIFF_HEADS * DIFF_V_DIM
    low = lax.broadcasted_iota(jnp.int32, (1, LANES), 1) < FOX_HEAD_DIM
    for j in range(FOX_HEADS // 2):
        even, odd = acc_b[2 * j], acc_b[2 * j + 1]
        even = even / even[:, FOX_HEAD_DIM:FOX_HEAD_DIM + 1]
        odd = odd / odd[:, FOX_HEAD_DIM:FOX_HEAD_DIM + 1]
        out_ref[:, base + j * LANES:base + (j + 1) * LANES] = jnp.where(
            low, even, pltpu.roll(odd, FOX_HEAD_DIM, 1)).astype(BF16)


STEPS_PER_BATCH = (SEQ // ATT_T) * (SEQ // ATT_T + 1) // 2
ATT_STEPS = BATCH * STEPS_PER_BATCH
DIAG_TK = ATT_T + META_KEYS


def _attn_kernel(qi_ref, kv_ref, coef_ref,
                 q_ref, qna_ref, qnb_ref, k_ref, kna_ref, knb_ref, va_ref, vb_ref, km_ref, vma_ref, vmb_ref,
                 lam_ref, g_ref,
                 out_ref, m_ref, acc_a, acc_b, ahead_ref, mask_ref, *pipe_refs):
    step = pl.program_id(0)
    nxt = jnp.minimum(step + 1, ATT_STEPS - 1)
    r, rn = step % STEPS_PER_BATCH, nxt % STEPS_PER_BATCH
    cur_diag = kv_ref[r] == qi_ref[r]
    nxt_diag = kv_ref[rn] == qi_ref[rn]
    cur_first = kv_ref[r] == 0
    nxt_first = kv_ref[rn] == 0
    s_bufs, p_bufs, a_bufs = _pipe_bufs(pipe_refs)
    T, MK = ATT_T, META_KEYS
    v_refs, vm_refs = (va_ref, vb_ref), (vma_ref, vmb_ref)
    cur_rows = lambda ref: (lambda u, g: ref[_head_tile(u, g)])
    nxt_rows = lambda refs: (lambda u, g: refs[u][g])

    def stage_qk(u, g, s_buf, q_of, k_of, diag):
        q = q_of(u, g)
        s = lax.dot_general(q, k_of(u, g), NT_DIMS, preferred_element_type=F32)
        if diag:
            s = s + _diag_bias(u, g, coef_ref, ahead_ref, mask_ref)
            s_buf[:, T:T + MK] = lax.dot_general(q, km_ref[_head_tile(u, g)], NT_DIMS,
                                                 preferred_element_type=F32)
        s_buf[:, :T] = s

    def stage_softmax(u, g, s_buf, p_buf, a_buf, diag, fresh=None):
        w = DIAG_TK if diag else T
        ht = _head_tile(u, g)
        m_prev = m_ref[ht]
        if fresh is not None:
            m_prev = jnp.where(fresh, NEG, m_prev)
        m_ref[ht], a_buf[...], p_buf[:, :w] = _softmax_update(s_buf[:, :w], m_prev)

    def stage_pv(u, g, p_buf, a_buf, diag):
        vi = g >> 1 if u == 0 else g
        pv = jnp.dot(p_buf[:, :T], v_refs[u][vi], preferred_element_type=F32)
        if diag:
            pv = pv + jnp.dot(p_buf[:, T:T + MK], vm_refs[u][vi], preferred_element_type=F32)
        _acc_update(u, g, acc_a, acc_b, a_buf[...], pv)

    def body(g, par, cd, nd):
        for u in range(PIPE_U):
            if isinstance(g, int) and g + 2 >= N_GROUPS:
                stage_qk(u, g + 2 - N_GROUPS, s_bufs[par][u],
                         nxt_rows((qna_ref, qnb_ref)), nxt_rows((kna_ref, knb_ref)), nd)
            else:
                stage_qk(u, g + 2, s_bufs[par][u], cur_rows(q_ref), cur_rows(k_ref), cd)
        for u in range(PIPE_U):
            stage_pv(u, g, p_bufs[par][u], a_bufs[par][u], cd)
        for u in range(PIPE_U):
            bufs = (s_bufs[1 - par][u], p_bufs[1 - par][u], a_bufs[1 - par][u])
            if isinstance(g, int) and g + 1 >= N_GROUPS:
                stage_softmax(u, g + 1 - N_GROUPS, *bufs, nd, fresh=nxt_first)
            else:
                stage_softmax(u, g + 1, *bufs, cd)

    @pl.when(cur_first)
    def _():
        acc_a[...] = jnp.zeros(acc_a.shape, F32)
        acc_b[...] = jnp.zeros(acc_b.shape, F32)
        fresh_m = jnp.full((N_HM - 1,) + m_ref.shape[1:], NEG, F32)
        m_ref[1:N_HM] = fresh_m
        m_ref[N_HM + 1:] = fresh_m

    @pl.when(step == 0)
    def _():
        _fill_masks(ahead_ref, mask_ref, tq=T, tk=T, pos_off=0)
        for u in range(PIPE_U):
            m_ref[_head_tile(u, 0)] = jnp.full(m_ref.shape[1:], NEG, F32)
        for g in range(2):
            for u in range(PIPE_U):
                stage_qk(u, g, s_bufs[g][u], cur_rows(q_ref), cur_rows(k_ref), True)
        for u in range(PIPE_U):
            stage_softmax(u, 0, s_bufs[0][u], p_bufs[0][u], a_bufs[0][u], True)

    for cd in (False, True):
        for nd in (False, True):
            @pl.when(jnp.logical_and(cur_diag if cd else jnp.logical_not(cur_diag),
                                     nxt_diag if nd else jnp.logical_not(nxt_diag)))
            def _(cd=cd, nd=nd):
                def loop_body(j, c):
                    body(2 * j, 0, cd, nd)
                    body(2 * j + 1, 1, cd, nd)
                    return c

                lax.fori_loop(0, (N_GROUPS - 2) // 2, loop_body, 0)
                body(N_GROUPS - 2, 0, cd, nd)
                body(N_GROUPS - 1, 1, cd, nd)
                if cd:
                    _attn_finalize(lam_ref, g_ref, acc_a, acc_b, out_ref)


def _meta_attn_kernel(coef_ref, q_ref, k_ref, va_ref, vb_ref, lam_ref, g_ref, out_ref,
                      m_ref, acc_a, acc_b, ahead_ref, mask_ref):
    _fill_masks(ahead_ref, mask_ref, tq=META_BLK, tk=META_BLK, pos_off=CHUNK - N_META)
    m_ref[...] = jnp.full(m_ref.shape, NEG, F32)
    acc_a[...] = jnp.zeros(acc_a.shape, F32)
    acc_b[...] = jnp.zeros(acc_b.shape, F32)

    def body(g, c):
        for u in range(PIPE_U):
            ht = _head_tile(u, g)
            s = lax.dot_general(q_ref[ht], k_ref[ht], NT_DIMS, preferred_element_type=F32)
            s = s + _diag_bias(u, g, coef_ref, ahead_ref, mask_ref)
            m_ref[ht], alpha, p = _softmax_update(s, m_ref[ht])
            v = va_ref[g >> 1] if u == 0 else vb_ref[g]
            _acc_update(u, g, acc_a, acc_b, alpha, jnp.dot(p, v, preferred_element_type=F32))
        return c

    lax.fori_loop(0, N_GROUPS, body, 0)
    _attn_finalize(lam_ref, g_ref, acc_a, acc_b, out_ref)


N_PIPE_BUFS = 2 * PIPE_U


def _pipe_bufs(refs):
    n = N_PIPE_BUFS
    nest = lambda flat: [list(flat[:PIPE_U]), list(flat[PIPE_U:])]
    return nest(refs[:n]), nest(refs[n:2 * n]), nest(refs[2 * n:3 * n])


def _attn_scratch(tq, tk, buf_w=None):
    state = [pltpu.VMEM((N_HT, tq, 1), F32), pltpu.VMEM((N_HM, tq, VA_W), F32),
             pltpu.VMEM((FOX_HEADS, tq, VB_W), F32),
             pltpu.VMEM((tq, tk), F32), pltpu.VMEM((2, tq, tk), F32)]
    if buf_w is None:
        return state
    return (state + [pltpu.VMEM((tq, buf_w), F32)] * N_PIPE_BUFS
            + [pltpu.VMEM((tq, buf_w), BF16)] * N_PIPE_BUFS
            + [pltpu.VMEM((tq, 1), F32)] * N_PIPE_BUFS)


def _attention(q_all, k_all, va, vb, lam, gain_row):
    coef = jnp.asarray([-2.0 * SLOPES[hm // 2] * LOG2E for hm in range(N_HM)], F32)
    smem = pl.BlockSpec(memory_space=pltpu.SMEM)
    meta_blk = NF // META_BLK

    def meta_spec(heads, w):
        return pl.BlockSpec((heads, META_BLK, w), lambda *_: (0, meta_blk, 0))

    const2 = lambda shape: pl.BlockSpec(shape, lambda *_: (0, 0))

    out_meta = pl.pallas_call(
        _meta_attn_kernel,
        grid=(1,),
        in_specs=[smem, meta_spec(N_HT, LANES), meta_spec(N_HT, LANES),
                  meta_spec(DIFF_HEADS, VA_W), meta_spec(FOX_HEADS, VB_W),
                  const2((4, DIFF_HEAD_DIM)), const2((8, LANES))],
        out_specs=pl.BlockSpec((META_BLK, ATT_OUT_W), lambda s: (0, 0)),
        out_shape=jax.ShapeDtypeStruct((META_BLK, ATT_OUT_W), BF16),
        scratch_shapes=_attn_scratch(META_BLK, META_BLK),
        compiler_params=_cparams(("arbitrary",)),
        name="attn_meta",
    )(coef, q_all, k_all, va, vb, lam, gain_row)

    n_q = SEQ // ATT_T
    qi_tab = np.concatenate([np.full(q + 1, q, np.int32) for q in range(n_q)])
    kv_tab = np.concatenate([np.arange(q + 1, dtype=np.int32) for q in range(n_q)])

    assert len(qi_tab) == STEPS_PER_BATCH

    def row_block(tab_of, ahead):
        def index(s, qi, kv):
            s = jnp.minimum(s + ahead, ATT_STEPS - 1)
            return (s // STEPS_PER_BATCH) * n_q + tab_of(qi, kv)[s % STEPS_PER_BATCH]
        return index

    q_blk, q_nxt = row_block(lambda qi, kv: qi, 0), row_block(lambda qi, kv: qi, 1)
    k_blk, k_nxt = row_block(lambda qi, kv: kv, 0), row_block(lambda qi, kv: kv, 1)

    def tile_spec(heads, w, blk, head_blk=0):
        return pl.BlockSpec((heads, ATT_T, w), lambda s, qi, kv: (head_blk, blk(s, qi, kv), 0))

    def lookahead_specs(blk):
        n = 2
        return [tile_spec(n, LANES, blk), tile_spec(n, LANES, blk, head_blk=N_HM // n)]

    def meta_keys_spec(heads, w):
        return pl.BlockSpec((heads, META_KEYS, w), lambda *_: (0, NF // META_KEYS, 0))

    grid_spec = pltpu.PrefetchScalarGridSpec(
        num_scalar_prefetch=2,
        grid=(ATT_STEPS,),
        in_specs=[smem, tile_spec(N_HT, LANES, q_blk), *lookahead_specs(q_nxt),
                  tile_spec(N_HT, LANES, k_blk), *lookahead_specs(k_nxt),
                  tile_spec(DIFF_HEADS, VA_W, k_blk), tile_spec(FOX_HEADS, VB_W, k_blk),
                  meta_keys_spec(N_HT, LANES), meta_keys_spec(DIFF_HEADS, VA_W), meta_keys_spec(FOX_HEADS, VB_W),
                  const2((4, DIFF_HEAD_DIM)), const2((8, LANES))],
        out_specs=pl.BlockSpec((ATT_T, ATT_OUT_W), lambda s, qi, kv: (q_blk(s, qi, kv), 0)),
        scratch_shapes=_attn_scratch(ATT_T, ATT_T, DIAG_TK),
    )
    out_frames = pl.pallas_call(
        _attn_kernel,
        grid_spec=grid_spec,
        out_shape=jax.ShapeDtypeStruct((NF, ATT_OUT_W), BF16),
        compiler_params=_cparams(("arbitrary",)),
        name="attn_frames",
    )(jnp.asarray(qi_tab), jnp.asarray(kv_tab), coef,
      q_all, q_all, q_all, k_all, k_all, k_all, va, vb, k_all, va, vb, lam, gain_row)
    return out_frames, out_meta


def _log_gamma(h):
    return jnp.log1p(jnp.full((1, 1), -(2.0 ** (-5.0 - h)), F32))


def _retention_kernel(q_ref, kt_ref, v_ref, g_ref, ktm_ref, vm_ref, o_ref, state_ref, decay_ref):
    b = pl.program_id(0)
    t = pl.program_id(1)
    T = RET_T

    @pl.when(jnp.logical_and(b == 0, t == 0))
    def _():
        i = lax.broadcasted_iota(jnp.int32, (T, T), 0)
        j = lax.broadcasted_iota(jnp.int32, (T, T), 1)
        dist = jnp.abs(i - j).astype(F32)
        vis = (j >> 6) <= (i >> 6)
        for h in range(RET_HEADS):
            decay_ref[h] = jnp.where(vis, jnp.exp(_log_gamma(h) * dist), 0.0)

    @pl.when(t == 0)
    def _():
        m = lax.broadcasted_iota(jnp.int32, (1, META_BLK), 1)
        for h in range(RET_HEADS):
            w = jnp.exp(_log_gamma(h) * (N_META - 1 - m).astype(F32))
            kt = ktm_ref[h * RET_QK_DIM:(h + 1) * RET_QK_DIM, :].astype(F32)
            kd = jnp.where(m < N_META, kt * w, 0.0).astype(BF16)
            state_ref[h] = jnp.dot(kd, vm_ref[:, h * RET_V_DIM:(h + 1) * RET_V_DIM],
                                   preferred_element_type=F32)

    row = lax.broadcasted_iota(jnp.int32, (T, 1), 0).astype(F32)
    col = lax.broadcasted_iota(jnp.int32, (1, T), 1).astype(F32)
    for h in range(RET_HEADS):
        lg = _log_gamma(h)
        q = q_ref[:, h * RET_QK_DIM:(h + 1) * RET_QK_DIM]
        kt = kt_ref[h * RET_QK_DIM:(h + 1) * RET_QK_DIM, :]
        v = v_ref[:, h * RET_V_DIM:(h + 1) * RET_V_DIM]
        scores = jnp.dot(q, kt, preferred_element_type=F32) * decay_ref[h]
        o = jnp.dot(scores.astype(BF16), v, preferred_element_type=F32)
        qd = (q.astype(F32) * jnp.exp(lg * (row + 1.0))).astype(BF16)
        state = state_ref[h]
        o = o + jnp.dot(qd, state.astype(BF16), preferred_element_type=F32)
        kd = (kt.astype(F32) * jnp.exp(lg * (T - 1.0 - col))).astype(BF16)
        state_ref[h] = jnp.exp(lg * float(T)) * state + jnp.dot(kd, v, preferred_element_type=F32)
        o = o * lax.rsqrt(jnp.mean(o * o, axis=1, keepdims=True) + RMS_EPS)
        gate = g_ref[:, h * RET_V_DIM:(h + 1) * RET_V_DIM].astype(F32)
        o_ref[:, h * RET_V_DIM:(h + 1) * RET_V_DIM] = (gate * jax.nn.sigmoid(gate) * o).astype(BF16)


def _retention(q, kt, v, g):
    n_t = SEQ // RET_T
    qk_w = RET_HEADS * RET_QK_DIM
    v_w = RET_HEADS * RET_V_DIM
    meta_blk = NF // META_BLK
    return pl.pallas_call(
        _retention_kernel,
        grid=(BATCH, n_t),
        in_specs=[pl.BlockSpec((RET_T, qk_w), lambda b, t: (b * n_t + t, 0)),
                  pl.BlockSpec((qk_w, RET_T), lambda b, t: (0, b * n_t + t)),
                  pl.BlockSpec((RET_T, v_w), lambda b, t: (b * n_t + t, 0)),
                  pl.BlockSpec((RET_T, v_w), lambda b, t: (b * n_t + t, 0)),
                  pl.BlockSpec((qk_w, META_BLK), lambda b, t: (0, meta_blk)),
                  pl.BlockSpec((META_BLK, v_w), lambda b, t: (meta_blk, 0))],
        out_specs=pl.BlockSpec((RET_T, v_w), lambda b, t: (b * n_t + t, 0)),
        out_shape=jax.ShapeDtypeStruct((NF, v_w), BF16),
        scratch_shapes=[pltpu.VMEM((RET_HEADS, RET_QK_DIM, RET_V_DIM), F32),
                        pltpu.VMEM((RET_HEADS, RET_T, RET_T), F32)],
        compiler_params=_cparams(("arbitrary", "arbitrary")),
        name="retention",
    )(q, kt, v, g, kt, v)


def _rows8(*rows):
    n = rows[0].shape[-1]
    out = jnp.zeros((8, n), F32)
    for i, r in enumerate(rows):
        out = out.at[i].set(r.astype(F32))
    return out


def _unit_coef(n):
    return _coef_rows(n, {})


TM_ALL = 2064
TM_LN_ALL = 768
TM_FRAMES = 1024


def kernel(x, meta_tokens, even_w_in, even_f_bias, diff_lambda, diff_subln_g, even_w_out,
           ret_w_in, ret_w_out, ln_g, ln_b, ffn_w1, ffn_w2):
    frames = x.reshape(NF, D_MODEL)
    meta = jnp.pad(meta_tokens.astype(F32), ((0, META_BLK - N_META), (0, 0)))

    w_even, c_even = _even_weights(even_w_in[0]), _even_consts()
    f_bias_row = _rows8(jnp.pad(even_f_bias[0], (0, LANES - FOX_HEADS)))
    proj = _even_proj(frames, w_even, c_even, f_bias_row, tm=512, row0=0, meta=False, cover_tail=True,
                      name="proj_even")
    q_all, k_all, va, vb = _even_proj(meta, w_even, c_even, f_bias_row, tm=META_BLK, row0=NF, meta=True,
                                      into=proj, name="proj_even_meta")

    attn, attn_meta = _attention(q_all, k_all, va, vb, diff_lambda[0].astype(F32), _rows8(diff_subln_g[0]))

    w_out, gb = even_w_out[0].astype(BF16), _rows8(ln_g[0, 0], ln_b[0, 0])
    h1s = _mm_res_ln(attn, w_out, frames, gb, rows=NF, tm=TM_FRAMES, out_rows=NT, cover_tail=True,
                     name="even_out_ln")
    h1, h1_16 = _mm_res_ln(attn_meta, w_out, meta, gb, rows=META_BLK, tm=META_BLK, out_row0=NF, out_rows=NT,
                           into=h1s, name="even_out_ln_meta")
    f1 = _mm_relu2(h1_16, ffn_w1[0].astype(BF16), rows=NT, tm=TM_ALL, tn=1024, name="ffn0_up")
    h2, h2_16 = _mm_res_ln(f1, ffn_w2[0].astype(BF16), h1, _rows8(ln_g[0, 1], ln_b[0, 1]),
                           rows=NT, tm=TM_LN_ALL, name="ffn0_down_ln")

    qk_w = RET_HEADS * RET_QK_DIM
    v_w = RET_HEADS * RET_V_DIM
    rw = ret_w_in[0]
    rq = _proj(h2_16, rw[:, :qk_w].astype(BF16), _unit_coef(qk_w), tm=TM_ALL, name="proj_ret_q")
    rkt = _proj_t((rw[:, qk_w:2 * qk_w] * RET_QK_DIM ** -0.5).T.astype(BF16), h2_16, tm=TM_LN_ALL)
    rv = _proj(h2_16, rw[:, 2 * qk_w:2 * qk_w + v_w].astype(BF16), _unit_coef(v_w), tm=TM_ALL, name="proj_ret_v")
    rg = _proj(h2_16, rw[:, 2 * qk_w + v_w:].astype(BF16), _unit_coef(v_w), tm=TM_ALL, name="proj_ret_g")
    y = _retention(rq, rkt, rv, rg)

    h3, h3_16 = _mm_res_ln(y, ret_w_out[0].astype(BF16), h2, _rows8(ln_g[1, 0], ln_b[1, 0]),
                           rows=NF, tm=TM_FRAMES, name="ret_out_ln")
    f2 = _mm_relu2(h3_16, ffn_w1[1].astype(BF16), rows=NF, tm=2048, tn=1024, name="ffn1_up")
    out, _ = _mm_res_ln(f2, ffn_w2[1].astype(BF16), h3, _rows8(ln_g[1, 1], ln_b[1, 1]),
                        rows=NF, tm=TM_FRAMES, name="ffn1_down_ln")
    return out.reshape(BATCH, SEQ, D_MODEL)
```

```python
import functools
import math

import jax
import jax.numpy as jnp
import numpy as np
from jax import lax
from jax.experimental import pallas as pl
from jax.experimental.pallas import tpu as pltpu

F32 = jnp.float32
BF16 = jnp.bfloat16

D_MODEL = 1024
BATCH = 4
SEQ = 8192
DEPTH = 2
CHUNK = 64
N_META = 16
DIFF_HEADS = 4
DIFF_HEAD_DIM = 64
DIFF_V_DIM = 128
FOX_HEADS = 8
FOX_HEAD_DIM = 64
RET_HEADS = 4
RET_QK_DIM = 256
RET_V_DIM = 512
D_FF = 4 * D_MODEL
DEEPNORM_ALPHA = (2 * DEPTH) ** 0.25
LN_EPS = 1e-5
RMS_EPS = 1e-6
LAM_INIT0 = 0.8 - 0.6 * math.exp(-0.3 * 0)
LOG2E = math.log2(math.e)

NF = BATCH * SEQ
META_BLK = 256
NT = NF + META_BLK
LANES = 128
NEG = -1e30

ATT_T = 512
META_KEYS = 128
RET_T = 512
VMEM_LIMIT = 56 * 1024 * 1024
ATT_VMEM_LIMIT = VMEM_LIMIT


def _cparams(sem, vmem_limit=VMEM_LIMIT, fuse_inputs=None):
    return pltpu.CompilerParams(dimension_semantics=sem, vmem_limit_bytes=vmem_limit,
                                allow_input_fusion=fuse_inputs)


def _bf16_pieces(x, n=3):
    out = []
    r = np.float32(x)
    for _ in range(n):
        p = np.float32(np.asarray(r, dtype=BF16).astype(np.float32))
        out.append(float(p))
        r = np.float32(r - p)
    return out


LOG2E_PIECES = _bf16_pieces(LOG2E)


def _row_positions(i, tm, row0=0):
    r = row0 + i * tm + lax.broadcasted_iota(jnp.int32, (tm, 1), 0)
    is_frame = r < NF
    m = r - NF
    pos = jnp.where(is_frame, (r & (SEQ - 1)) + N_META, m)
    valid = jnp.logical_or(is_frame, m < N_META)
    return pos, valid


def _proj_kernel(a_ref, w_ref, c_ref, o_ref, *, tm, heads, head_w, use_pos):
    y = jnp.dot(a_ref[...], w_ref[...], preferred_element_type=F32)
    y = y * c_ref[4:5, :] + c_ref[0:1, :]
    if use_pos:
        pos, valid = _row_positions(pl.program_id(0), tm)
        hi = (pos >> 7).astype(F32)
        lo = (pos & 127).astype(F32)
        y = y + hi * c_ref[1:2, :] + lo * c_ref[2:3, :] + jnp.where(valid, 0.0, 1.0) * c_ref[3:4, :]
    if heads is None:
        o_ref[...] = y.astype(o_ref.dtype)
    else:
        for h in range(heads):
            o_ref[h] = y[:, h * head_w:(h + 1) * head_w].astype(o_ref.dtype)


def _proj(a, w, coef, *, tm, name, heads=None, head_w=None, use_pos=False, out_dtype=BF16):
    m, k = a.shape
    n = w.shape[1]
    if heads is None:
        out_shape = jax.ShapeDtypeStruct((m, n), out_dtype)
        out_spec = pl.BlockSpec((tm, n), lambda i: (i, 0))
    else:
        out_shape = jax.ShapeDtypeStruct((heads, m, head_w), out_dtype)
        out_spec = pl.BlockSpec((heads, tm, head_w), lambda i: (0, i, 0))
    return pl.pallas_call(
        functools.partial(_proj_kernel, tm=tm, heads=heads, head_w=head_w, use_pos=use_pos),
        grid=(m // tm,),
        in_specs=[pl.BlockSpec((tm, k), lambda i: (i, 0)),
                  pl.BlockSpec((k, n), lambda i: (0, 0), pipeline_mode=pl.Buffered(1)),
                  pl.BlockSpec((8, n), lambda i: (0, 0))],
        out_specs=out_spec,
        out_shape=out_shape,
        compiler_params=_cparams(("arbitrary",), fuse_inputs=[False, True, False]),
        name=name,
    )(a, w, coef)


def _proj_t_kernel(wt_ref, a_ref, o_ref):
    o_ref[...] = lax.dot_general(wt_ref[...], a_ref[...], (((1,), (1,)), ((), ())),
                                 preferred_element_type=F32).astype(o_ref.dtype)


def _proj_t(wt, a, *, tm):
    n, k = wt.shape
    m = a.shape[0]
    return pl.pallas_call(
        _proj_t_kernel,
        grid=(m // tm,),
        in_specs=[pl.BlockSpec((n, k), lambda i: (0, 0)),
                  pl.BlockSpec((tm, k), lambda i: (i, 0))],
        out_specs=pl.BlockSpec((n, tm), lambda i: (0, i)),
        out_shape=jax.ShapeDtypeStruct((n, m), BF16),
        compiler_params=_cparams(("arbitrary",), fuse_inputs=[True, False]),
        name="proj_ret_kt",
    )(wt, a)


def _relu2_kernel(a_ref, w_ref, o_ref):
    y = jnp.dot(a_ref[...], w_ref[...], preferred_element_type=F32)
    y = jnp.maximum(y, 0.0)
    o_ref[...] = (y * y).astype(o_ref.dtype)


def _mm_relu2(a, w, *, rows, tm, tn, name):
    k = a.shape[1]
    n = w.shape[1]
    return pl.pallas_call(
        _relu2_kernel,
        grid=(rows // tm, n // tn),
        in_specs=[pl.BlockSpec((tm, k), lambda i, j: (i, 0)),
                  pl.BlockSpec((k, tn), lambda i, j: (0, j))],
        out_specs=pl.BlockSpec((tm, tn), lambda i, j: (i, j)),
        out_shape=jax.ShapeDtypeStruct((rows, n), BF16),
        compiler_params=_cparams(("arbitrary", "arbitrary"), fuse_inputs=[False, True]),
        name=name,
    )(a, w)


def _res_ln_kernel(a_ref, w_ref, r_ref, gb_ref, *refs):
    o32_ref, o16_ref = refs[-2:]
    y = jnp.dot(a_ref[...], w_ref[...], preferred_element_type=F32)
    z = DEEPNORM_ALPHA * r_ref[...] + y
    mu = jnp.mean(z, axis=-1, keepdims=True)
    zc = z - mu
    var = jnp.mean(zc * zc, axis=-1, keepdims=True)
    out = zc * lax.rsqrt(var + LN_EPS) * gb_ref[0:1, :] + gb_ref[1:2, :]
    o32_ref[...] = out
    o16_ref[...] = out.astype(BF16)


def _mm_res_ln(a, w, res, gb, *, rows, tm, name, out_row0=0, out_rows=None, into=None, cover_tail=False):
    k = a.shape[1]
    n = w.shape[1]
    blk0 = out_row0 // tm
    steps = rows // tm
    out_rows = rows if out_rows is None else out_rows
    n_alias = 0 if into is None else len(into)
    src = lambda i: (jnp.minimum(i, steps - 1), 0)
    return pl.pallas_call(
        _res_ln_kernel,
        grid=(steps + int(cover_tail),),
        in_specs=[pl.BlockSpec((tm, k), src),
                  pl.BlockSpec((k, n), lambda i: (0, 0), pipeline_mode=pl.Buffered(1)),
                  pl.BlockSpec((tm, n), src),
                  pl.BlockSpec((8, n), lambda i: (0, 0))] + [pl.BlockSpec(memory_space=pl.ANY)] * n_alias,
        out_specs=[pl.BlockSpec((tm, n), lambda i: (blk0 + i, 0)),
                   pl.BlockSpec((tm, n), lambda i: (blk0 + i, 0))],
        out_shape=[jax.ShapeDtypeStruct((out_rows, n), F32),
                   jax.ShapeDtypeStruct((out_rows, n), BF16)],
        input_output_aliases={4 + j: j for j in range(n_alias)},
        compiler_params=_cparams(("arbitrary",), fuse_inputs=None if n_alias else [False, True, False, False]),
        name=name,
    )(a, w, res, gb, *(into or ()))


N_HM = 2 * DIFF_HEADS
N_HT = N_HM + FOX_HEADS
VA_W = 2 * LANES
VB_W = LANES
BIAS_LANE0 = DIFF_HEAD_DIM
DIFF_BETA_PIECES = 2
FOX_BETA_PIECES = 3
ATT_OUT_W = DIFF_HEADS * DIFF_V_DIM + FOX_HEADS * FOX_HEAD_DIM
SLOPES = [2.0 ** (-8.0 * (h + 1) / DIFF_HEADS) for h in range(DIFF_HEADS)]


def _pad_heads(w, heads, width, pad_to):
    k = w.shape[0]
    w = w.reshape(k, heads, width)
    w = jnp.pad(w, ((0, 0), (0, 0), (0, pad_to - width)))
    return w.reshape(k, heads * pad_to)


def _coef_rows(n, rows):
    c = np.zeros((8, n), np.float32)
    c[4, :] = 1.0
    for r, vals in rows.items():
        for col, val in vals:
            c[r, col] = val
    return jnp.asarray(c)


EVEN_QK_W = N_HT * DIFF_HEAD_DIM
EVEN_VA_W = DIFF_HEADS * DIFF_V_DIM
EVEN_VB_W = FOX_HEADS * FOX_HEAD_DIM
EVEN_N = 2 * EVEN_QK_W + EVEN_VA_W + EVEN_VB_W + LANES
Q_SCALE = DIFF_HEAD_DIM ** -0.5 * LOG2E
ROW_QCONST, ROW_KHI, ROW_KLO, ROW_KPAD, ROW_VCONST = 0, N_HT, 2 * N_HT, 3 * N_HT, 4 * N_HT


def _even_weights(w_in):
    qk_w = DIFF_HEADS * 2 * DIFF_HEAD_DIM
    v_w = DIFF_HEADS * DIFF_V_DIM
    fox_w = FOX_HEADS * FOX_HEAD_DIM
    o = np.cumsum([0, qk_w, qk_w, v_w, fox_w, fox_w, fox_w, FOX_HEADS])
    sl = [w_in[:, o[i]:o[i + 1]] for i in range(7)]
    w_fb = jnp.pad(sl[6], ((0, 0), (0, LANES - FOX_HEADS)))
    return jnp.concatenate([sl[0], sl[3], sl[1], sl[4], sl[2], sl[5], w_fb], axis=1).astype(BF16)


def _even_consts():
    c = np.zeros((4 * N_HT + 8, LANES), np.float32)
    for ht in range(N_HT):
        pieces = DIFF_BETA_PIECES if ht < N_HM else FOX_BETA_PIECES
        for p in range(pieces):
            for r in range(3):
                c[ROW_QCONST + ht, BIAS_LANE0 + 3 * p + r] = LOG2E_PIECES[r]
        if ht < N_HM:
            slope = SLOPES[ht // 2]
            c[ROW_KHI + ht, BIAS_LANE0:BIAS_LANE0 + 3] = slope * 128.0
            c[ROW_KLO + ht, BIAS_LANE0 + 3:BIAS_LANE0 + 6] = slope
            c[ROW_KPAD + ht, BIAS_LANE0] = NEG
    c[ROW_VCONST, 0] = 1.0
    c[ROW_VCONST + 1, FOX_HEAD_DIM] = 1.0
    return jnp.asarray(c)


PIECE_LANES = 8


def _pack3(x):
    hi = x.astype(BF16).astype(F32)
    r1 = x - hi
    mid = r1.astype(BF16).astype(F32)
    lo = (r1 - mid).astype(BF16).astype(F32)
    return (hi + pltpu.roll(mid, PIECE_LANES, 1) + pltpu.roll(lo, 2 * PIECE_LANES, 1)).astype(BF16)


def _unpack3_sum(y):
    return y + pltpu.roll(y, LANES - PIECE_LANES, 1) + pltpu.roll(y, LANES - 2 * PIECE_LANES, 1)


def _forget_bias_lanes(logits, fbias_ref, tri_ref, sel_ref, carry_ref, *, tm, meta):
    @pl.when(pl.program_id(0) % (SEQ // tm) == 0)
    def _():
        carry_ref[...] = jnp.zeros_like(carry_ref)

    heads = lax.broadcasted_iota(jnp.int32, (1, LANES), 1) < FOX_HEADS
    x = logits + fbias_ref[0:1, :]
    logf = jnp.minimum(x, 0.0) - jnp.log1p(jnp.exp(-jnp.abs(x)))
    part = jnp.dot(tri_ref[...], _pack3(jnp.where(heads, logf, 0.0)), preferred_element_type=F32)
    cum = carry_ref[0:1, :] + _unpack3_sum(part)
    carry_ref[0:1, :] = cum[tm - 1:tm, :]
    if meta:
        row = lax.broadcasted_iota(jnp.int32, (tm, 1), 0)
        g = jnp.where(row < N_META, cum[N_META - 1:N_META, :] - cum, NEG)
    else:
        g = -cum
    return jnp.dot(_pack3(jnp.where(heads, g, 0.0)), sel_ref[...], preferred_element_type=F32)


def _even_proj_kernel(a_ref, w_ref, c_ref, fbias_ref, tri_ref, sel_ref, *refs, tm, row0, meta):
    q_ref, k_ref, va_ref, vb_ref, carry_ref = refs[-5:]
    a = a_ref[...].astype(BF16)
    logits = jnp.dot(a, w_ref[:, EVEN_N - LANES:], preferred_element_type=F32)
    fox_bias = _forget_bias_lanes(logits, fbias_ref, tri_ref, sel_ref, carry_ref, tm=tm, meta=meta)
    y = jnp.dot(a, w_ref[:, :EVEN_N - LANES], preferred_element_type=F32)
    lane = lax.broadcasted_iota(jnp.int32, (1, LANES), 1)
    real = lane < DIFF_HEAD_DIM
    pos, valid = _row_positions(pl.program_id(0), tm, row0)
    hi = (pos >> 7).astype(F32)
    lo = (pos & 127).astype(F32)
    pad = jnp.where(valid, 0.0, 1.0)

    def spread(pair, odd):
        return pltpu.roll(pair, DIFF_HEAD_DIM, 1) if odd else pair

    for j in range(N_HT // 2):
        qp = y[:, j * LANES:(j + 1) * LANES] * Q_SCALE
        kp = y[:, EVEN_QK_W + j * LANES:EVEN_QK_W + (j + 1) * LANES]
        for odd in range(2):
            ht = 2 * j + odd
            row = lambda r: c_ref[r + ht:r + ht + 1, :]
            q_ref[ht] = jnp.where(real, spread(qp, odd), row(ROW_QCONST)).astype(BF16)
            if ht < N_HM:
                k_bias = hi * row(ROW_KHI) + lo * row(ROW_KLO) + pad * row(ROW_KPAD)
            else:
                k_bias = fox_bias[:, (ht - N_HM) * LANES:(ht - N_HM + 1) * LANES]
            k_ref[ht] = jnp.where(real, spread(kp, odd), k_bias).astype(BF16)
    base = 2 * EVEN_QK_W
    ones_half = jnp.broadcast_to(c_ref[ROW_VCONST:ROW_VCONST + 1, :], (tm, LANES)).astype(BF16)
    for h in range(DIFF_HEADS):
        va_ref[h, :, :LANES] = y[:, base + h * LANES:base + (h + 1) * LANES].astype(BF16)
        va_ref[h, :, LANES:] = ones_half
    base += EVEN_VA_W
    for j in range(FOX_HEADS // 2):
        vp = y[:, base + j * LANES:base + (j + 1) * LANES]
        for odd in range(2):
            vb_ref[2 * j + odd] = jnp.where(real, spread(vp, odd),
                                            c_ref[ROW_VCONST + 1:ROW_VCONST + 2, :]).astype(BF16)


def _even_proj(a, w, consts, fbias_row, *, tm, row0, meta, into=None, cover_tail=False, name):
    m, k = a.shape
    blk0 = row0 // tm
    steps = m // tm
    n_alias = 0 if into is None else len(into)
    tri = jnp.tril(jnp.ones((tm, tm), F32)).astype(BF16)
    sel = np.zeros((LANES, FOX_HEADS * LANES), np.float32)
    for p in range(FOX_BETA_PIECES):
        for h in range(FOX_HEADS):
            for r in range(3):
                sel[PIECE_LANES * p + h, h * LANES + BIAS_LANE0 + 3 * p + r] = 1.0
    sel = jnp.asarray(sel, BF16)
    const2 = lambda arr: pl.BlockSpec(arr.shape, lambda i: (0, 0))
    n_in = 6
    return pl.pallas_call(
        functools.partial(_even_proj_kernel, tm=tm, row0=row0, meta=meta),
        grid=(steps + int(cover_tail),),
        in_specs=[pl.BlockSpec((tm, k), lambda i: (jnp.minimum(i, steps - 1), 0)),
                  pl.BlockSpec((k, EVEN_N), lambda i: (0, 0), pipeline_mode=pl.Buffered(1)),
                  const2(consts), const2(fbias_row), const2(tri), const2(sel)]
                 + [pl.BlockSpec(memory_space=pl.ANY)] * n_alias,
        out_specs=[pl.BlockSpec((N_HT, tm, LANES), lambda i: (0, blk0 + i, 0)),
                   pl.BlockSpec((N_HT, tm, LANES), lambda i: (0, blk0 + i, 0)),
                   pl.BlockSpec((DIFF_HEADS, tm, VA_W), lambda i: (0, blk0 + i, 0)),
                   pl.BlockSpec((FOX_HEADS, tm, VB_W), lambda i: (0, blk0 + i, 0))],
        out_shape=[jax.ShapeDtypeStruct((N_HT, NT, LANES), BF16),
                   jax.ShapeDtypeStruct((N_HT, NT, LANES), BF16),
                   jax.ShapeDtypeStruct((DIFF_HEADS, NT, VA_W), BF16),
                   jax.ShapeDtypeStruct((FOX_HEADS, NT, VB_W), BF16)],
        scratch_shapes=[pltpu.VMEM((8, LANES), F32)],
        input_output_aliases={n_in + j: j for j in range(n_alias)},
        compiler_params=_cparams(("arbitrary",)),
        name=name,
    )(a, w, consts, fbias_row, tri, sel, *(into or ()))


def _fill_masks(ahead_ref, mask_ref, *, tq, tk, pos_off):
    i = lax.broadcasted_iota(jnp.int32, (tq, tk), 0)
    j = lax.broadcasted_iota(jnp.int32, (tq, tk), 1)
    ahead_ref[...] = jnp.maximum(j - i, 0).astype(F32)
    mask_ref[0] = jnp.where(((j + pos_off) >> 6) > ((i + pos_off) >> 6), NEG, 0.0)
    mask_ref[1] = jnp.where(j > i, NEG, 0.0)


def _softmax_update(s, m_prev):
    m_new = jnp.maximum(m_prev, jnp.max(s, axis=1, keepdims=True))
    alpha = jnp.exp2(m_prev - m_new)
    p = jnp.exp2(s - m_new).astype(BF16)
    return m_new, alpha, p


PIPE_U = 2
PER_KIND = PIPE_U // 2
N_GROUPS = N_HT // PIPE_U
NT_DIMS = (((1,), (1,)), ((), ()))


def _kind(u):
    return 0 if u < PER_KIND else 1


def _local(u, g):
    return PER_KIND * g + u % PER_KIND


def _head_tile(u, g):
    return _local(u, g) + N_HM * _kind(u)


def _diag_bias(u, g, coef_ref, ahead_ref, mask_ref):
    if _kind(u) == 0:
        return ahead_ref[...] * coef_ref[_local(u, g)] + mask_ref[0]
    return mask_ref[1]


def _acc_update(u, g, acc_a, acc_b, alpha, pv):
    i = _local(u, g)
    if _kind(u) == 0:
        acc_a[i] = alpha * acc_a[i] + pv
    else:
        acc_b[i] = alpha * acc_b[i] + pv


def _attn_finalize(lam_ref, g_ref, acc_a, acc_b, out_ref):
    lv = lam_ref[...]
    lam = (jnp.exp(jnp.sum(lv[0:1] * lv[1:2], axis=1, keepdims=True))
           - jnp.exp(jnp.sum(lv[2:3] * lv[3:4], axis=1, keepdims=True)) + LAM_INIT0)
    gain = g_ref[0:1, :] * (1.0 - LAM_INIT0)
    for h in range(DIFF_HEADS):
        a0 = acc_a[2 * h]
        a1 = acc_a[2 * h + 1]
        o = (a0[:, :DIFF_V_DIM] / a0[:, DIFF_V_DIM:DIFF_V_DIM + 1]
             - lam * (a1[:, :DIFF_V_DIM] / a1[:, DIFF_V_DIM:DIFF_V_DIM + 1]))
        ms = jnp.mean(o * o, axis=1, keepdims=True)
        out_ref[:, h * DIFF_V_DIM:(h + 1) * DIFF_V_DIM] = (o * lax.rsqrt(ms + RMS_EPS) * gain).astype(BF16)
    base = DIFF_HEADS * DIFF_V_DIM
    low = lax.broadcasted_iota(jnp.int32, (1, LANES), 1) < FOX_HEAD_DIM
    for j in range(FOX_HEADS // 2):
        even, odd = acc_b[2 * j], acc_b[2 * j + 1]
        even = even / even[:, FOX_HEAD_DIM:FOX_HEAD_DIM + 1]
        odd = odd / odd[:, FOX_HEAD_DIM:FOX_HEAD_DIM + 1]
        out_ref[:, base + j * LANES:base + (j + 1) * LANES] = jnp.where(
            low, even, pltpu.roll(odd, FOX_HEAD_DIM, 1)).astype(BF16)


STEPS_PER_BATCH = (SEQ // ATT_T) * (SEQ // ATT_T + 1) // 2
ATT_STEPS = BATCH * STEPS_PER_BATCH
DIAG_TK = ATT_T + META_KEYS


def _attn_kernel(qi_ref, kv_ref, coef_ref,
                 q_ref, qna_ref, qnb_ref, k_ref, kna_ref, knb_ref, va_ref, vb_ref, km_ref, vma_ref, vmb_ref,
                 lam_ref, g_ref,
                 out_ref, m_ref, acc_a, acc_b, ahead_ref, mask_ref, *pipe_refs):
    step = pl.program_id(0)
    nxt = jnp.minimum(step + 1, ATT_STEPS - 1)
    r, rn = step % STEPS_PER_BATCH, nxt % STEPS_PER_BATCH
    cur_diag = kv_ref[r] == qi_ref[r]
    nxt_diag = kv_ref[rn] == qi_ref[rn]
    cur_first = kv_ref[r] == 0
    nxt_first = kv_ref[rn] == 0
    s_bufs, p_bufs, a_bufs = _pipe_bufs(pipe_refs)
    T, MK = ATT_T, META_KEYS
    v_refs, vm_refs = (va_ref, vb_ref), (vma_ref, vmb_ref)
    cur_rows = lambda ref: (lambda u, g: ref[_head_tile(u, g)])
    nxt_rows = lambda refs: (lambda u, g: refs[_kind(u)][_local(u, g)])

    def stage_qk(u, g, s_buf, q_of, k_of, diag):
        q = q_of(u, g)
        s = lax.dot_general(q, k_of(u, g), NT_DIMS, preferred_element_type=F32)
        if diag:
            s = s + _diag_bias(u, g, coef_ref, ahead_ref, mask_ref)
            s_buf[:, T:T + MK] = lax.dot_general(q, km_ref[_head_tile(u, g)], NT_DIMS,
                                                 preferred_element_type=F32)
        s_buf[:, :T] = s

    def stage_softmax(u, g, s_buf, p_buf, a_buf, diag, fresh=None):
        w = DIAG_TK if diag else T
        ht = _head_tile(u, g)
        m_prev = m_ref[ht]
        if fresh is not None:
            m_prev = jnp.where(fresh, NEG, m_prev)
        m_ref[ht], a_buf[...], p_buf[:, :w] = _softmax_update(s_buf[:, :w], m_prev)

    def stage_pv(u, g, p_buf, a_buf, diag):
        kind = _kind(u)
        vi = _local(u, g) >> 1 if kind == 0 else _local(u, g)
        pv = jnp.dot(p_buf[:, :T], v_refs[kind][vi], preferred_element_type=F32)
        if diag:
            pv = pv + jnp.dot(p_buf[:, T:T + MK], vm_refs[kind][vi], preferred_element_type=F32)
        _acc_update(u, g, acc_a, acc_b, a_buf[...], pv)

    def body(g, par, cd, nd):
        for u in range(PIPE_U):
            if isinstance(g, int) and g + 2 >= N_GROUPS:
                stage_qk(u, g + 2 - N_GROUPS, s_bufs[par][u],
                         nxt_rows((qna_ref, qnb_ref)), nxt_rows((kna_ref, knb_ref)), nd)
            else:
                stage_qk(u, g + 2, s_bufs[par][u], cur_rows(q_ref), cur_rows(k_ref), cd)
        for u in range(PIPE_U):
            stage_pv(u, g, p_bufs[par][u], a_bufs[par][u], cd)
        for u in range(PIPE_U):
            bufs = (s_bufs[1 - par][u], p_bufs[1 - par][u], a_bufs[1 - par][u])
            if isinstance(g, int) and g + 1 >= N_GROUPS:
                stage_softmax(u, g + 1 - N_GROUPS, *bufs, nd, fresh=nxt_first)
            else:
                stage_softmax(u, g + 1, *bufs, cd)

    @pl.when(cur_first)
    def _():
        acc_a[...] = jnp.zeros(acc_a.shape, F32)
        acc_b[...] = jnp.zeros(acc_b.shape, F32)
        fresh_m = jnp.full((N_HM - PER_KIND,) + m_ref.shape[1:], NEG, F32)
        m_ref[PER_KIND:N_HM] = fresh_m
        m_ref[N_HM + PER_KIND:] = fresh_m

    @pl.when(step == 0)
    def _():
        _fill_masks(ahead_ref, mask_ref, tq=T, tk=T, pos_off=0)
        for u in range(PIPE_U):
            m_ref[_head_tile(u, 0)] = jnp.full(m_ref.shape[1:], NEG, F32)
        for g in range(2):
            for u in range(PIPE_U):
                stage_qk(u, g, s_bufs[g][u], cur_rows(q_ref), cur_rows(k_ref), True)
        for u in range(PIPE_U):
            stage_softmax(u, 0, s_bufs[0][u], p_bufs[0][u], a_bufs[0][u], True)

    for cd in (False, True):
        for nd in (False, True):
            @pl.when(jnp.logical_and(cur_diag if cd else jnp.logical_not(cur_diag),
                                     nxt_diag if nd else jnp.logical_not(nxt_diag)))
            def _(cd=cd, nd=nd):
                def loop_body(j, c):
                    body(2 * j, 0, cd, nd)
                    body(2 * j + 1, 1, cd, nd)
                    return c

                lax.fori_loop(0, (N_GROUPS - 2) // 2, loop_body, 0)
                body(N_GROUPS - 2, 0, cd, nd)
                body(N_GROUPS - 1, 1, cd, nd)
                if cd:
                    _attn_finalize(lam_ref, g_ref, acc_a, acc_b, out_ref)


def _meta_attn_kernel(coef_ref, q_ref, k_ref, va_ref, vb_ref, lam_ref, g_ref, out_ref,
                      m_ref, acc_a, acc_b, ahead_ref, mask_ref):
    _fill_masks(ahead_ref, mask_ref, tq=META_BLK, tk=META_BLK, pos_off=CHUNK - N_META)
    m_ref[...] = jnp.full(m_ref.shape, NEG, F32)
    acc_a[...] = jnp.zeros(acc_a.shape, F32)
    acc_b[...] = jnp.zeros(acc_b.shape, F32)

    def body(g, c):
        for u in range(PIPE_U):
            ht = _head_tile(u, g)
            s = lax.dot_general(q_ref[ht], k_ref[ht], NT_DIMS, preferred_element_type=F32)
            s = s + _diag_bias(u, g, coef_ref, ahead_ref, mask_ref)
            m_ref[ht], alpha, p = _softmax_update(s, m_ref[ht])
            v = va_ref[_local(u, g) >> 1] if _kind(u) == 0 else vb_ref[_local(u, g)]
            _acc_update(u, g, acc_a, acc_b, alpha, jnp.dot(p, v, preferred_element_type=F32))
        return c

    lax.fori_loop(0, N_GROUPS, body, 0)
    _attn_finalize(lam_ref, g_ref, acc_a, acc_b, out_ref)


N_PIPE_BUFS = 2 * PIPE_U


def _pipe_bufs(refs):
    n = N_PIPE_BUFS
    nest = lambda flat: [list(flat[:PIPE_U]), list(flat[PIPE_U:])]
    return nest(refs[:n]), nest(refs[n:2 * n]), nest(refs[2 * n:3 * n])


def _attn_scratch(tq, tk, buf_w=None):
    state = [pltpu.VMEM((N_HT, tq, 1), F32), pltpu.VMEM((N_HM, tq, VA_W), F32),
             pltpu.VMEM((FOX_HEADS, tq, VB_W), F32),
             pltpu.VMEM((tq, tk), F32), pltpu.VMEM((2, tq, tk), F32)]
    if buf_w is None:
        return state
    return (state + [pltpu.VMEM((tq, buf_w), F32)] * N_PIPE_BUFS
            + [pltpu.VMEM((tq, buf_w), BF16)] * N_PIPE_BUFS
            + [pltpu.VMEM((tq, 1), F32)] * N_PIPE_BUFS)


def _attention(q_all, k_all, va, vb, lam, gain_row):
    coef = jnp.asarray([-2.0 * SLOPES[hm // 2] * LOG2E for hm in range(N_HM)], F32)
    smem = pl.BlockSpec(memory_space=pltpu.SMEM)
    meta_blk = NF // META_BLK

    def meta_spec(heads, w):
        return pl.BlockSpec((heads, META_BLK, w), lambda *_: (0, meta_blk, 0))

    const2 = lambda shape: pl.BlockSpec(shape, lambda *_: (0, 0))

    out_meta = pl.pallas_call(
        _meta_attn_kernel,
        grid=(1,),
        in_specs=[smem, meta_spec(N_HT, LANES), meta_spec(N_HT, LANES),
                  meta_spec(DIFF_HEADS, VA_W), meta_spec(FOX_HEADS, VB_W),
                  const2((4, DIFF_HEAD_DIM)), const2((8, LANES))],
        out_specs=pl.BlockSpec((META_BLK, ATT_OUT_W), lambda s: (0, 0)),
        out_shape=jax.ShapeDtypeStruct((META_BLK, ATT_OUT_W), BF16),
        scratch_shapes=_attn_scratch(META_BLK, META_BLK),
        compiler_params=_cparams(("arbitrary",)),
        name="attn_meta",
    )(coef, q_all, k_all, va, vb, lam, gain_row)

    n_q = SEQ // ATT_T
    qi_tab = np.concatenate([np.full(q + 1, q, np.int32) for q in range(n_q)])
    kv_tab = np.concatenate([np.arange(q + 1, dtype=np.int32) for q in range(n_q)])

    assert len(qi_tab) == STEPS_PER_BATCH

    def row_block(tab_of, ahead):
        def index(s, qi, kv):
            s = jnp.minimum(s + ahead, ATT_STEPS - 1)
            return (s // STEPS_PER_BATCH) * n_q + tab_of(qi, kv)[s % STEPS_PER_BATCH]
        return index

    q_blk, q_nxt = row_block(lambda qi, kv: qi, 0), row_block(lambda qi, kv: qi, 1)
    k_blk, k_nxt = row_block(lambda qi, kv: kv, 0), row_block(lambda qi, kv: kv, 1)

    def tile_spec(heads, w, blk, head_blk=0):
        return pl.BlockSpec((heads, ATT_T, w), lambda s, qi, kv: (head_blk, blk(s, qi, kv), 0))

    def lookahead_specs(blk):
        n = 2 * PER_KIND
        return [tile_spec(n, LANES, blk), tile_spec(n, LANES, blk, head_blk=N_HM // n)]

    def meta_keys_spec(heads, w):
        return pl.BlockSpec((heads, META_KEYS, w), lambda *_: (0, NF // META_KEYS, 0))

    grid_spec = pltpu.PrefetchScalarGridSpec(
        num_scalar_prefetch=2,
        grid=(ATT_STEPS,),
        in_specs=[smem, tile_spec(N_HT, LANES, q_blk), *lookahead_specs(q_nxt),
                  tile_spec(N_HT, LANES, k_blk), *lookahead_specs(k_nxt),
                  tile_spec(DIFF_HEADS, VA_W, k_blk), tile_spec(FOX_HEADS, VB_W, k_blk),
                  meta_keys_spec(N_HT, LANES), meta_keys_spec(DIFF_HEADS, VA_W), meta_keys_spec(FOX_HEADS, VB_W),
                  const2((4, DIFF_HEAD_DIM)), const2((8, LANES))],
        out_specs=pl.BlockSpec((ATT_T, ATT_OUT_W), lambda s, qi, kv: (q_blk(s, qi, kv), 0)),
        scratch_shapes=_attn_scratch(ATT_T, ATT_T, DIAG_TK),
    )
    out_frames = pl.pallas_call(
        _attn_kernel,
        grid_spec=grid_spec,
        out_shape=jax.ShapeDtypeStruct((NF, ATT_OUT_W), BF16),
        compiler_params=_cparams(("arbitrary",), ATT_VMEM_LIMIT),
        name="attn_frames",
    )(jnp.asarray(qi_tab), jnp.asarray(kv_tab), coef,
      q_all, q_all, q_all, k_all, k_all, k_all, va, vb, k_all, va, vb, lam, gain_row)
    return out_frames, out_meta


def _log_gamma(h):
    return jnp.log1p(jnp.full((1, 1), -(2.0 ** (-5.0 - h)), F32))


def _retention_kernel(q_ref, kt_ref, v_ref, g_ref, ktm_ref, vm_ref, o_ref, state_ref, decay_ref):
    b = pl.program_id(0)
    t = pl.program_id(1)
    T = RET_T

    @pl.when(jnp.logical_and(b == 0, t == 0))
    def _():
        i = lax.broadcasted_iota(jnp.int32, (T, T), 0)
        j = lax.broadcasted_iota(jnp.int32, (T, T), 1)
        dist = jnp.abs(i - j).astype(F32)
        vis = (j >> 6) <= (i >> 6)
        for h in range(RET_HEADS):
            decay_ref[h] = jnp.where(vis, jnp.exp(_log_gamma(h) * dist), 0.0)

    @pl.when(t == 0)
    def _():
        m = lax.broadcasted_iota(jnp.int32, (1, META_BLK), 1)
        for h in range(RET_HEADS):
            w = jnp.exp(_log_gamma(h) * (N_META - 1 - m).astype(F32))
            kt = ktm_ref[h * RET_QK_DIM:(h + 1) * RET_QK_DIM, :].astype(F32)
            kd = jnp.where(m < N_META, kt * w, 0.0).astype(BF16)
            state_ref[h] = jnp.dot(kd, vm_ref[:, h * RET_V_DIM:(h + 1) * RET_V_DIM],
                                   preferred_element_type=F32)

    row = lax.broadcasted_iota(jnp.int32, (T, 1), 0).astype(F32)
    col = lax.broadcasted_iota(jnp.int32, (1, T), 1).astype(F32)
    for h in range(RET_HEADS):
        lg = _log_gamma(h)
        q = q_ref[:, h * RET_QK_DIM:(h + 1) * RET_QK_DIM]
        kt = kt_ref[h * RET_QK_DIM:(h + 1) * RET_QK_DIM, :]
        v = v_ref[:, h * RET_V_DIM:(h + 1) * RET_V_DIM]
        scores = jnp.dot(q, kt, preferred_element_type=F32) * decay_ref[h]
        o = jnp.dot(scores.astype(BF16), v, preferred_element_type=F32)
        qd = (q.astype(F32) * jnp.exp(lg * (row + 1.0))).astype(BF16)
        state = state_ref[h]
        o = o + jnp.dot(qd, state.astype(BF16), preferred_element_type=F32)
        kd = (kt.astype(F32) * jnp.exp(lg * (T - 1.0 - col))).astype(BF16)
        state_ref[h] = jnp.exp(lg * float(T)) * state + jnp.dot(kd, v, preferred_element_type=F32)
        o = o * lax.rsqrt(jnp.mean(o * o, axis=1, keepdims=True) + RMS_EPS)
        gate = g_ref[:, h * RET_V_DIM:(h + 1) * RET_V_DIM].astype(F32)
        o_ref[:, h * RET_V_DIM:(h + 1) * RET_V_DIM] = (gate * jax.nn.sigmoid(gate) * o).astype(BF16)


def _retention(q, kt, v, g):
    n_t = SEQ // RET_T
    qk_w = RET_HEADS * RET_QK_DIM
    v_w = RET_HEADS * RET_V_DIM
    meta_blk = NF // META_BLK
    return pl.pallas_call(
        _retention_kernel,
        grid=(BATCH, n_t),
        in_specs=[pl.BlockSpec((RET_T, qk_w), lambda b, t: (b * n_t + t, 0)),
                  pl.BlockSpec((qk_w, RET_T), lambda b, t: (0, b * n_t + t)),
                  pl.BlockSpec((RET_T, v_w), lambda b, t: (b * n_t + t, 0)),
                  pl.BlockSpec((RET_T, v_w), lambda b, t: (b * n_t + t, 0)),
                  pl.BlockSpec((qk_w, META_BLK), lambda b, t: (0, meta_blk)),
                  pl.BlockSpec((META_BLK, v_w), lambda b, t: (meta_blk, 0))],
        out_specs=pl.BlockSpec((RET_T, v_w), lambda b, t: (b * n_t + t, 0)),
        out_shape=jax.ShapeDtypeStruct((NF, v_w), BF16),
        scratch_shapes=[pltpu.VMEM((RET_HEADS, RET_QK_DIM, RET_V_DIM), F32),
                        pltpu.VMEM((RET_HEADS, RET_T, RET_T), F32)],
        compiler_params=_cparams(("arbitrary", "arbitrary")),
        name="retention",
    )(q, kt, v, g, kt, v)


def _rows8(*rows):
    n = rows[0].shape[-1]
    out = jnp.zeros((8, n), F32)
    for i, r in enumerate(rows):
        out = out.at[i].set(r.astype(F32))
    return out


def _unit_coef(n):
    return _coef_rows(n, {})


TM_ALL = 2064
TM_LN_ALL = 768
TM_FRAMES = 1024


def kernel(x, meta_tokens, even_w_in, even_f_bias, diff_lambda, diff_subln_g, even_w_out,
           ret_w_in, ret_w_out, ln_g, ln_b, ffn_w1, ffn_w2):
    frames = x.reshape(NF, D_MODEL)
    meta = jnp.pad(meta_tokens.astype(F32), ((0, META_BLK - N_META), (0, 0)))

    w_even, c_even = _even_weights(even_w_in[0]), _even_consts()
    f_bias_row = _rows8(jnp.pad(even_f_bias[0], (0, LANES - FOX_HEADS)))
    proj = _even_proj(frames, w_even, c_even, f_bias_row, tm=512, row0=0, meta=False, cover_tail=True,
                      name="proj_even")
    q_all, k_all, va, vb = _even_proj(meta, w_even, c_even, f_bias_row, tm=META_BLK, row0=NF, meta=True,
                                      into=proj, name="proj_even_meta")

    attn, attn_meta = _attention(q_all, k_all, va, vb, diff_lambda[0].astype(F32), _rows8(diff_subln_g[0]))

    w_out, gb = even_w_out[0].astype(BF16), _rows8(ln_g[0, 0], ln_b[0, 0])
    h1s = _mm_res_ln(attn, w_out, frames, gb, rows=NF, tm=TM_FRAMES, out_rows=NT, cover_tail=True,
                     name="even_out_ln")
    h1, h1_16 = _mm_res_ln(attn_meta, w_out, meta, gb, rows=META_BLK, tm=META_BLK, out_row0=NF, out_rows=NT,
                           into=h1s, name="even_out_ln_meta")
    f1 = _mm_relu2(h1_16, ffn_w1[0].astype(BF16), rows=NT, tm=TM_ALL, tn=1024, name="ffn0_up")
    h2, h2_16 = _mm_res_ln(f1, ffn_w2[0].astype(BF16), h1, _rows8(ln_g[0, 1], ln_b[0, 1]),
                           rows=NT, tm=TM_LN_ALL, name="ffn0_down_ln")

    qk_w = RET_HEADS * RET_QK_DIM
    v_w = RET_HEADS * RET_V_DIM
    rw = ret_w_in[0]
    rq = _proj(h2_16, rw[:, :qk_w].astype(BF16), _unit_coef(qk_w), tm=TM_ALL, name="proj_ret_q")
    rkt = _proj_t((rw[:, qk_w:2 * qk_w] * RET_QK_DIM ** -0.5).T.astype(BF16), h2_16, tm=TM_LN_ALL)
    rv = _proj(h2_16, rw[:, 2 * qk_w:2 * qk_w + v_w].astype(BF16), _unit_coef(v_w), tm=TM_ALL, name="proj_ret_v")
    rg = _proj(h2_16, rw[:, 2 * qk_w + v_w:].astype(BF16), _unit_coef(v_w), tm=TM_ALL, name="proj_ret_g")
    y = _retention(rq, rkt, rv, rg)

    h3, h3_16 = _mm_res_ln(y, ret_w_out[0].astype(BF16), h2, _rows8(ln_g[1, 0], ln_b[1, 0]),
                           rows=NF, tm=TM_FRAMES, name="ret_out_ln")
    f2 = _mm_relu2(h3_16, ffn_w1[1].astype(BF16), rows=NF, tm=2048, tn=1024, name="ffn1_up")
    out, _ = _mm_res_ln(f2, ffn_w2[1].astype(BF16), h3, _rows8(ln_g[1, 1], ln_b[1, 1]),
                        rows=NF, tm=TM_FRAMES, name="ffn1_down_ln")
    return out.reshape(BATCH, SEQ, D_MODEL)
```

```python
import functools
import math

import jax
import jax.numpy as jnp
import numpy as np
from jax import lax
from jax.experimental import pallas as pl
from jax.experimental.pallas import tpu as pltpu

F32 = jnp.float32
BF16 = jnp.bfloat16

D_MODEL = 1024
BATCH = 4
SEQ = 8192
DEPTH = 2
CHUNK = 64
N_META = 16
DIFF_HEADS = 4
DIFF_HEAD_DIM = 64
DIFF_V_DIM = 128
FOX_HEADS = 8
FOX_HEAD_DIM = 64
RET_HEADS = 4
RET_QK_DIM = 256
RET_V_DIM = 512
D_FF = 4 * D_MODEL
DEEPNORM_ALPHA = (2 * DEPTH) ** 0.25
LN_EPS = 1e-5
RMS_EPS = 1e-6
LAM_INIT0 = 0.8 - 0.6 * math.exp(-0.3 * 0)
LOG2E = math.log2(math.e)

NF = BATCH * SEQ
META_BLK = 256
NT = NF + META_BLK
LANES = 128
NEG = -1e30

ATT_T = 512
META_KEYS = 128
RET_T = 256
VMEM_LIMIT = 56 * 1024 * 1024
ATT_VMEM_LIMIT = VMEM_LIMIT


def _cparams(sem, vmem_limit=VMEM_LIMIT):
    return pltpu.CompilerParams(dimension_semantics=sem, vmem_limit_bytes=vmem_limit)


def _bf16_pieces(x, n=3):
    out = []
    r = np.float32(x)
    for _ in range(n):
        p = np.float32(np.asarray(r, dtype=BF16).astype(np.float32))
        out.append(float(p))
        r = np.float32(r - p)
    return out


LOG2E_PIECES = _bf16_pieces(LOG2E)


def _row_positions(i, tm, row0=0):
    r = row0 + i * tm + lax.broadcasted_iota(jnp.int32, (tm, 1), 0)
    is_frame = r < NF
    m = r - NF
    pos = jnp.where(is_frame, (r & (SEQ - 1)) + N_META, m)
    valid = jnp.logical_or(is_frame, m < N_META)
    return pos, valid


def _proj_kernel(a_ref, w_ref, c_ref, o_ref, *, tm, heads, head_w, use_pos):
    y = jnp.dot(a_ref[...], w_ref[...], preferred_element_type=F32)
    y = y * c_ref[4:5, :] + c_ref[0:1, :]
    if use_pos:
        pos, valid = _row_positions(pl.program_id(0), tm)
        hi = (pos >> 7).astype(F32)
        lo = (pos & 127).astype(F32)
        y = y + hi * c_ref[1:2, :] + lo * c_ref[2:3, :] + jnp.where(valid, 0.0, 1.0) * c_ref[3:4, :]
    if heads is None:
        o_ref[...] = y.astype(o_ref.dtype)
    else:
        for h in range(heads):
            o_ref[h] = y[:, h * head_w:(h + 1) * head_w].astype(o_ref.dtype)


def _proj(a, w, coef, *, tm, name, heads=None, head_w=None, use_pos=False, out_dtype=BF16):
    m, k = a.shape
    n = w.shape[1]
    if heads is None:
        out_shape = jax.ShapeDtypeStruct((m, n), out_dtype)
        out_spec = pl.BlockSpec((tm, n), lambda i: (i, 0))
    else:
        out_shape = jax.ShapeDtypeStruct((heads, m, head_w), out_dtype)
        out_spec = pl.BlockSpec((heads, tm, head_w), lambda i: (0, i, 0))
    return pl.pallas_call(
        functools.partial(_proj_kernel, tm=tm, heads=heads, head_w=head_w, use_pos=use_pos),
        grid=(m // tm,),
        in_specs=[pl.BlockSpec((tm, k), lambda i: (i, 0)),
                  pl.BlockSpec((k, n), lambda i: (0, 0), pipeline_mode=pl.Buffered(1)),
                  pl.BlockSpec((8, n), lambda i: (0, 0))],
        out_specs=out_spec,
        out_shape=out_shape,
        compiler_params=_cparams(("arbitrary",)),
        name=name,
    )(a, w, coef)


def _proj_t_kernel(wt_ref, a_ref, o_ref):
    o_ref[...] = lax.dot_general(wt_ref[...], a_ref[...], (((1,), (1,)), ((), ())),
                                 preferred_element_type=F32).astype(o_ref.dtype)


def _proj_t(wt, a, *, tm):
    n, k = wt.shape
    m = a.shape[0]
    return pl.pallas_call(
        _proj_t_kernel,
        grid=(m // tm,),
        in_specs=[pl.BlockSpec((n, k), lambda i: (0, 0)),
                  pl.BlockSpec((tm, k), lambda i: (i, 0))],
        out_specs=pl.BlockSpec((n, tm), lambda i: (0, i)),
        out_shape=jax.ShapeDtypeStruct((n, m), BF16),
        compiler_params=_cparams(("arbitrary",)),
        name="proj_ret_kt",
    )(wt, a)


def _relu2_kernel(a_ref, w_ref, o_ref):
    y = jnp.dot(a_ref[...], w_ref[...], preferred_element_type=F32)
    y = jnp.maximum(y, 0.0)
    o_ref[...] = (y * y).astype(o_ref.dtype)


def _mm_relu2(a, w, *, rows, tm, tn, name):
    k = a.shape[1]
    n = w.shape[1]
    return pl.pallas_call(
        _relu2_kernel,
        grid=(rows // tm, n // tn),
        in_specs=[pl.BlockSpec((tm, k), lambda i, j: (i, 0)),
                  pl.BlockSpec((k, tn), lambda i, j: (0, j))],
        out_specs=pl.BlockSpec((tm, tn), lambda i, j: (i, j)),
        out_shape=jax.ShapeDtypeStruct((rows, n), BF16),
        compiler_params=_cparams(("arbitrary", "arbitrary")),
        name=name,
    )(a, w)


def _res_ln_kernel(a_ref, w_ref, r_ref, gb_ref, *refs):
    o32_ref, o16_ref = refs[-2:]
    y = jnp.dot(a_ref[...], w_ref[...], preferred_element_type=F32)
    z = DEEPNORM_ALPHA * r_ref[...] + y
    mu = jnp.mean(z, axis=-1, keepdims=True)
    zc = z - mu
    var = jnp.mean(zc * zc, axis=-1, keepdims=True)
    out = zc * lax.rsqrt(var + LN_EPS) * gb_ref[0:1, :] + gb_ref[1:2, :]
    o32_ref[...] = out
    o16_ref[...] = out.astype(BF16)


def _mm_res_ln(a, w, res, gb, *, rows, tm, name, out_row0=0, out_rows=None, into=None, cover_tail=False):
    k = a.shape[1]
    n = w.shape[1]
    blk0 = out_row0 // tm
    steps = rows // tm
    out_rows = rows if out_rows is None else out_rows
    n_alias = 0 if into is None else len(into)
    src = lambda i: (jnp.minimum(i, steps - 1), 0)
    return pl.pallas_call(
        _res_ln_kernel,
        grid=(steps + int(cover_tail),),
        in_specs=[pl.BlockSpec((tm, k), src),
                  pl.BlockSpec((k, n), lambda i: (0, 0), pipeline_mode=pl.Buffered(1)),
                  pl.BlockSpec((tm, n), src),
                  pl.BlockSpec((8, n), lambda i: (0, 0))] + [pl.BlockSpec(memory_space=pl.ANY)] * n_alias,
        out_specs=[pl.BlockSpec((tm, n), lambda i: (blk0 + i, 0)),
                   pl.BlockSpec((tm, n), lambda i: (blk0 + i, 0))],
        out_shape=[jax.ShapeDtypeStruct((out_rows, n), F32),
                   jax.ShapeDtypeStruct((out_rows, n), BF16)],
        input_output_aliases={4 + j: j for j in range(n_alias)},
        compiler_params=_cparams(("arbitrary",)),
        name=name,
    )(a, w, res, gb, *(into or ()))


N_HM = 2 * DIFF_HEADS
N_HT = N_HM + FOX_HEADS
VA_W = 2 * LANES
VB_W = LANES
BIAS_LANE0 = DIFF_HEAD_DIM
DIFF_BETA_PIECES = 2
FOX_BETA_PIECES = 3
ATT_OUT_W = DIFF_HEADS * DIFF_V_DIM + FOX_HEADS * FOX_HEAD_DIM
SLOPES = [2.0 ** (-8.0 * (h + 1) / DIFF_HEADS) for h in range(DIFF_HEADS)]


def _pad_heads(w, heads, width, pad_to):
    k = w.shape[0]
    w = w.reshape(k, heads, width)
    w = jnp.pad(w, ((0, 0), (0, 0), (0, pad_to - width)))
    return w.reshape(k, heads * pad_to)


def _coef_rows(n, rows):
    c = np.zeros((8, n), np.float32)
    c[4, :] = 1.0
    for r, vals in rows.items():
        for col, val in vals:
            c[r, col] = val
    return jnp.asarray(c)


EVEN_QK_W = N_HT * DIFF_HEAD_DIM
EVEN_VA_W = DIFF_HEADS * DIFF_V_DIM
EVEN_VB_W = FOX_HEADS * FOX_HEAD_DIM
EVEN_N = 2 * EVEN_QK_W + EVEN_VA_W + EVEN_VB_W + LANES
Q_SCALE = DIFF_HEAD_DIM ** -0.5 * LOG2E
ROW_QCONST, ROW_KHI, ROW_KLO, ROW_KPAD, ROW_VCONST = 0, N_HT, 2 * N_HT, 3 * N_HT, 4 * N_HT


def _even_weights(w_in):
    qk_w = DIFF_HEADS * 2 * DIFF_HEAD_DIM
    v_w = DIFF_HEADS * DIFF_V_DIM
    fox_w = FOX_HEADS * FOX_HEAD_DIM
    o = np.cumsum([0, qk_w, qk_w, v_w, fox_w, fox_w, fox_w, FOX_HEADS])
    sl = [w_in[:, o[i]:o[i + 1]] for i in range(7)]
    w_fb = jnp.pad(sl[6], ((0, 0), (0, LANES - FOX_HEADS)))
    return jnp.concatenate([sl[0], sl[3], sl[1], sl[4], sl[2], sl[5], w_fb], axis=1).astype(BF16)


def _even_consts():
    c = np.zeros((4 * N_HT + 8, LANES), np.float32)
    for ht in range(N_HT):
        pieces = DIFF_BETA_PIECES if ht < N_HM else FOX_BETA_PIECES
        for p in range(pieces):
            for r in range(3):
                c[ROW_QCONST + ht, BIAS_LANE0 + 3 * p + r] = LOG2E_PIECES[r]
        if ht < N_HM:
            slope = SLOPES[ht // 2]
            c[ROW_KHI + ht, BIAS_LANE0:BIAS_LANE0 + 3] = slope * 128.0
            c[ROW_KLO + ht, BIAS_LANE0 + 3:BIAS_LANE0 + 6] = slope
            c[ROW_KPAD + ht, BIAS_LANE0] = NEG
    c[ROW_VCONST, 0] = 1.0
    c[ROW_VCONST + 1, FOX_HEAD_DIM] = 1.0
    return jnp.asarray(c)


PIECE_LANES = 8


def _pack3(x):
    hi = x.astype(BF16).astype(F32)
    r1 = x - hi
    mid = r1.astype(BF16).astype(F32)
    lo = (r1 - mid).astype(BF16).astype(F32)
    return (hi + pltpu.roll(mid, PIECE_LANES, 1) + pltpu.roll(lo, 2 * PIECE_LANES, 1)).astype(BF16)


def _unpack3_sum(y):
    return y + pltpu.roll(y, LANES - PIECE_LANES, 1) + pltpu.roll(y, LANES - 2 * PIECE_LANES, 1)


def _forget_prefix(logits, tile, fbias_ref, tri_ref, carry_ref, *, tm, meta):
    @pl.when(tile % (SEQ // tm) == 0)
    def _():
        carry_ref[...] = jnp.zeros_like(carry_ref)

    heads = lax.broadcasted_iota(jnp.int32, (1, LANES), 1) < FOX_HEADS
    x = logits + fbias_ref[0:1, :]
    logf = jnp.minimum(x, 0.0) - jnp.log1p(jnp.exp(-jnp.abs(x)))
    part = jnp.dot(tri_ref[...], _pack3(jnp.where(heads, logf, 0.0)), preferred_element_type=F32)
    cum = carry_ref[0:1, :] + _unpack3_sum(part)
    carry_ref[0:1, :] = cum[tm - 1:tm, :]
    if meta:
        row = lax.broadcasted_iota(jnp.int32, (tm, 1), 0)
        g = jnp.where(row < N_META, cum[N_META - 1:N_META, :] - cum, NEG)
    else:
        g = -cum
    return _pack3(jnp.where(heads, g, 0.0))


Y_SPLIT = EVEN_QK_W + N_HM * DIFF_HEAD_DIM


def _even_proj_kernel(a_ref, w_ref, c_ref, fbias_ref, tri_ref, sel_ref, *refs, tm, row0, meta, last_tile):
    q_ref, ka_ref, kb_ref, va_ref, vb_ref, carry_ref, logits_ref, kb_real_ref = refs[-8:]
    i = pl.program_id(0)
    lane = lax.broadcasted_iota(jnp.int32, (1, LANES), 1)
    real = lane < DIFF_HEAD_DIM
    cur, prev = i % 2, (i + 1) % 2

    @pl.when(i == 0)
    def _():
        logits_ref[...] = jnp.zeros_like(logits_ref)
        kb_real_ref[...] = jnp.zeros_like(kb_real_ref)

    a = a_ref[...].astype(BF16)
    packed_g = _forget_prefix(logits_ref[prev], i - 1, fbias_ref, tri_ref, carry_ref, tm=tm, meta=meta)
    y1 = jnp.dot(a, w_ref[:, :Y_SPLIT], preferred_element_type=F32)
    fox_bias = jnp.dot(packed_g, sel_ref[...], preferred_element_type=F32)
    y2 = jnp.dot(a, w_ref[:, Y_SPLIT:EVEN_N - LANES], preferred_element_type=F32)
    logits_ref[cur] = jnp.dot(a, w_ref[:, EVEN_N - LANES:], preferred_element_type=F32)
    y = jnp.concatenate([y1, y2], axis=1)
    for h in range(FOX_HEADS):
        kb_ref[h] = jnp.where(real, kb_real_ref[prev, h], fox_bias[:, h * LANES:(h + 1) * LANES]).astype(BF16)
    pos, valid = _row_positions(jnp.minimum(i, last_tile), tm, row0)
    hi = (pos >> 7).astype(F32)
    lo = (pos & 127).astype(F32)
    pad = jnp.where(valid, 0.0, 1.0)

    def spread(pair, odd):
        return pltpu.roll(pair, DIFF_HEAD_DIM, 1) if odd else pair

    for j in range(N_HT // 2):
        qp = y[:, j * LANES:(j + 1) * LANES] * Q_SCALE
        kp = y[:, EVEN_QK_W + j * LANES:EVEN_QK_W + (j + 1) * LANES]
        for odd in range(2):
            ht = 2 * j + odd
            row = lambda r: c_ref[r + ht:r + ht + 1, :]
            q_ref[ht] = jnp.where(real, spread(qp, odd), row(ROW_QCONST)).astype(BF16)
            if ht < N_HM:
                k_bias = hi * row(ROW_KHI) + lo * row(ROW_KLO) + pad * row(ROW_KPAD)
                ka_ref[ht] = jnp.where(real, spread(kp, odd), k_bias).astype(BF16)
            else:
                kb_real_ref[cur, ht - N_HM] = spread(kp, odd)
    base = 2 * EVEN_QK_W
    ones_half = jnp.broadcast_to(c_ref[ROW_VCONST:ROW_VCONST + 1, :], (tm, LANES)).astype(BF16)
    for h in range(DIFF_HEADS):
        va_ref[h, :, :LANES] = y[:, base + h * LANES:base + (h + 1) * LANES].astype(BF16)
        va_ref[h, :, LANES:] = ones_half
    base += EVEN_VA_W
    for j in range(FOX_HEADS // 2):
        vp = y[:, base + j * LANES:base + (j + 1) * LANES]
        for odd in range(2):
            vb_ref[2 * j + odd] = jnp.where(real, spread(vp, odd),
                                            c_ref[ROW_VCONST + 1:ROW_VCONST + 2, :]).astype(BF16)


def _even_proj(a, w, consts, fbias_row, *, tm, row0, meta, into=None, cover_tail=False, name):
    m, k = a.shape
    blk0 = row0 // tm
    steps = m // tm
    n_alias = 0 if into is None else len(into)
    tri = jnp.tril(jnp.ones((tm, tm), F32)).astype(BF16)
    sel = np.zeros((LANES, FOX_HEADS * LANES), np.float32)
    for p in range(FOX_BETA_PIECES):
        for h in range(FOX_HEADS):
            for r in range(3):
                sel[PIECE_LANES * p + h, h * LANES + BIAS_LANE0 + 3 * p + r] = 1.0
    sel = jnp.asarray(sel, BF16)
    const2 = lambda arr: pl.BlockSpec(arr.shape, lambda i: (0, 0))
    n_in = 6
    last_tile = steps - 1 + int(cover_tail)
    now = lambda i: blk0 + jnp.minimum(i, last_tile)
    lagged = lambda i: blk0 + jnp.clip(i - 1, 0, last_tile)
    return pl.pallas_call(
        functools.partial(_even_proj_kernel, tm=tm, row0=row0, meta=meta, last_tile=last_tile),
        grid=(last_tile + 2,),
        in_specs=[pl.BlockSpec((tm, k), lambda i: (jnp.minimum(i, steps - 1), 0)),
                  pl.BlockSpec((k, EVEN_N), lambda i: (0, 0), pipeline_mode=pl.Buffered(1)),
                  const2(consts), const2(fbias_row), const2(tri), const2(sel)]
                 + [pl.BlockSpec(memory_space=pl.ANY)] * n_alias,
        out_specs=[pl.BlockSpec((N_HT, tm, LANES), lambda i: (0, now(i), 0)),
                   pl.BlockSpec((N_HM, tm, LANES), lambda i: (0, now(i), 0)),
                   pl.BlockSpec((FOX_HEADS, tm, LANES), lambda i: (0, lagged(i), 0)),
                   pl.BlockSpec((DIFF_HEADS, tm, VA_W), lambda i: (0, now(i), 0)),
                   pl.BlockSpec((FOX_HEADS, tm, VB_W), lambda i: (0, now(i), 0))],
        out_shape=[jax.ShapeDtypeStruct((N_HT, NT, LANES), BF16),
                   jax.ShapeDtypeStruct((N_HM, NT, LANES), BF16),
                   jax.ShapeDtypeStruct((FOX_HEADS, NT, LANES), BF16),
                   jax.ShapeDtypeStruct((DIFF_HEADS, NT, VA_W), BF16),
                   jax.ShapeDtypeStruct((FOX_HEADS, NT, VB_W), BF16)],
        scratch_shapes=[pltpu.VMEM((8, LANES), F32), pltpu.VMEM((2, tm, LANES), F32),
                        pltpu.VMEM((2, FOX_HEADS, tm, LANES), F32)],
        input_output_aliases={n_in + j: j for j in range(n_alias)},
        compiler_params=_cparams(("arbitrary",)),
        name=name,
    )(a, w, consts, fbias_row, tri, sel, *(into or ()))


def _fill_masks(ahead_ref, mask_ref, *, tq, tk, pos_off):
    i = lax.broadcasted_iota(jnp.int32, (tq, tk), 0)
    j = lax.broadcasted_iota(jnp.int32, (tq, tk), 1)
    ahead_ref[...] = jnp.maximum(j - i, 0).astype(F32)
    mask_ref[0] = jnp.where(((j + pos_off) >> 6) > ((i + pos_off) >> 6), NEG, 0.0)
    mask_ref[1] = jnp.where(j > i, NEG, 0.0)


def _softmax_update(s, m_prev):
    m_new = jnp.maximum(m_prev, jnp.max(s, axis=1, keepdims=True))
    alpha = jnp.exp2(m_prev - m_new)
    p = jnp.exp2(s - m_new).astype(BF16)
    return m_new, alpha, p


PIPE_U = 2
PER_KIND = PIPE_U // 2
N_GROUPS = N_HT // PIPE_U
NT_DIMS = (((1,), (1,)), ((), ()))


def _kind(u):
    return 0 if u < PER_KIND else 1


def _local(u, g):
    return PER_KIND * g + u % PER_KIND


def _head_tile(u, g):
    return _local(u, g) + N_HM * _kind(u)


def _diag_bias(u, g, coef_ref, ahead_ref, mask_ref):
    if _kind(u) == 0:
        return ahead_ref[...] * coef_ref[_local(u, g)] + mask_ref[0]
    return mask_ref[1]


def _acc_update(u, g, acc_a, acc_b, alpha, pv):
    i = _local(u, g)
    if _kind(u) == 0:
        acc_a[i] = alpha * acc_a[i] + pv
    else:
        acc_b[i] = alpha * acc_b[i] + pv


def _attn_finalize(lam_ref, g_ref, acc_a, acc_b, out_ref):
    lv = lam_ref[...]
    lam = (jnp.exp(jnp.sum(lv[0:1] * lv[1:2], axis=1, keepdims=True))
           - jnp.exp(jnp.sum(lv[2:3] * lv[3:4], axis=1, keepdims=True)) + LAM_INIT0)
    gain = g_ref[0:1, :] * (1.0 - LAM_INIT0)
    for h in range(DIFF_HEADS):
        a0 = acc_a[2 * h]
        a1 = acc_a[2 * h + 1]
        o = (a0[:, :DIFF_V_DIM] / a0[:, DIFF_V_DIM:DIFF_V_DIM + 1]
             - lam * (a1[:, :DIFF_V_DIM] / a1[:, DIFF_V_DIM:DIFF_V_DIM + 1]))
        ms = jnp.mean(o * o, axis=1, keepdims=True)
        out_ref[:, h * DIFF_V_DIM:(h + 1) * DIFF_V_DIM] = (o * lax.rsqrt(ms + RMS_EPS) * gain).astype(BF16)
    base = DIFF_HEADS * DIFF_V_DIM
    low = lax.broadcasted_iota(jnp.int32, (1, LANES), 1) < FOX_HEAD_DIM
    for j in range(FOX_HEADS // 2):
        even, odd = acc_b[2 * j], acc_b[2 * j + 1]
        even = even / even[:, FOX_HEAD_DIM:FOX_HEAD_DIM + 1]
        odd = odd / odd[:, FOX_HEAD_DIM:FOX_HEAD_DIM + 1]
        out_ref[:, base + j * LANES:base + (j + 1) * LANES] = jnp.where(
            low, even, pltpu.roll(odd, FOX_HEAD_DIM, 1)).astype(BF16)


STEPS_PER_BATCH = (SEQ // ATT_T) * (SEQ // ATT_T + 1) // 2
ATT_STEPS = BATCH * STEPS_PER_BATCH
DIAG_TK = ATT_T + META_KEYS


def _attn_kernel(qi_ref, kv_ref, coef_ref,
                 q_ref, qna_ref, qnb_ref, ka_ref, kb_ref, kna_ref, knb_ref, va_ref, vb_ref,
                 kma_ref, kmb_ref, vma_ref, vmb_ref,
                 lam_ref, g_ref,
                 out_ref, m_ref, acc_a, acc_b, ahead_ref, mask_ref, *pipe_refs):
    step = pl.program_id(0)
    nxt = jnp.minimum(step + 1, ATT_STEPS - 1)
    r, rn = step % STEPS_PER_BATCH, nxt % STEPS_PER_BATCH
    cur_diag = kv_ref[r] == qi_ref[r]
    nxt_diag = kv_ref[rn] == qi_ref[rn]
    cur_first = kv_ref[r] == 0
    nxt_first = kv_ref[rn] == 0
    s_bufs, p_bufs, a_bufs = _pipe_bufs(pipe_refs)
    T, MK = ATT_T, META_KEYS
    v_refs, vm_refs, km_refs = (va_ref, vb_ref), (vma_ref, vmb_ref), (kma_ref, kmb_ref)
    cur_rows = lambda ref: (lambda u, g: ref[_head_tile(u, g)])
    cur_keys = nxt_rows = lambda refs: (lambda u, g: refs[_kind(u)][_local(u, g)])

    def stage_qk(u, g, s_buf, q_of, k_of, diag):
        q = q_of(u, g)
        s = lax.dot_general(q, k_of(u, g), NT_DIMS, preferred_element_type=F32)
        if diag:
            s = s + _diag_bias(u, g, coef_ref, ahead_ref, mask_ref)
            s_buf[:, T:T + MK] = lax.dot_general(q, km_refs[_kind(u)][_local(u, g)], NT_DIMS,
                                                 preferred_element_type=F32)
        s_buf[:, :T] = s

    def stage_softmax(u, g, s_buf, p_buf, a_buf, diag, fresh=None):
        w = DIAG_TK if diag else T
        ht = _head_tile(u, g)
        m_prev = m_ref[ht]
        if fresh is not None:
            m_prev = jnp.where(fresh, NEG, m_prev)
        m_ref[ht], a_buf[...], p_buf[:, :w] = _softmax_update(s_buf[:, :w], m_prev)

    def stage_pv(u, g, p_buf, a_buf, diag):
        kind = _kind(u)
        vi = _local(u, g) >> 1 if kind == 0 else _local(u, g)
        pv = jnp.dot(p_buf[:, :T], v_refs[kind][vi], preferred_element_type=F32)
        if diag:
            pv = pv + jnp.dot(p_buf[:, T:T + MK], vm_refs[kind][vi], preferred_element_type=F32)
        _acc_update(u, g, acc_a, acc_b, a_buf[...], pv)

    def body(g, par, cd, nd):
        for u in range(PIPE_U):
            if isinstance(g, int) and g + 2 >= N_GROUPS:
                stage_qk(u, g + 2 - N_GROUPS, s_bufs[par][u],
                         nxt_rows((qna_ref, qnb_ref)), nxt_rows((kna_ref, knb_ref)), nd)
            else:
                stage_qk(u, g + 2, s_bufs[par][u], cur_rows(q_ref), cur_keys((ka_ref, kb_ref)), cd)
        for u in range(PIPE_U):
            stage_pv(u, g, p_bufs[par][u], a_bufs[par][u], cd)
        for u in range(PIPE_U):
            bufs = (s_bufs[1 - par][u], p_bufs[1 - par][u], a_bufs[1 - par][u])
            if isinstance(g, int) and g + 1 >= N_GROUPS:
                stage_softmax(u, g + 1 - N_GROUPS, *bufs, nd, fresh=nxt_first)
            else:
                stage_softmax(u, g + 1, *bufs, cd)

    @pl.when(cur_first)
    def _():
        acc_a[...] = jnp.zeros(acc_a.shape, F32)
        acc_b[...] = jnp.zeros(acc_b.shape, F32)
        fresh_m = jnp.full((N_HM - PER_KIND,) + m_ref.shape[1:], NEG, F32)
        m_ref[PER_KIND:N_HM] = fresh_m
        m_ref[N_HM + PER_KIND:] = fresh_m

    @pl.when(step == 0)
    def _():
        _fill_masks(ahead_ref, mask_ref, tq=T, tk=T, pos_off=0)
        for u in range(PIPE_U):
            m_ref[_head_tile(u, 0)] = jnp.full(m_ref.shape[1:], NEG, F32)
        for g in range(2):
            for u in range(PIPE_U):
                stage_qk(u, g, s_bufs[g][u], cur_rows(q_ref), cur_keys((ka_ref, kb_ref)), True)
        for u in range(PIPE_U):
            stage_softmax(u, 0, s_bufs[0][u], p_bufs[0][u], a_bufs[0][u], True)

    for cd in (False, True):
        for nd in (False, True):
            @pl.when(jnp.logical_and(cur_diag if cd else jnp.logical_not(cur_diag),
                                     nxt_diag if nd else jnp.logical_not(nxt_diag)))
            def _(cd=cd, nd=nd):
                def loop_body(j, c):
                    body(2 * j, 0, cd, nd)
                    body(2 * j + 1, 1, cd, nd)
                    return c

                lax.fori_loop(0, (N_GROUPS - 2) // 2, loop_body, 0)
                body(N_GROUPS - 2, 0, cd, nd)
                body(N_GROUPS - 1, 1, cd, nd)
                if cd:
                    _attn_finalize(lam_ref, g_ref, acc_a, acc_b, out_ref)


def _meta_attn_kernel(coef_ref, q_ref, ka_ref, kb_ref, va_ref, vb_ref, lam_ref, g_ref, out_ref,
                      m_ref, acc_a, acc_b, ahead_ref, mask_ref):
    _fill_masks(ahead_ref, mask_ref, tq=META_BLK, tk=META_BLK, pos_off=CHUNK - N_META)
    m_ref[...] = jnp.full(m_ref.shape, NEG, F32)
    acc_a[...] = jnp.zeros(acc_a.shape, F32)
    acc_b[...] = jnp.zeros(acc_b.shape, F32)

    def body(g, c):
        for u in range(PIPE_U):
            ht = _head_tile(u, g)
            k = ka_ref[_local(u, g)] if _kind(u) == 0 else kb_ref[_local(u, g)]
            s = lax.dot_general(q_ref[ht], k, NT_DIMS, preferred_element_type=F32)
            s = s + _diag_bias(u, g, coef_ref, ahead_ref, mask_ref)
            m_ref[ht], alpha, p = _softmax_update(s, m_ref[ht])
            v = va_ref[_local(u, g) >> 1] if _kind(u) == 0 else vb_ref[_local(u, g)]
            _acc_update(u, g, acc_a, acc_b, alpha, jnp.dot(p, v, preferred_element_type=F32))
        return c

    lax.fori_loop(0, N_GROUPS, body, 0)
    _attn_finalize(lam_ref, g_ref, acc_a, acc_b, out_ref)


N_PIPE_BUFS = 2 * PIPE_U


def _pipe_bufs(refs):
    n = N_PIPE_BUFS
    nest = lambda flat: [list(flat[:PIPE_U]), list(flat[PIPE_U:])]
    return nest(refs[:n]), nest(refs[n:2 * n]), nest(refs[2 * n:3 * n])


def _attn_scratch(tq, tk, buf_w=None):
    state = [pltpu.VMEM((N_HT, tq, 1), F32), pltpu.VMEM((N_HM, tq, VA_W), F32),
             pltpu.VMEM((FOX_HEADS, tq, VB_W), F32),
             pltpu.VMEM((tq, tk), F32), pltpu.VMEM((2, tq, tk), F32)]
    if buf_w is None:
        return state
    return (state + [pltpu.VMEM((tq, buf_w), F32)] * N_PIPE_BUFS
            + [pltpu.VMEM((tq, buf_w), BF16)] * N_PIPE_BUFS
            + [pltpu.VMEM((tq, 1), F32)] * N_PIPE_BUFS)


def _attention(q_all, ka, kb, va, vb, lam, gain_row):
    coef = jnp.asarray([-2.0 * SLOPES[hm // 2] * LOG2E for hm in range(N_HM)], F32)
    smem = pl.BlockSpec(memory_space=pltpu.SMEM)
    meta_blk = NF // META_BLK

    def meta_spec(heads, w):
        return pl.BlockSpec((heads, META_BLK, w), lambda *_: (0, meta_blk, 0))

    const2 = lambda shape: pl.BlockSpec(shape, lambda *_: (0, 0))

    out_meta = pl.pallas_call(
        _meta_attn_kernel,
        grid=(1,),
        in_specs=[smem, meta_spec(N_HT, LANES), meta_spec(N_HM, LANES), meta_spec(FOX_HEADS, LANES),
                  meta_spec(DIFF_HEADS, VA_W), meta_spec(FOX_HEADS, VB_W),
                  const2((4, DIFF_HEAD_DIM)), const2((8, LANES))],
        out_specs=pl.BlockSpec((META_BLK, ATT_OUT_W), lambda s: (0, 0)),
        out_shape=jax.ShapeDtypeStruct((META_BLK, ATT_OUT_W), BF16),
        scratch_shapes=_attn_scratch(META_BLK, META_BLK),
        compiler_params=_cparams(("arbitrary",)),
        name="attn_meta",
    )(coef, q_all, ka, kb, va, vb, lam, gain_row)

    n_q = SEQ // ATT_T
    qi_tab = np.concatenate([np.full(q + 1, q, np.int32) for q in range(n_q)])
    kv_tab = np.concatenate([np.arange(q + 1, dtype=np.int32) for q in range(n_q)])

    assert len(qi_tab) == STEPS_PER_BATCH

    def row_block(tab_of, ahead):
        def index(s, qi, kv):
            s = jnp.minimum(s + ahead, ATT_STEPS - 1)
            return (s // STEPS_PER_BATCH) * n_q + tab_of(qi, kv)[s % STEPS_PER_BATCH]
        return index

    q_blk, q_nxt = row_block(lambda qi, kv: qi, 0), row_block(lambda qi, kv: qi, 1)
    k_blk, k_nxt = row_block(lambda qi, kv: kv, 0), row_block(lambda qi, kv: kv, 1)

    def tile_spec(heads, w, blk, head_blk=0):
        return pl.BlockSpec((heads, ATT_T, w), lambda s, qi, kv: (head_blk, blk(s, qi, kv), 0))

    n_ahead = 2 * PER_KIND

    def meta_keys_spec(heads, w):
        return pl.BlockSpec((heads, META_KEYS, w), lambda *_: (0, NF // META_KEYS, 0))

    grid_spec = pltpu.PrefetchScalarGridSpec(
        num_scalar_prefetch=2,
        grid=(ATT_STEPS,),
        in_specs=[smem, tile_spec(N_HT, LANES, q_blk),
                  tile_spec(n_ahead, LANES, q_nxt), tile_spec(n_ahead, LANES, q_nxt, head_blk=N_HM // n_ahead),
                  tile_spec(N_HM, LANES, k_blk), tile_spec(FOX_HEADS, LANES, k_blk),
                  tile_spec(n_ahead, LANES, k_nxt), tile_spec(n_ahead, LANES, k_nxt),
                  tile_spec(DIFF_HEADS, VA_W, k_blk), tile_spec(FOX_HEADS, VB_W, k_blk),
                  meta_keys_spec(N_HM, LANES), meta_keys_spec(FOX_HEADS, LANES),
                  meta_keys_spec(DIFF_HEADS, VA_W), meta_keys_spec(FOX_HEADS, VB_W),
                  const2((4, DIFF_HEAD_DIM)), const2((8, LANES))],
        out_specs=pl.BlockSpec((ATT_T, ATT_OUT_W), lambda s, qi, kv: (q_blk(s, qi, kv), 0)),
        scratch_shapes=_attn_scratch(ATT_T, ATT_T, DIAG_TK),
    )
    out_frames = pl.pallas_call(
        _attn_kernel,
        grid_spec=grid_spec,
        out_shape=jax.ShapeDtypeStruct((NF, ATT_OUT_W), BF16),
        compiler_params=_cparams(("arbitrary",), ATT_VMEM_LIMIT),
        name="attn_frames",
    )(jnp.asarray(qi_tab), jnp.asarray(kv_tab), coef,
      q_all, q_all, q_all, ka, kb, ka, kb, va, vb, ka, kb, va, vb, lam, gain_row)
    return out_frames, out_meta


def _log_gamma(h):
    return jnp.log1p(jnp.full((1, 1), -(2.0 ** (-5.0 - h)), F32))


def _retention_kernel(q_ref, kt_ref, v_ref, g_ref, ktm_ref, vm_ref, o_ref, state_ref, decay_ref):
    b = pl.program_id(0)
    t = pl.program_id(1)
    T = RET_T

    @pl.when(jnp.logical_and(b == 0, t == 0))
    def _():
        i = lax.broadcasted_iota(jnp.int32, (T, T), 0)
        j = lax.broadcasted_iota(jnp.int32, (T, T), 1)
        dist = jnp.abs(i - j).astype(F32)
        vis = (j >> 6) <= (i >> 6)
        for h in range(RET_HEADS):
            decay_ref[h] = jnp.where(vis, jnp.exp(_log_gamma(h) * dist), 0.0)

    @pl.when(t == 0)
    def _():
        m = lax.broadcasted_iota(jnp.int32, (1, META_BLK), 1)
        for h in range(RET_HEADS):
            w = jnp.exp(_log_gamma(h) * (N_META - 1 - m).astype(F32))
            kt = ktm_ref[h * RET_QK_DIM:(h + 1) * RET_QK_DIM, :].astype(F32)
            kd = jnp.where(m < N_META, kt * w, 0.0).astype(BF16)
            state_ref[h] = jnp.dot(kd, vm_ref[:, h * RET_V_DIM:(h + 1) * RET_V_DIM],
                                   preferred_element_type=F32)

    row = lax.broadcasted_iota(jnp.int32, (T, 1), 0).astype(F32)
    col = lax.broadcasted_iota(jnp.int32, (1, T), 1).astype(F32)
    for h in range(RET_HEADS):
        lg = _log_gamma(h)
        q = q_ref[:, h * RET_QK_DIM:(h + 1) * RET_QK_DIM]
        kt = kt_ref[h * RET_QK_DIM:(h + 1) * RET_QK_DIM, :]
        v = v_ref[:, h * RET_V_DIM:(h + 1) * RET_V_DIM]
        scores = jnp.dot(q, kt, preferred_element_type=F32) * decay_ref[h]
        o = jnp.dot(scores.astype(BF16), v, preferred_element_type=F32)
        qd = (q.astype(F32) * jnp.exp(lg * (row + 1.0))).astype(BF16)
        state = state_ref[h]
        o = o + jnp.dot(qd, state.astype(BF16), preferred_element_type=F32)
        kd = (kt.astype(F32) * jnp.exp(lg * (T - 1.0 - col))).astype(BF16)
        state_ref[h] = jnp.exp(lg * float(T)) * state + jnp.dot(kd, v, preferred_element_type=F32)
        o = o * lax.rsqrt(jnp.mean(o * o, axis=1, keepdims=True) + RMS_EPS)
        gate = g_ref[:, h * RET_V_DIM:(h + 1) * RET_V_DIM].astype(F32)
        o_ref[:, h * RET_V_DIM:(h + 1) * RET_V_DIM] = (gate * jax.nn.sigmoid(gate) * o).astype(BF16)


def _retention(q, kt, v, g):
    n_t = SEQ // RET_T
    qk_w = RET_HEADS * RET_QK_DIM
    v_w = RET_HEADS * RET_V_DIM
    meta_blk = NF // META_BLK
    return pl.pallas_call(
        _retention_kernel,
        grid=(BATCH, n_t),
        in_specs=[pl.BlockSpec((RET_T, qk_w), lambda b, t: (b * n_t + t, 0)),
                  pl.BlockSpec((qk_w, RET_T), lambda b, t: (0, b * n_t + t)),
                  pl.BlockSpec((RET_T, v_w), lambda b, t: (b * n_t + t, 0)),
                  pl.BlockSpec((RET_T, v_w), lambda b, t: (b * n_t + t, 0)),
                  pl.BlockSpec((qk_w, META_BLK), lambda b, t: (0, meta_blk)),
                  pl.BlockSpec((META_BLK, v_w), lambda b, t: (meta_blk, 0))],
        out_specs=pl.BlockSpec((RET_T, v_w), lambda b, t: (b * n_t + t, 0)),
        out_shape=jax.ShapeDtypeStruct((NF, v_w), BF16),
        scratch_shapes=[pltpu.VMEM((RET_HEADS, RET_QK_DIM, RET_V_DIM), F32),
                        pltpu.VMEM((RET_HEADS, RET_T, RET_T), F32)],
        compiler_params=_cparams(("arbitrary", "arbitrary")),
        name="retention",
    )(q, kt, v, g, kt, v)


def _rows8(*rows):
    n = rows[0].shape[-1]
    out = jnp.zeros((8, n), F32)
    for i, r in enumerate(rows):
        out = out.at[i].set(r.astype(F32))
    return out


def _unit_coef(n):
    return _coef_rows(n, {})


TM_ALL = 2064
TM_LN_ALL = 768
TM_FRAMES = 1024


def kernel(x, meta_tokens, even_w_in, even_f_bias, diff_lambda, diff_subln_g, even_w_out,
           ret_w_in, ret_w_out, ln_g, ln_b, ffn_w1, ffn_w2):
    frames = x.reshape(NF, D_MODEL)
    meta = jnp.pad(meta_tokens.astype(F32), ((0, META_BLK - N_META), (0, 0)))

    w_even, c_even = _even_weights(even_w_in[0]), _even_consts()
    f_bias_row = _rows8(jnp.pad(even_f_bias[0], (0, LANES - FOX_HEADS)))
    proj = _even_proj(frames, w_even, c_even, f_bias_row, tm=512, row0=0, meta=False, cover_tail=True,
                      name="proj_even")
    q_all, ka, kb, va, vb = _even_proj(meta, w_even, c_even, f_bias_row, tm=META_BLK, row0=NF, meta=True,
                                       into=proj, name="proj_even_meta")

    attn, attn_meta = _attention(q_all, ka, kb, va, vb, diff_lambda[0].astype(F32), _rows8(diff_subln_g[0]))

    w_out, gb = even_w_out[0].astype(BF16), _rows8(ln_g[0, 0], ln_b[0, 0])
    h1s = _mm_res_ln(attn, w_out, frames, gb, rows=NF, tm=TM_FRAMES, out_rows=NT, cover_tail=True,
                     name="even_out_ln")
    h1, h1_16 = _mm_res_ln(attn_meta, w_out, meta, gb, rows=META_BLK, tm=META_BLK, out_row0=NF, out_rows=NT,
                           into=h1s, name="even_out_ln_meta")
    f1 = _mm_relu2(h1_16, ffn_w1[0].astype(BF16), rows=NT, tm=TM_ALL, tn=1024, name="ffn0_up")
    h2, h2_16 = _mm_res_ln(f1, ffn_w2[0].astype(BF16), h1, _rows8(ln_g[0, 1], ln_b[0, 1]),
                           rows=NT, tm=TM_LN_ALL, name="ffn0_down_ln")

    qk_w = RET_HEADS * RET_QK_DIM
    v_w = RET_HEADS * RET_V_DIM
    rw = ret_w_in[0]
    rq = _proj(h2_16, rw[:, :qk_w].astype(BF16), _unit_coef(qk_w), tm=TM_ALL, name="proj_ret_q")
    rkt = _proj_t((rw[:, qk_w:2 * qk_w] * RET_QK_DIM ** -0.5).T.astype(BF16), h2_16, tm=TM_LN_ALL)
    rv = _proj(h2_16, rw[:, 2 * qk_w:2 * qk_w + v_w].astype(BF16), _unit_coef(v_w), tm=TM_ALL, name="proj_ret_v")
    rg = _proj(h2_16, rw[:, 2 * qk_w + v_w:].astype(BF16), _unit_coef(v_w), tm=TM_ALL, name="proj_ret_g")
    y = _retention(rq, rkt, rv, rg)

    h3, h3_16 = _mm_res_ln(y, ret_w_out[0].astype(BF16), h2, _rows8(ln_g[1, 0], ln_b[1, 0]),
                           rows=NF, tm=TM_FRAMES, name="ret_out_ln")
    f2 = _mm_relu2(h3_16, ffn_w1[1].astype(BF16), rows=NF, tm=2048, tn=1024, name="ffn1_up")
    out, _ = _mm_res_ln(f2, ffn_w2[1].astype(BF16), h3, _rows8(ln_g[1, 1], ln_b[1, 1]),
                        rows=NF, tm=TM_FRAMES, name="ffn1_down_ln")
    return out.reshape(BATCH, SEQ, D_MODEL)
```

```python
import functools
import math

import jax
import jax.numpy as jnp
import numpy as np
from jax import lax
from jax.experimental import pallas as pl
from jax.experimental.pallas import tpu as pltpu

F32 = jnp.float32
BF16 = jnp.bfloat16

D_MODEL = 1024
BATCH = 4
SEQ = 8192
DEPTH = 2
CHUNK = 64
N_META = 16
DIFF_HEADS = 4
DIFF_HEAD_DIM = 64
DIFF_V_DIM = 128
FOX_HEADS = 8
FOX_HEAD_DIM = 64
RET_HEADS = 4
RET_QK_DIM = 256
RET_V_DIM = 512
D_FF = 4 * D_MODEL
DEEPNORM_ALPHA = (2 * DEPTH) ** 0.25
LN_EPS = 1e-5
RMS_EPS = 1e-6
LAM_INIT0 = 0.8 - 0.6 * math.exp(-0.3 * 0)
LOG2E = math.log2(math.e)

NF = BATCH * SEQ
META_BLK = 256
NT = NF + META_BLK
LANES = 128
NEG = -1e30

ATT_T = 512
META_KEYS = 128
RET_T = 256
VMEM_LIMIT = 56 * 1024 * 1024


def _cparams(sem):
    return pltpu.CompilerParams(dimension_semantics=sem, vmem_limit_bytes=VMEM_LIMIT)


def _bf16_pieces(x, n=3):
    out = []
    r = np.float32(x)
    for _ in range(n):
        p = np.float32(np.asarray(r, dtype=BF16).astype(np.float32))
        out.append(float(p))
        r = np.float32(r - p)
    return out


LOG2E_PIECES = _bf16_pieces(LOG2E)


def _row_positions(i, tm, row0=0):
    r = row0 + i * tm + lax.broadcasted_iota(jnp.int32, (tm, 1), 0)
    is_frame = r < NF
    m = r - NF
    pos = jnp.where(is_frame, (r & (SEQ - 1)) + N_META, m)
    valid = jnp.logical_or(is_frame, m < N_META)
    return pos, valid


def _proj_kernel(a_ref, w_ref, c_ref, o_ref, *, tm, heads, head_w, use_pos):
    y = jnp.dot(a_ref[...], w_ref[...], preferred_element_type=F32)
    y = y * c_ref[4:5, :] + c_ref[0:1, :]
    if use_pos:
        pos, valid = _row_positions(pl.program_id(0), tm)
        hi = (pos >> 7).astype(F32)
        lo = (pos & 127).astype(F32)
        y = y + hi * c_ref[1:2, :] + lo * c_ref[2:3, :] + jnp.where(valid, 0.0, 1.0) * c_ref[3:4, :]
    if heads is None:
        o_ref[...] = y.astype(o_ref.dtype)
    else:
        for h in range(heads):
            o_ref[h] = y[:, h * head_w:(h + 1) * head_w].astype(o_ref.dtype)


def _proj(a, w, coef, *, tm, name, heads=None, head_w=None, use_pos=False, out_dtype=BF16):
    m, k = a.shape
    n = w.shape[1]
    if heads is None:
        out_shape = jax.ShapeDtypeStruct((m, n), out_dtype)
        out_spec = pl.BlockSpec((tm, n), lambda i: (i, 0))
    else:
        out_shape = jax.ShapeDtypeStruct((heads, m, head_w), out_dtype)
        out_spec = pl.BlockSpec((heads, tm, head_w), lambda i: (0, i, 0))
    return pl.pallas_call(
        functools.partial(_proj_kernel, tm=tm, heads=heads, head_w=head_w, use_pos=use_pos),
        grid=(m // tm,),
        in_specs=[pl.BlockSpec((tm, k), lambda i: (i, 0)),
                  pl.BlockSpec((k, n), lambda i: (0, 0), pipeline_mode=pl.Buffered(1)),
                  pl.BlockSpec((8, n), lambda i: (0, 0))],
        out_specs=out_spec,
        out_shape=out_shape,
        compiler_params=_cparams(("arbitrary",)),
        name=name,
    )(a, w, coef)


def _proj_t_kernel(wt_ref, a_ref, o_ref):
    o_ref[...] = lax.dot_general(wt_ref[...], a_ref[...], (((1,), (1,)), ((), ())),
                                 preferred_element_type=F32).astype(o_ref.dtype)


def _proj_t(wt, a, *, tm):
    n, k = wt.shape
    m = a.shape[0]
    return pl.pallas_call(
        _proj_t_kernel,
        grid=(m // tm,),
        in_specs=[pl.BlockSpec((n, k), lambda i: (0, 0)),
                  pl.BlockSpec((tm, k), lambda i: (i, 0))],
        out_specs=pl.BlockSpec((n, tm), lambda i: (0, i)),
        out_shape=jax.ShapeDtypeStruct((n, m), BF16),
        compiler_params=_cparams(("arbitrary",)),
        name="proj_ret_kt",
    )(wt, a)


def _relu2_kernel(a_ref, w_ref, o_ref):
    y = jnp.dot(a_ref[...], w_ref[...], preferred_element_type=F32)
    y = jnp.maximum(y, 0.0)
    o_ref[...] = (y * y).astype(o_ref.dtype)


def _mm_relu2(a, w, *, rows, tm, tn, name):
    k = a.shape[1]
    n = w.shape[1]
    return pl.pallas_call(
        _relu2_kernel,
        grid=(rows // tm, n // tn),
        in_specs=[pl.BlockSpec((tm, k), lambda i, j: (i, 0)),
                  pl.BlockSpec((k, tn), lambda i, j: (0, j))],
        out_specs=pl.BlockSpec((tm, tn), lambda i, j: (i, j)),
        out_shape=jax.ShapeDtypeStruct((rows, n), BF16),
        compiler_params=_cparams(("arbitrary", "arbitrary")),
        name=name,
    )(a, w)


def _res_ln_kernel(a_ref, w_ref, r_ref, gb_ref, *refs):
    o32_ref, o16_ref = refs[-2:]
    y = jnp.dot(a_ref[...], w_ref[...], preferred_element_type=F32)
    z = DEEPNORM_ALPHA * r_ref[...] + y
    mu = jnp.mean(z, axis=-1, keepdims=True)
    zc = z - mu
    var = jnp.mean(zc * zc, axis=-1, keepdims=True)
    out = zc * lax.rsqrt(var + LN_EPS) * gb_ref[0:1, :] + gb_ref[1:2, :]
    o32_ref[...] = out
    o16_ref[...] = out.astype(BF16)


def _mm_res_ln(a, w, res, gb, *, rows, tm, name, out_row0=0, out_rows=None, into=None, cover_tail=False):
    k = a.shape[1]
    n = w.shape[1]
    blk0 = out_row0 // tm
    steps = rows // tm
    out_rows = rows if out_rows is None else out_rows
    n_alias = 0 if into is None else len(into)
    src = lambda i: (jnp.minimum(i, steps - 1), 0)
    return pl.pallas_call(
        _res_ln_kernel,
        grid=(steps + int(cover_tail),),
        in_specs=[pl.BlockSpec((tm, k), src),
                  pl.BlockSpec((k, n), lambda i: (0, 0), pipeline_mode=pl.Buffered(1)),
                  pl.BlockSpec((tm, n), src),
                  pl.BlockSpec((8, n), lambda i: (0, 0))] + [pl.BlockSpec(memory_space=pl.ANY)] * n_alias,
        out_specs=[pl.BlockSpec((tm, n), lambda i: (blk0 + i, 0)),
                   pl.BlockSpec((tm, n), lambda i: (blk0 + i, 0))],
        out_shape=[jax.ShapeDtypeStruct((out_rows, n), F32),
                   jax.ShapeDtypeStruct((out_rows, n), BF16)],
        input_output_aliases={4 + j: j for j in range(n_alias)},
        compiler_params=_cparams(("arbitrary",)),
        name=name,
    )(a, w, res, gb, *(into or ()))


N_HM = 2 * DIFF_HEADS
N_HT = N_HM + FOX_HEADS
VA_W = 2 * LANES
VB_W = LANES
BIAS_LANE0 = DIFF_HEAD_DIM
DIFF_BETA_PIECES = 2
FOX_BETA_PIECES = 3
ATT_OUT_W = DIFF_HEADS * DIFF_V_DIM + FOX_HEADS * FOX_HEAD_DIM
SLOPES = [2.0 ** (-8.0 * (h + 1) / DIFF_HEADS) for h in range(DIFF_HEADS)]


def _coef_rows(n, rows):
    c = np.zeros((8, n), np.float32)
    c[4, :] = 1.0
    for r, vals in rows.items():
        for col, val in vals:
            c[r, col] = val
    return jnp.asarray(c)


EVEN_QK_W = N_HT * DIFF_HEAD_DIM
EVEN_VA_W = DIFF_HEADS * DIFF_V_DIM
EVEN_VB_W = FOX_HEADS * FOX_HEAD_DIM
EVEN_N = 2 * EVEN_QK_W + EVEN_VA_W + EVEN_VB_W + LANES
Q_SCALE = DIFF_HEAD_DIM ** -0.5 * LOG2E
ROW_QCONST, ROW_KHI, ROW_KLO, ROW_KPAD, ROW_VCONST = 0, N_HT, 2 * N_HT, 3 * N_HT, 4 * N_HT


def _even_weights(w_in):
    qk_w = DIFF_HEADS * 2 * DIFF_HEAD_DIM
    v_w = DIFF_HEADS * DIFF_V_DIM
    fox_w = FOX_HEADS * FOX_HEAD_DIM
    o = np.cumsum([0, qk_w, qk_w, v_w, fox_w, fox_w, fox_w, FOX_HEADS])
    sl = [w_in[:, o[i]:o[i + 1]] for i in range(7)]
    w_fb = jnp.pad(sl[6], ((0, 0), (0, LANES - FOX_HEADS)))
    return jnp.concatenate([sl[0], sl[3], sl[1], sl[4], sl[2], sl[5], w_fb], axis=1).astype(BF16)


def _even_consts():
    c = np.zeros((4 * N_HT + 8, LANES), np.float32)
    for ht in range(N_HT):
        pieces = DIFF_BETA_PIECES if ht < N_HM else FOX_BETA_PIECES
        for p in range(pieces):
            for r in range(3):
                c[ROW_QCONST + ht, BIAS_LANE0 + 3 * p + r] = LOG2E_PIECES[r]
        if ht < N_HM:
            slope = SLOPES[ht // 2]
            c[ROW_KHI + ht, BIAS_LANE0:BIAS_LANE0 + 3] = slope * 128.0
            c[ROW_KLO + ht, BIAS_LANE0 + 3:BIAS_LANE0 + 6] = slope
            c[ROW_KPAD + ht, BIAS_LANE0] = NEG
    c[ROW_VCONST, 0] = 1.0
    c[ROW_VCONST + 1, FOX_HEAD_DIM] = 1.0
    return jnp.asarray(c)


PIECE_LANES = 8


def _pack3(x):
    hi = x.astype(BF16).astype(F32)
    r1 = x - hi
    mid = r1.astype(BF16).astype(F32)
    lo = (r1 - mid).astype(BF16).astype(F32)
    return (hi + pltpu.roll(mid, PIECE_LANES, 1) + pltpu.roll(lo, 2 * PIECE_LANES, 1)).astype(BF16)


def _unpack3_sum(y):
    return y + pltpu.roll(y, LANES - PIECE_LANES, 1) + pltpu.roll(y, LANES - 2 * PIECE_LANES, 1)


def _forget_bias_lanes(logits, fbias_ref, tri_ref, sel_ref, carry_ref, *, tm, meta):
    @pl.when(pl.program_id(0) % (SEQ // tm) == 0)
    def _():
        carry_ref[...] = jnp.zeros_like(carry_ref)

    heads = lax.broadcasted_iota(jnp.int32, (1, LANES), 1) < FOX_HEADS
    x = logits + fbias_ref[0:1, :]
    logf = jnp.minimum(x, 0.0) - jnp.log1p(jnp.exp(-jnp.abs(x)))
    part = jnp.dot(tri_ref[...], _pack3(jnp.where(heads, logf, 0.0)), preferred_element_type=F32)
    cum = carry_ref[0:1, :] + _unpack3_sum(part)
    carry_ref[0:1, :] = cum[tm - 1:tm, :]
    if meta:
        row = lax.broadcasted_iota(jnp.int32, (tm, 1), 0)
        g = jnp.where(row < N_META, cum[N_META - 1:N_META, :] - cum, NEG)
    else:
        g = -cum
    return jnp.dot(_pack3(jnp.where(heads, g, 0.0)), sel_ref[...], preferred_element_type=F32)


def _even_proj_kernel(a_ref, w_ref, c_ref, fbias_ref, tri_ref, sel_ref, *refs, tm, row0, meta):
    q_ref, k_ref, va_ref, vb_ref, carry_ref = refs[-5:]
    a = a_ref[...].astype(BF16)
    logits = jnp.dot(a, w_ref[:, EVEN_N - LANES:], preferred_element_type=F32)
    fox_bias = _forget_bias_lanes(logits, fbias_ref, tri_ref, sel_ref, carry_ref, tm=tm, meta=meta)
    y = jnp.dot(a, w_ref[:, :EVEN_N - LANES], preferred_element_type=F32)
    lane = lax.broadcasted_iota(jnp.int32, (1, LANES), 1)
    real = lane < DIFF_HEAD_DIM
    pos, valid = _row_positions(pl.program_id(0), tm, row0)
    hi = (pos >> 7).astype(F32)
    lo = (pos & 127).astype(F32)
    pad = jnp.where(valid, 0.0, 1.0)

    def spread(pair, odd):
        return pltpu.roll(pair, DIFF_HEAD_DIM, 1) if odd else pair

    for j in range(N_HT // 2):
        qp = y[:, j * LANES:(j + 1) * LANES] * Q_SCALE
        kp = y[:, EVEN_QK_W + j * LANES:EVEN_QK_W + (j + 1) * LANES]
        for odd in range(2):
            ht = 2 * j + odd
            row = lambda r: c_ref[r + ht:r + ht + 1, :]
            q_ref[ht] = jnp.where(real, spread(qp, odd), row(ROW_QCONST)).astype(BF16)
            if ht < N_HM:
                k_bias = hi * row(ROW_KHI) + lo * row(ROW_KLO) + pad * row(ROW_KPAD)
            else:
                k_bias = fox_bias[:, (ht - N_HM) * LANES:(ht - N_HM + 1) * LANES]
            k_ref[ht] = jnp.where(real, spread(kp, odd), k_bias).astype(BF16)
    base = 2 * EVEN_QK_W
    ones_half = jnp.broadcast_to(c_ref[ROW_VCONST:ROW_VCONST + 1, :], (tm, LANES)).astype(BF16)
    for h in range(DIFF_HEADS):
        va_ref[h, :, :LANES] = y[:, base + h * LANES:base + (h + 1) * LANES].astype(BF16)
        va_ref[h, :, LANES:] = ones_half
    base += EVEN_VA_W
    for j in range(FOX_HEADS // 2):
        vp = y[:, base + j * LANES:base + (j + 1) * LANES]
        for odd in range(2):
            vb_ref[2 * j + odd] = jnp.where(real, spread(vp, odd),
                                            c_ref[ROW_VCONST + 1:ROW_VCONST + 2, :]).astype(BF16)


def _even_proj(a, w, consts, fbias_row, *, tm, row0, meta, into=None, cover_tail=False, name):
    m, k = a.shape
    blk0 = row0 // tm
    steps = m // tm
    n_alias = 0 if into is None else len(into)
    tri = jnp.tril(jnp.ones((tm, tm), F32)).astype(BF16)
    sel = np.zeros((LANES, FOX_HEADS * LANES), np.float32)
    for p in range(FOX_BETA_PIECES):
        for h in range(FOX_HEADS):
            for r in range(3):
                sel[PIECE_LANES * p + h, h * LANES + BIAS_LANE0 + 3 * p + r] = 1.0
    sel = jnp.asarray(sel, BF16)
    const2 = lambda arr: pl.BlockSpec(arr.shape, lambda i: (0, 0))
    n_in = 6
    return pl.pallas_call(
        functools.partial(_even_proj_kernel, tm=tm, row0=row0, meta=meta),
        grid=(steps + int(cover_tail),),
        in_specs=[pl.BlockSpec((tm, k), lambda i: (jnp.minimum(i, steps - 1), 0)),
                  pl.BlockSpec((k, EVEN_N), lambda i: (0, 0), pipeline_mode=pl.Buffered(1)),
                  const2(consts), const2(fbias_row), const2(tri), const2(sel)]
                 + [pl.BlockSpec(memory_space=pl.ANY)] * n_alias,
        out_specs=[pl.BlockSpec((N_HT, tm, LANES), lambda i: (0, blk0 + i, 0)),
                   pl.BlockSpec((N_HT, tm, LANES), lambda i: (0, blk0 + i, 0)),
                   pl.BlockSpec((DIFF_HEADS, tm, VA_W), lambda i: (0, blk0 + i, 0)),
                   pl.BlockSpec((FOX_HEADS, tm, VB_W), lambda i: (0, blk0 + i, 0))],
        out_shape=[jax.ShapeDtypeStruct((N_HT, NT, LANES), BF16),
                   jax.ShapeDtypeStruct((N_HT, NT, LANES), BF16),
                   jax.ShapeDtypeStruct((DIFF_HEADS, NT, VA_W), BF16),
                   jax.ShapeDtypeStruct((FOX_HEADS, NT, VB_W), BF16)],
        scratch_shapes=[pltpu.VMEM((8, LANES), F32)],
        input_output_aliases={n_in + j: j for j in range(n_alias)},
        compiler_params=_cparams(("arbitrary",)),
        name=name,
    )(a, w, consts, fbias_row, tri, sel, *(into or ()))


def _fill_diag_bias(bias_ref, coef_ref, *, tq, tk, pos_off):
    i = lax.broadcasted_iota(jnp.int32, (tq, tk), 0)
    j = lax.broadcasted_iota(jnp.int32, (tq, tk), 1)
    ahead = jnp.maximum(j - i, 0).astype(F32)
    chunk_mask = jnp.where(((j + pos_off) >> 6) > ((i + pos_off) >> 6), NEG, 0.0)
    for h in range(DIFF_HEADS):
        bias_ref[h] = ahead * coef_ref[2 * h] + chunk_mask
    bias_ref[DIFF_HEADS] = jnp.where(j > i, NEG, 0.0)


def _softmax_update(s, m_prev):
    m_new = jnp.maximum(m_prev, jnp.max(s, axis=1, keepdims=True))
    alpha = jnp.exp2(m_prev - m_new)
    p = jnp.exp2(s - m_new).astype(BF16)
    return m_new, alpha, p


PIPE_U = 2
N_GROUPS = N_HT // PIPE_U
NT_DIMS = (((1,), (1,)), ((), ()))


def _head_tile(u, g):
    return g + N_HM * u


def _diag_bias(u, g, bias_ref):
    return bias_ref[g >> 1 if u == 0 else DIFF_HEADS]


def _acc_update(u, g, acc_a, acc_b, alpha, pv):
    if u == 0:
        acc_a[g] = alpha * acc_a[g] + pv
    else:
        acc_b[g] = alpha * acc_b[g] + pv


def _attn_finalize(lam_ref, g_ref, acc_a, acc_b, out_ref):
    lv = lam_ref[...]
    lam = (jnp.exp(jnp.sum(lv[0:1] * lv[1:2], axis=1, keepdims=True))
           - jnp.exp(jnp.sum(lv[2:3] * lv[3:4], axis=1, keepdims=True)) + LAM_INIT0)
    gain = g_ref[0:1, :] * (1.0 - LAM_INIT0)
    for h in range(DIFF_HEADS):
        a0 = acc_a[2 * h]
        a1 = acc_a[2 * h + 1]
        o = (a0[:, :DIFF_V_DIM] * (1.0 / a0[:, DIFF_V_DIM:DIFF_V_DIM + 1])
             - a1[:, :DIFF_V_DIM] * (lam / a1[:, DIFF_V_DIM:DIFF_V_DIM + 1]))
        ms = jnp.mean(o * o, axis=1, keepdims=True)
        out_ref[:, h * DIFF_V_DIM:(h + 1) * DIFF_V_DIM] = (o * lax.rsqrt(ms + RMS_EPS) * gain).astype(BF16)
    base = DIFF_HEADS * DIFF_V_DIM
    low = lax.broadcasted_iota(jnp.int32, (1, LANES), 1) < FOX_HEAD_DIM
    for j in range(FOX_HEADS // 2):
        even, odd = acc_b[2 * j], acc_b[2 * j + 1]
        even = even * (1.0 / even[:, FOX_HEAD_DIM:FOX_HEAD_DIM + 1])
        odd = odd * (1.0 / odd[:, FOX_HEAD_DIM:FOX_HEAD_DIM + 1])
        out_ref[:, base + j * LANES:base + (j + 1) * LANES] = jnp.where(
            low, even, pltpu.roll(odd, FOX_HEAD_DIM, 1)).astype(BF16)


STEPS_PER_BATCH = (SEQ // ATT_T) * (SEQ // ATT_T + 1) // 2
ATT_STEPS = BATCH * STEPS_PER_BATCH
DIAG_TK = ATT_T + META_KEYS


def _attn_kernel(qi_ref, kv_ref, coef_ref,
                 q_ref, qna_ref, qnb_ref, k_ref, kna_ref, knb_ref, va_ref, vb_ref, km_ref, vma_ref, vmb_ref,
                 lam_ref, g_ref,
                 out_ref, m_ref, acc_a, acc_b, bias_ref, *pipe_refs):
    step = pl.program_id(0)
    nxt = jnp.minimum(step + 1, ATT_STEPS - 1)
    r, rn = step % STEPS_PER_BATCH, nxt % STEPS_PER_BATCH
    cur_diag = kv_ref[r] == qi_ref[r]
    nxt_diag = kv_ref[rn] == qi_ref[rn]
    cur_first = kv_ref[r] == 0
    nxt_first = kv_ref[rn] == 0
    s_bufs, p_bufs, a_bufs = _pipe_bufs(pipe_refs)
    T, MK = ATT_T, META_KEYS
    v_refs, vm_refs = (va_ref, vb_ref), (vma_ref, vmb_ref)
    cur_rows = lambda ref: (lambda u, g: ref[_head_tile(u, g)])
    nxt_rows = lambda refs: (lambda u, g: refs[u][g])

    def stage_qk(u, g, s_buf, q_of, k_of, diag):
        q = q_of(u, g)
        s = lax.dot_general(q, k_of(u, g), NT_DIMS, preferred_element_type=F32)
        if diag:
            s = s + _diag_bias(u, g, bias_ref)
            s_buf[:, T:T + MK] = lax.dot_general(q, km_ref[_head_tile(u, g)], NT_DIMS,
                                                 preferred_element_type=F32)
        s_buf[:, :T] = s

    def stage_softmax(u, g, s_buf, p_buf, a_buf, diag, fresh=None):
        w = DIAG_TK if diag else T
        ht = _head_tile(u, g)
        m_prev = m_ref[ht]
        if fresh is not None:
            m_prev = jnp.where(fresh, NEG, m_prev)
        m_ref[ht], a_buf[...], p_buf[:, :w] = _softmax_update(s_buf[:, :w], m_prev)

    def stage_pv(u, g, p_buf, a_buf, diag):
        vi = g >> 1 if u == 0 else g
        pv = jnp.dot(p_buf[:, :T], v_refs[u][vi], preferred_element_type=F32)
        if diag:
            pv = pv + jnp.dot(p_buf[:, T:T + MK], vm_refs[u][vi], preferred_element_type=F32)
        _acc_update(u, g, acc_a, acc_b, a_buf[...], pv)

    def body(g, par, cd, nd):
        for u in range(PIPE_U):
            if isinstance(g, int) and g + 2 >= N_GROUPS:
                stage_qk(u, g + 2 - N_GROUPS, s_bufs[par][u],
                         nxt_rows((qna_ref, qnb_ref)), nxt_rows((kna_ref, knb_ref)), nd)
            else:
                stage_qk(u, g + 2, s_bufs[par][u], cur_rows(q_ref), cur_rows(k_ref), cd)
        for u in range(PIPE_U):
            stage_pv(u, g, p_bufs[par][u], a_bufs[par][u], cd)
        for u in range(PIPE_U):
            bufs = (s_bufs[1 - par][u], p_bufs[1 - par][u], a_bufs[1 - par][u])
            if isinstance(g, int) and g + 1 >= N_GROUPS:
                stage_softmax(u, g + 1 - N_GROUPS, *bufs, nd, fresh=nxt_first)
            else:
                stage_softmax(u, g + 1, *bufs, cd)

    @pl.when(cur_first)
    def _():
        acc_a[...] = jnp.zeros(acc_a.shape, F32)
        acc_b[...] = jnp.zeros(acc_b.shape, F32)
        fresh_m = jnp.full((N_HM - 1,) + m_ref.shape[1:], NEG, F32)
        m_ref[1:N_HM] = fresh_m
        m_ref[N_HM + 1:] = fresh_m

    @pl.when(step == 0)
    def _():
        _fill_diag_bias(bias_ref, coef_ref, tq=T, tk=T, pos_off=0)
        for u in range(PIPE_U):
            m_ref[_head_tile(u, 0)] = jnp.full(m_ref.shape[1:], NEG, F32)
        for g in range(2):
            for u in range(PIPE_U):
                stage_qk(u, g, s_bufs[g][u], cur_rows(q_ref), cur_rows(k_ref), True)
        for u in range(PIPE_U):
            stage_softmax(u, 0, s_bufs[0][u], p_bufs[0][u], a_bufs[0][u], True)

    for cd in (False, True):
        for nd in (False, True):
            @pl.when(jnp.logical_and(cur_diag if cd else jnp.logical_not(cur_diag),
                                     nxt_diag if nd else jnp.logical_not(nxt_diag)))
            def _(cd=cd, nd=nd):
                def loop_body(j, c):
                    body(2 * j, 0, cd, nd)
                    body(2 * j + 1, 1, cd, nd)
                    return c

                lax.fori_loop(0, (N_GROUPS - 2) // 2, loop_body, 0)
                body(N_GROUPS - 2, 0, cd, nd)
                body(N_GROUPS - 1, 1, cd, nd)
                if cd:
                    _attn_finalize(lam_ref, g_ref, acc_a, acc_b, out_ref)


def _meta_attn_kernel(coef_ref, q_ref, k_ref, va_ref, vb_ref, lam_ref, g_ref, out_ref,
                      m_ref, acc_a, acc_b, bias_ref):
    _fill_diag_bias(bias_ref, coef_ref, tq=META_BLK, tk=META_BLK, pos_off=CHUNK - N_META)
    m_ref[...] = jnp.full(m_ref.shape, NEG, F32)
    acc_a[...] = jnp.zeros(acc_a.shape, F32)
    acc_b[...] = jnp.zeros(acc_b.shape, F32)

    def body(g, c):
        for u in range(PIPE_U):
            ht = _head_tile(u, g)
            s = lax.dot_general(q_ref[ht], k_ref[ht], NT_DIMS, preferred_element_type=F32)
            s = s + _diag_bias(u, g, bias_ref)
            m_ref[ht], alpha, p = _softmax_update(s, m_ref[ht])
            v = va_ref[g >> 1] if u == 0 else vb_ref[g]
            _acc_update(u, g, acc_a, acc_b, alpha, jnp.dot(p, v, preferred_element_type=F32))
        return c

    lax.fori_loop(0, N_GROUPS, body, 0)
    _attn_finalize(lam_ref, g_ref, acc_a, acc_b, out_ref)


N_PIPE_BUFS = 2 * PIPE_U


def _pipe_bufs(refs):
    n = N_PIPE_BUFS
    nest = lambda flat: [list(flat[:PIPE_U]), list(flat[PIPE_U:])]
    return nest(refs[:n]), nest(refs[n:2 * n]), nest(refs[2 * n:3 * n])


def _attn_scratch(tq, tk, buf_w=None):
    state = [pltpu.VMEM((N_HT, tq, 1), F32), pltpu.VMEM((N_HM, tq, VA_W), F32),
             pltpu.VMEM((FOX_HEADS, tq, VB_W), F32), pltpu.VMEM((DIFF_HEADS + 1, tq, tk), F32)]
    if buf_w is None:
        return state
    return (state + [pltpu.VMEM((tq, buf_w), F32)] * N_PIPE_BUFS
            + [pltpu.VMEM((tq, buf_w), BF16)] * N_PIPE_BUFS
            + [pltpu.VMEM((tq, 1), F32)] * N_PIPE_BUFS)


def _attention(q_all, k_all, va, vb, lam, gain_row):
    coef = jnp.asarray([-2.0 * SLOPES[hm // 2] * LOG2E for hm in range(N_HM)], F32)
    smem = pl.BlockSpec(memory_space=pltpu.SMEM)
    meta_blk = NF // META_BLK

    def meta_spec(heads, w):
        return pl.BlockSpec((heads, META_BLK, w), lambda *_: (0, meta_blk, 0))

    const2 = lambda shape: pl.BlockSpec(shape, lambda *_: (0, 0))

    out_meta = pl.pallas_call(
        _meta_attn_kernel,
        grid=(1,),
        in_specs=[smem, meta_spec(N_HT, LANES), meta_spec(N_HT, LANES),
                  meta_spec(DIFF_HEADS, VA_W), meta_spec(FOX_HEADS, VB_W),
                  const2((4, DIFF_HEAD_DIM)), const2((8, LANES))],
        out_specs=pl.BlockSpec((META_BLK, ATT_OUT_W), lambda s: (0, 0)),
        out_shape=jax.ShapeDtypeStruct((META_BLK, ATT_OUT_W), BF16),
        scratch_shapes=_attn_scratch(META_BLK, META_BLK),
        compiler_params=_cparams(("arbitrary",)),
        name="attn_meta",
    )(coef, q_all, k_all, va, vb, lam, gain_row)

    n_q = SEQ // ATT_T
    qi_tab = np.concatenate([np.full(q + 1, q, np.int32) for q in range(n_q)])
    kv_tab = np.concatenate([np.arange(q + 1, dtype=np.int32) for q in range(n_q)])

    assert len(qi_tab) == STEPS_PER_BATCH

    def row_block(tab_of, ahead):
        def index(s, qi, kv):
            s = jnp.minimum(s + ahead, ATT_STEPS - 1)
            return (s // STEPS_PER_BATCH) * n_q + tab_of(qi, kv)[s % STEPS_PER_BATCH]
        return index

    q_blk, q_nxt = row_block(lambda qi, kv: qi, 0), row_block(lambda qi, kv: qi, 1)
    k_blk, k_nxt = row_block(lambda qi, kv: kv, 0), row_block(lambda qi, kv: kv, 1)

    def tile_spec(heads, w, blk, head_blk=0):
        return pl.BlockSpec((heads, ATT_T, w), lambda s, qi, kv: (head_blk, blk(s, qi, kv), 0))

    def lookahead_specs(blk):
        n = 2
        return [tile_spec(n, LANES, blk), tile_spec(n, LANES, blk, head_blk=N_HM // n)]

    def meta_keys_spec(heads, w):
        return pl.BlockSpec((heads, META_KEYS, w), lambda *_: (0, NF // META_KEYS, 0))

    grid_spec = pltpu.PrefetchScalarGridSpec(
        num_scalar_prefetch=2,
        grid=(ATT_STEPS,),
        in_specs=[smem, tile_spec(N_HT, LANES, q_blk), *lookahead_specs(q_nxt),
                  tile_spec(N_HT, LANES, k_blk), *lookahead_specs(k_nxt),
                  tile_spec(DIFF_HEADS, VA_W, k_blk), tile_spec(FOX_HEADS, VB_W, k_blk),
                  meta_keys_spec(N_HT, LANES), meta_keys_spec(DIFF_HEADS, VA_W), meta_keys_spec(FOX_HEADS, VB_W),
                  const2((4, DIFF_HEAD_DIM)), const2((8, LANES))],
        out_specs=pl.BlockSpec((ATT_T, ATT_OUT_W), lambda s, qi, kv: (q_blk(s, qi, kv), 0)),
        scratch_shapes=_attn_scratch(ATT_T, ATT_T, DIAG_TK),
    )
    out_frames = pl.pallas_call(
        _attn_kernel,
        grid_spec=grid_spec,
        out_shape=jax.ShapeDtypeStruct((NF, ATT_OUT_W), BF16),
        compiler_params=_cparams(("arbitrary",)),
        name="attn_frames",
    )(jnp.asarray(qi_tab), jnp.asarray(kv_tab), coef,
      q_all, q_all, q_all, k_all, k_all, k_all, va, vb, k_all, va, vb, lam, gain_row)
    return out_frames, out_meta


def _log_gamma(h):
    return jnp.log1p(jnp.full((1, 1), -(2.0 ** (-5.0 - h)), F32))


def _retention_kernel(q_ref, kt_ref, v_ref, g_ref, ktm_ref, vm_ref, o_ref, state_ref, decay_ref):
    b = pl.program_id(0)
    t = pl.program_id(1)
    T = RET_T

    @pl.when(jnp.logical_and(b == 0, t == 0))
    def _():
        i = lax.broadcasted_iota(jnp.int32, (T, T), 0)
        j = lax.broadcasted_iota(jnp.int32, (T, T), 1)
        dist = jnp.abs(i - j).astype(F32)
        vis = (j >> 6) <= (i >> 6)
        for h in range(RET_HEADS):
            decay_ref[h] = jnp.where(vis, jnp.exp(_log_gamma(h) * dist), 0.0)

    @pl.when(t == 0)
    def _():
        m = lax.broadcasted_iota(jnp.int32, (1, META_BLK), 1)
        for h in range(RET_HEADS):
            w = jnp.exp(_log_gamma(h) * (N_META - 1 - m).astype(F32))
            kt = ktm_ref[h * RET_QK_DIM:(h + 1) * RET_QK_DIM, :].astype(F32)
            kd = jnp.where(m < N_META, kt * w, 0.0).astype(BF16)
            state_ref[h] = jnp.dot(kd, vm_ref[:, h * RET_V_DIM:(h + 1) * RET_V_DIM],
                                   preferred_element_type=F32)

    row = lax.broadcasted_iota(jnp.int32, (T, 1), 0).astype(F32)
    col = lax.broadcasted_iota(jnp.int32, (1, T), 1).astype(F32)
    for h in range(RET_HEADS):
        lg = _log_gamma(h)
        q = q_ref[:, h * RET_QK_DIM:(h + 1) * RET_QK_DIM]
        kt = kt_ref[h * RET_QK_DIM:(h + 1) * RET_QK_DIM, :]
        v = v_ref[:, h * RET_V_DIM:(h + 1) * RET_V_DIM]
        scores = jnp.dot(q, kt, preferred_element_type=F32) * decay_ref[h]
        o = jnp.dot(scores.astype(BF16), v, preferred_element_type=F32)
        qd = (q.astype(F32) * jnp.exp(lg * (row + 1.0))).astype(BF16)
        state = state_ref[h]
        o = o + jnp.dot(qd, state.astype(BF16), preferred_element_type=F32)
        kd = (kt.astype(F32) * jnp.exp(lg * (T - 1.0 - col))).astype(BF16)
        state_ref[h] = jnp.exp(lg * float(T)) * state + jnp.dot(kd, v, preferred_element_type=F32)
        o = o * lax.rsqrt(jnp.mean(o * o, axis=1, keepdims=True) + RMS_EPS)
        gate = g_ref[:, h * RET_V_DIM:(h + 1) * RET_V_DIM].astype(F32)
        o_ref[:, h * RET_V_DIM:(h + 1) * RET_V_DIM] = (gate * jax.nn.sigmoid(gate) * o).astype(BF16)


def _retention(q, kt, v, g):
    n_t = SEQ // RET_T
    qk_w = RET_HEADS * RET_QK_DIM
    v_w = RET_HEADS * RET_V_DIM
    meta_blk = NF // META_BLK
    return pl.pallas_call(
        _retention_kernel,
        grid=(BATCH, n_t),
        in_specs=[pl.BlockSpec((RET_T, qk_w), lambda b, t: (b * n_t + t, 0)),
                  pl.BlockSpec((qk_w, RET_T), lambda b, t: (0, b * n_t + t)),
                  pl.BlockSpec((RET_T, v_w), lambda b, t: (b * n_t + t, 0)),
                  pl.BlockSpec((RET_T, v_w), lambda b, t: (b * n_t + t, 0)),
                  pl.BlockSpec((qk_w, META_BLK), lambda b, t: (0, meta_blk)),
                  pl.BlockSpec((META_BLK, v_w), lambda b, t: (meta_blk, 0))],
        out_specs=pl.BlockSpec((RET_T, v_w), lambda b, t: (b * n_t + t, 0)),
        out_shape=jax.ShapeDtypeStruct((NF, v_w), BF16),
        scratch_shapes=[pltpu.VMEM((RET_HEADS, RET_QK_DIM, RET_V_DIM), F32),
                        pltpu.VMEM((RET_HEADS, RET_T, RET_T), F32)],
        compiler_params=_cparams(("arbitrary", "arbitrary")),
        name="retention",
    )(q, kt, v, g, kt, v)


def _rows8(*rows):
    n = rows[0].shape[-1]
    out = jnp.zeros((8, n), F32)
    for i, r in enumerate(rows):
        out = out.at[i].set(r.astype(F32))
    return out


def _unit_coef(n):
    return _coef_rows(n, {})


TM_ALL = 2064
TM_LN_ALL = 768
TM_FRAMES = 1024


def kernel(x, meta_tokens, even_w_in, even_f_bias, diff_lambda, diff_subln_g, even_w_out,
           ret_w_in, ret_w_out, ln_g, ln_b, ffn_w1, ffn_w2):
    frames = x.reshape(NF, D_MODEL)
    meta = jnp.pad(meta_tokens.astype(F32), ((0, META_BLK - N_META), (0, 0)))

    w_even, c_even = _even_weights(even_w_in[0]), _even_consts()
    f_bias_row = _rows8(jnp.pad(even_f_bias[0], (0, LANES - FOX_HEADS)))
    proj = _even_proj(frames, w_even, c_even, f_bias_row, tm=512, row0=0, meta=False, cover_tail=True,
                      name="proj_even")
    q_all, k_all, va, vb = _even_proj(meta, w_even, c_even, f_bias_row, tm=META_BLK, row0=NF, meta=True,
                                      into=proj, name="proj_even_meta")

    attn, attn_meta = _attention(q_all, k_all, va, vb, diff_lambda[0].astype(F32), _rows8(diff_subln_g[0]))

    w_out, gb = even_w_out[0].astype(BF16), _rows8(ln_g[0, 0], ln_b[0, 0])
    h1s = _mm_res_ln(attn, w_out, frames, gb, rows=NF, tm=TM_FRAMES, out_rows=NT, cover_tail=True,
                     name="even_out_ln")
    h1, h1_16 = _mm_res_ln(attn_meta, w_out, meta, gb, rows=META_BLK, tm=META_BLK, out_row0=NF, out_rows=NT,
                           into=h1s, name="even_out_ln_meta")
    f1 = _mm_relu2(h1_16, ffn_w1[0].astype(BF16), rows=NT, tm=TM_ALL, tn=1024, name="ffn0_up")
    h2, h2_16 = _mm_res_ln(f1, ffn_w2[0].astype(BF16), h1, _rows8(ln_g[0, 1], ln_b[0, 1]),
                           rows=NT, tm=TM_LN_ALL, name="ffn0_down_ln")

    qk_w = RET_HEADS * RET_QK_DIM
    v_w = RET_HEADS * RET_V_DIM
    rw = ret_w_in[0]
    rq = _proj(h2_16, rw[:, :qk_w].astype(BF16), _unit_coef(qk_w), tm=TM_ALL, name="proj_ret_q")
    rkt = _proj_t((rw[:, qk_w:2 * qk_w] * RET_QK_DIM ** -0.5).T.astype(BF16), h2_16, tm=TM_LN_ALL)
    rv = _proj(h2_16, rw[:, 2 * qk_w:2 * qk_w + v_w].astype(BF16), _unit_coef(v_w), tm=TM_ALL, name="proj_ret_v")
    rg = _proj(h2_16, rw[:, 2 * qk_w + v_w:].astype(BF16), _unit_coef(v_w), tm=TM_ALL, name="proj_ret_g")
    y = _retention(rq, rkt, rv, rg)

    h3, h3_16 = _mm_res_ln(y, ret_w_out[0].astype(BF16), h2, _rows8(ln_g[1, 0], ln_b[1, 0]),
                           rows=NF, tm=TM_FRAMES, name="ret_out_ln")
    f2 = _mm_relu2(h3_16, ffn_w1[1].astype(BF16), rows=NF, tm=2048, tn=1024, name="ffn1_up")
    out, _ = _mm_res_ln(f2, ffn_w2[1].astype(BF16), h3, _rows8(ln_g[1, 1], ln_b[1, 1]),
                        rows=NF, tm=TM_FRAMES, name="ffn1_down_ln")
    return out.reshape(BATCH, SEQ, D_MODEL)
```

```python
import functools
import math

import jax
import jax.numpy as jnp
import numpy as np
from jax import lax
from jax.experimental import pallas as pl
from jax.experimental.pallas import tpu as pltpu

F32 = jnp.float32
BF16 = jnp.bfloat16

D_MODEL = 1024
BATCH = 4
SEQ = 8192
DEPTH = 2
CHUNK = 64
N_META = 16
DIFF_HEADS = 4
DIFF_HEAD_DIM = 64
DIFF_V_DIM = 128
FOX_HEADS = 8
FOX_HEAD_DIM = 64
RET_HEADS = 4
RET_QK_DIM = 256
RET_V_DIM = 512
D_FF = 4 * D_MODEL
DEEPNORM_ALPHA = (2 * DEPTH) ** 0.25
LN_EPS = 1e-5
RMS_EPS = 1e-6
LAM_INIT0 = 0.8 - 0.6 * math.exp(-0.3 * 0)
LOG2E = math.log2(math.e)

NF = BATCH * SEQ
META_BLK = 256
NT = NF + META_BLK
LANES = 128
NEG = -1e30

ATT_T = 512
META_KEYS = 128
RET_T = 256
VMEM_LIMIT = 56 * 1024 * 1024


def _cparams(sem):
    return pltpu.CompilerParams(dimension_semantics=sem, vmem_limit_bytes=VMEM_LIMIT)


def _bf16_pieces(x, n=3):
    out = []
    r = np.float32(x)
    for _ in range(n):
        p = np.float32(np.asarray(r, dtype=BF16).astype(np.float32))
        out.append(float(p))
        r = np.float32(r - p)
    return out


LOG2E_PIECES = _bf16_pieces(LOG2E)


def _row_positions(i, tm, row0=0):
    r = row0 + i * tm + lax.broadcasted_iota(jnp.int32, (tm, 1), 0)
    is_frame = r < NF
    m = r - NF
    pos = jnp.where(is_frame, (r & (SEQ - 1)) + N_META, m)
    valid = jnp.logical_or(is_frame, m < N_META)
    return pos, valid


def _proj_kernel(a_ref, w_ref, c_ref, o_ref, *, tm, heads, head_w, use_pos):
    y = jnp.dot(a_ref[...], w_ref[...], preferred_element_type=F32)
    y = y * c_ref[4:5, :] + c_ref[0:1, :]
    if use_pos:
        pos, valid = _row_positions(pl.program_id(0), tm)
        hi = (pos >> 7).astype(F32)
        lo = (pos & 127).astype(F32)
        y = y + hi * c_ref[1:2, :] + lo * c_ref[2:3, :] + jnp.where(valid, 0.0, 1.0) * c_ref[3:4, :]
    if heads is None:
        o_ref[...] = y.astype(o_ref.dtype)
    else:
        for h in range(heads):
            o_ref[h] = y[:, h * head_w:(h + 1) * head_w].astype(o_ref.dtype)


def _proj(a, w, coef, *, tm, name, heads=None, head_w=None, use_pos=False, out_dtype=BF16):
    m, k = a.shape
    n = w.shape[1]
    if heads is None:
        out_shape = jax.ShapeDtypeStruct((m, n), out_dtype)
        out_spec = pl.BlockSpec((tm, n), lambda i: (i, 0))
    else:
        out_shape = jax.ShapeDtypeStruct((heads, m, head_w), out_dtype)
        out_spec = pl.BlockSpec((heads, tm, head_w), lambda i: (0, i, 0))
    return pl.pallas_call(
        functools.partial(_proj_kernel, tm=tm, heads=heads, head_w=head_w, use_pos=use_pos),
        grid=(m // tm,),
        in_specs=[pl.BlockSpec((tm, k), lambda i: (i, 0)),
                  pl.BlockSpec((k, n), lambda i: (0, 0), pipeline_mode=pl.Buffered(1)),
                  pl.BlockSpec((8, n), lambda i: (0, 0))],
        out_specs=out_spec,
        out_shape=out_shape,
        compiler_params=_cparams(("arbitrary",)),
        name=name,
    )(a, w, coef)


def _proj_t_kernel(wt_ref, a_ref, o_ref):
    o_ref[...] = lax.dot_general(wt_ref[...], a_ref[...], (((1,), (1,)), ((), ())),
                                 preferred_element_type=F32).astype(o_ref.dtype)


def _proj_t(wt, a, *, tm):
    n, k = wt.shape
    m = a.shape[0]
    return pl.pallas_call(
        _proj_t_kernel,
        grid=(m // tm,),
        in_specs=[pl.BlockSpec((n, k), lambda i: (0, 0)),
                  pl.BlockSpec((tm, k), lambda i: (i, 0))],
        out_specs=pl.BlockSpec((n, tm), lambda i: (0, i)),
        out_shape=jax.ShapeDtypeStruct((n, m), BF16),
        compiler_params=_cparams(("arbitrary",)),
        name="proj_ret_kt",
    )(wt, a)


def _proj_split_kernel(a_ref, w_ref, *o_refs):
    y = jnp.dot(a_ref[...], w_ref[...], preferred_element_type=F32)
    col = 0
    for o_ref in o_refs:
        width = o_ref.shape[1]
        o_ref[...] = y[:, col:col + width].astype(o_ref.dtype)
        col += width


def _proj_split(a, w, widths, *, tm, name):
    m, k = a.shape
    n = w.shape[1]
    assert sum(widths) == n
    return pl.pallas_call(
        _proj_split_kernel,
        grid=(m // tm,),
        in_specs=[pl.BlockSpec((tm, k), lambda i: (i, 0)),
                  pl.BlockSpec((k, n), lambda i: (0, 0), pipeline_mode=pl.Buffered(1))],
        out_specs=[pl.BlockSpec((tm, width), lambda i: (i, 0)) for width in widths],
        out_shape=[jax.ShapeDtypeStruct((m, width), BF16) for width in widths],
        compiler_params=_cparams(("arbitrary",)),
        name=name,
    )(a, w)


def _relu2_kernel(a_ref, w_ref, o_ref):
    y = jnp.dot(a_ref[...], w_ref[...], preferred_element_type=F32)
    y = jnp.maximum(y, 0.0)
    o_ref[...] = (y * y).astype(o_ref.dtype)


def _mm_relu2(a, w, *, rows, tm, tn, name):
    k = a.shape[1]
    n = w.shape[1]
    return pl.pallas_call(
        _relu2_kernel,
        grid=(rows // tm, n // tn),
        in_specs=[pl.BlockSpec((tm, k), lambda i, j: (i, 0)),
                  pl.BlockSpec((k, tn), lambda i, j: (0, j))],
        out_specs=pl.BlockSpec((tm, tn), lambda i, j: (i, j)),
        out_shape=jax.ShapeDtypeStruct((rows, n), BF16),
        compiler_params=_cparams(("arbitrary", "arbitrary")),
        name=name,
    )(a, w)


def _res_ln_kernel(a_ref, w_ref, r_ref, gb_ref, *refs):
    o32_ref, o16_ref = refs[-2:]
    y = jnp.dot(a_ref[...], w_ref[...], preferred_element_type=F32)
    z = DEEPNORM_ALPHA * r_ref[...] + y
    mu = jnp.mean(z, axis=-1, keepdims=True)
    zc = z - mu
    var = jnp.mean(zc * zc, axis=-1, keepdims=True)
    out = zc * lax.rsqrt(var + LN_EPS) * gb_ref[0:1, :] + gb_ref[1:2, :]
    o32_ref[...] = out
    o16_ref[...] = out.astype(BF16)


def _mm_res_ln(a, w, res, gb, *, rows, tm, name, out_row0=0, out_rows=None, into=None, cover_tail=False):
    k = a.shape[1]
    n = w.shape[1]
    blk0 = out_row0 // tm
    steps = rows // tm
    out_rows = rows if out_rows is None else out_rows
    n_alias = 0 if into is None else len(into)
    src = lambda i: (jnp.minimum(i, steps - 1), 0)
    return pl.pallas_call(
        _res_ln_kernel,
        grid=(steps + int(cover_tail),),
        in_specs=[pl.BlockSpec((tm, k), src),
                  pl.BlockSpec((k, n), lambda i: (0, 0), pipeline_mode=pl.Buffered(1)),
                  pl.BlockSpec((tm, n), src),
                  pl.BlockSpec((8, n), lambda i: (0, 0))] + [pl.BlockSpec(memory_space=pl.ANY)] * n_alias,
        out_specs=[pl.BlockSpec((tm, n), lambda i: (blk0 + i, 0)),
                   pl.BlockSpec((tm, n), lambda i: (blk0 + i, 0))],
        out_shape=[jax.ShapeDtypeStruct((out_rows, n), F32),
                   jax.ShapeDtypeStruct((out_rows, n), BF16)],
        input_output_aliases={4 + j: j for j in range(n_alias)},
        compiler_params=_cparams(("arbitrary",)),
        name=name,
    )(a, w, res, gb, *(into or ()))


N_HM = 2 * DIFF_HEADS
N_HT = N_HM + FOX_HEADS
VA_W = 2 * LANES
VB_W = LANES
BIAS_LANE0 = DIFF_HEAD_DIM
DIFF_BETA_PIECES = 2
FOX_BETA_PIECES = 3
ATT_OUT_W = DIFF_HEADS * DIFF_V_DIM + FOX_HEADS * FOX_HEAD_DIM
SLOPES = [2.0 ** (-8.0 * (h + 1) / DIFF_HEADS) for h in range(DIFF_HEADS)]


def _coef_rows(n, rows):
    c = np.zeros((8, n), np.float32)
    c[4, :] = 1.0
    for r, vals in rows.items():
        for col, val in vals:
            c[r, col] = val
    return jnp.asarray(c)


EVEN_QK_W = N_HT * DIFF_HEAD_DIM
EVEN_VA_W = DIFF_HEADS * DIFF_V_DIM
EVEN_VB_W = FOX_HEADS * FOX_HEAD_DIM
EVEN_N = 2 * EVEN_QK_W + EVEN_VA_W + EVEN_VB_W + LANES
Q_SCALE = DIFF_HEAD_DIM ** -0.5 * LOG2E
ROW_QCONST, ROW_KHI, ROW_KLO, ROW_KPAD, ROW_VCONST = 0, N_HT, 2 * N_HT, 3 * N_HT, 4 * N_HT


def _even_weights(w_in):
    qk_w = DIFF_HEADS * 2 * DIFF_HEAD_DIM
    v_w = DIFF_HEADS * DIFF_V_DIM
    fox_w = FOX_HEADS * FOX_HEAD_DIM
    o = np.cumsum([0, qk_w, qk_w, v_w, fox_w, fox_w, fox_w, FOX_HEADS])
    sl = [w_in[:, o[i]:o[i + 1]] for i in range(7)]
    w_fb = jnp.pad(sl[6], ((0, 0), (0, LANES - FOX_HEADS)))
    return jnp.concatenate([sl[0], sl[3], sl[1], sl[4], sl[2], sl[5], w_fb], axis=1).astype(BF16)


def _even_consts():
    c = np.zeros((4 * N_HT + 8, LANES), np.float32)
    for ht in range(N_HT):
        pieces = DIFF_BETA_PIECES if ht < N_HM else FOX_BETA_PIECES
        for p in range(pieces):
            for r in range(3):
                c[ROW_QCONST + ht, BIAS_LANE0 + 3 * p + r] = LOG2E_PIECES[r]
        if ht < N_HM:
            slope = SLOPES[ht // 2]
            c[ROW_KHI + ht, BIAS_LANE0:BIAS_LANE0 + 3] = slope * 128.0
            c[ROW_KLO + ht, BIAS_LANE0 + 3:BIAS_LANE0 + 6] = slope
            c[ROW_KPAD + ht, BIAS_LANE0] = NEG
    c[ROW_VCONST, 0] = 1.0
    c[ROW_VCONST + 1, FOX_HEAD_DIM] = 1.0
    return jnp.asarray(c)


PIECE_LANES = 8


def _pack3(x):
    hi = x.astype(BF16).astype(F32)
    r1 = x - hi
    mid = r1.astype(BF16).astype(F32)
    lo = (r1 - mid).astype(BF16).astype(F32)
    return (hi + pltpu.roll(mid, PIECE_LANES, 1) + pltpu.roll(lo, 2 * PIECE_LANES, 1)).astype(BF16)


def _unpack3_sum(y):
    return y + pltpu.roll(y, LANES - PIECE_LANES, 1) + pltpu.roll(y, LANES - 2 * PIECE_LANES, 1)


def _forget_bias_lanes(logits, fbias_ref, tri_ref, sel_ref, carry_ref, *, tm, meta):
    @pl.when(pl.program_id(0) % (SEQ // tm) == 0)
    def _():
        carry_ref[...] = jnp.zeros_like(carry_ref)

    heads = lax.broadcasted_iota(jnp.int32, (1, LANES), 1) < FOX_HEADS
    x = logits + fbias_ref[0:1, :]
    logf = jnp.minimum(x, 0.0) - jnp.log1p(jnp.exp(-jnp.abs(x)))
    part = jnp.dot(tri_ref[...], _pack3(jnp.where(heads, logf, 0.0)), preferred_element_type=F32)
    cum = carry_ref[0:1, :] + _unpack3_sum(part)
    carry_ref[0:1, :] = cum[tm - 1:tm, :]
    if meta:
        row = lax.broadcasted_iota(jnp.int32, (tm, 1), 0)
        g = jnp.where(row < N_META, cum[N_META - 1:N_META, :] - cum, NEG)
    else:
        g = -cum
    return jnp.dot(_pack3(jnp.where(heads, g, 0.0)), sel_ref[...], preferred_element_type=F32)


def _even_proj_kernel(a_ref, w_ref, c_ref, fbias_ref, tri_ref, sel_ref, *refs, tm, row0, meta):
    q_ref, k_ref, va_ref, vb_ref, carry_ref = refs[-5:]
    a = a_ref[...].astype(BF16)
    logits = jnp.dot(a, w_ref[:, EVEN_N - LANES:], preferred_element_type=F32)
    fox_bias = _forget_bias_lanes(logits, fbias_ref, tri_ref, sel_ref, carry_ref, tm=tm, meta=meta)
    y = jnp.dot(a, w_ref[:, :EVEN_N - LANES], preferred_element_type=F32)
    lane = lax.broadcasted_iota(jnp.int32, (1, LANES), 1)
    real = lane < DIFF_HEAD_DIM
    pos, valid = _row_positions(pl.program_id(0), tm, row0)
    hi = (pos >> 7).astype(F32)
    lo = (pos & 127).astype(F32)
    pad = jnp.where(valid, 0.0, 1.0)

    def spread(pair, odd):
        return pltpu.roll(pair, DIFF_HEAD_DIM, 1) if odd else pair

    for j in range(N_HT // 2):
        qp = y[:, j * LANES:(j + 1) * LANES] * Q_SCALE
        kp = y[:, EVEN_QK_W + j * LANES:EVEN_QK_W + (j + 1) * LANES]
        for odd in range(2):
            ht = 2 * j + odd
            row = lambda r: c_ref[r + ht:r + ht + 1, :]
            q_ref[ht] = jnp.where(real, spread(qp, odd), row(ROW_QCONST)).astype(BF16)
            if ht < N_HM:
                k_bias = hi * row(ROW_KHI) + lo * row(ROW_KLO) + pad * row(ROW_KPAD)
            else:
                k_bias = fox_bias[:, (ht - N_HM) * LANES:(ht - N_HM + 1) * LANES]
            k_ref[ht] = jnp.where(real, spread(kp, odd), k_bias).astype(BF16)
    base = 2 * EVEN_QK_W
    ones_half = jnp.broadcast_to(c_ref[ROW_VCONST:ROW_VCONST + 1, :], (tm, LANES)).astype(BF16)
    for h in range(DIFF_HEADS):
        va_ref[h, :, :LANES] = y[:, base + h * LANES:base + (h + 1) * LANES].astype(BF16)
        va_ref[h, :, LANES:] = ones_half
    base += EVEN_VA_W
    for j in range(FOX_HEADS // 2):
        vp = y[:, base + j * LANES:base + (j + 1) * LANES]
        for odd in range(2):
            vb_ref[2 * j + odd] = jnp.where(real, spread(vp, odd),
                                            c_ref[ROW_VCONST + 1:ROW_VCONST + 2, :]).astype(BF16)


def _even_proj(a, w, consts, fbias_row, *, tm, row0, meta, into=None, cover_tail=False, name):
    m, k = a.shape
    blk0 = row0 // tm
    steps = m // tm
    n_alias = 0 if into is None else len(into)
    tri = jnp.tril(jnp.ones((tm, tm), F32)).astype(BF16)
    sel = np.zeros((LANES, FOX_HEADS * LANES), np.float32)
    for p in range(FOX_BETA_PIECES):
        for h in range(FOX_HEADS):
            for r in range(3):
                sel[PIECE_LANES * p + h, h * LANES + BIAS_LANE0 + 3 * p + r] = 1.0
    sel = jnp.asarray(sel, BF16)
    const2 = lambda arr: pl.BlockSpec(arr.shape, lambda i: (0, 0))
    n_in = 6
    return pl.pallas_call(
        functools.partial(_even_proj_kernel, tm=tm, row0=row0, meta=meta),
        grid=(steps + int(cover_tail),),
        in_specs=[pl.BlockSpec((tm, k), lambda i: (jnp.minimum(i, steps - 1), 0)),
                  pl.BlockSpec((k, EVEN_N), lambda i: (0, 0), pipeline_mode=pl.Buffered(1)),
                  const2(consts), const2(fbias_row), const2(tri), const2(sel)]
                 + [pl.BlockSpec(memory_space=pl.ANY)] * n_alias,
        out_specs=[pl.BlockSpec((N_HT, tm, LANES), lambda i: (0, blk0 + i, 0)),
                   pl.BlockSpec((N_HT, tm, LANES), lambda i: (0, blk0 + i, 0)),
                   pl.BlockSpec((DIFF_HEADS, tm, VA_W), lambda i: (0, blk0 + i, 0)),
                   pl.BlockSpec((FOX_HEADS, tm, VB_W), lambda i: (0, blk0 + i, 0))],
        out_shape=[jax.ShapeDtypeStruct((N_HT, NT, LANES), BF16),
                   jax.ShapeDtypeStruct((N_HT, NT, LANES), BF16),
                   jax.ShapeDtypeStruct((DIFF_HEADS, NT, VA_W), BF16),
                   jax.ShapeDtypeStruct((FOX_HEADS, NT, VB_W), BF16)],
        scratch_shapes=[pltpu.VMEM((8, LANES), F32)],
        input_output_aliases={n_in + j: j for j in range(n_alias)},
        compiler_params=_cparams(("arbitrary",)),
        name=name,
    )(a, w, consts, fbias_row, tri, sel, *(into or ()))


def _fill_diag_bias(bias_ref, coef_ref, *, tq, tk, pos_off):
    i = lax.broadcasted_iota(jnp.int32, (tq, tk), 0)
    j = lax.broadcasted_iota(jnp.int32, (tq, tk), 1)
    ahead = jnp.maximum(j - i, 0).astype(F32)
    chunk_mask = jnp.where(((j + pos_off) >> 6) > ((i + pos_off) >> 6), NEG, 0.0)
    for h in range(DIFF_HEADS):
        bias_ref[h] = ahead * coef_ref[2 * h] + chunk_mask
    bias_ref[DIFF_HEADS] = jnp.where(j > i, NEG, 0.0)


def _softmax_update(s, m_prev):
    m_new = jnp.maximum(m_prev, jnp.max(s, axis=1, keepdims=True))
    alpha = jnp.exp2(m_prev - m_new)
    p = jnp.exp2(s - m_new).astype(BF16)
    return m_new, alpha, p


PIPE_U = 2
N_GROUPS = N_HT // PIPE_U
NT_DIMS = (((1,), (1,)), ((), ()))


def _head_tile(u, g):
    return g + N_HM * u


def _diag_bias(u, g, bias_ref):
    return bias_ref[g >> 1 if u == 0 else DIFF_HEADS]


def _acc_update(u, g, acc_a, acc_b, alpha, pv):
    if u == 0:
        acc_a[g] = alpha * acc_a[g] + pv
    else:
        acc_b[g] = alpha * acc_b[g] + pv


def _attn_finalize(lam_ref, g_ref, acc_a, acc_b, out_ref):
    lv = lam_ref[...]
    lam = (jnp.exp(jnp.sum(lv[0:1] * lv[1:2], axis=1, keepdims=True))
           - jnp.exp(jnp.sum(lv[2:3] * lv[3:4], axis=1, keepdims=True)) + LAM_INIT0)
    gain = g_ref[0:1, :] * (1.0 - LAM_INIT0)
    for h in range(DIFF_HEADS):
        a0 = acc_a[2 * h]
        a1 = acc_a[2 * h + 1]
        o = (a0[:, :DIFF_V_DIM] * (1.0 / a0[:, DIFF_V_DIM:DIFF_V_DIM + 1])
             - a1[:, :DIFF_V_DIM] * (lam / a1[:, DIFF_V_DIM:DIFF_V_DIM + 1]))
        ms = jnp.mean(o * o, axis=1, keepdims=True)
        out_ref[:, h * DIFF_V_DIM:(h + 1) * DIFF_V_DIM] = (o * lax.rsqrt(ms + RMS_EPS) * gain).astype(BF16)
    base = DIFF_HEADS * DIFF_V_DIM
    low = lax.broadcasted_iota(jnp.int32, (1, LANES), 1) < FOX_HEAD_DIM
    for j in range(FOX_HEADS // 2):
        even, odd = acc_b[2 * j], acc_b[2 * j + 1]
        even = even * (1.0 / even[:, FOX_HEAD_DIM:FOX_HEAD_DIM + 1])
        odd = odd * (1.0 / odd[:, FOX_HEAD_DIM:FOX_HEAD_DIM + 1])
        out_ref[:, base + j * LANES:base + (j + 1) * LANES] = jnp.where(
            low, even, pltpu.roll(odd, FOX_HEAD_DIM, 1)).astype(BF16)


STEPS_PER_BATCH = (SEQ // ATT_T) * (SEQ // ATT_T + 1) // 2
ATT_STEPS = BATCH * STEPS_PER_BATCH
DIAG_TK = ATT_T + META_KEYS


def _attn_kernel(qi_ref, kv_ref, coef_ref,
                 q_ref, qna_ref, qnb_ref, k_ref, kna_ref, knb_ref, va_ref, vb_ref, km_ref, vma_ref, vmb_ref,
                 lam_ref, g_ref,
                 out_ref, m_ref, acc_a, acc_b, bias_ref, *pipe_refs):
    step = pl.program_id(0)
    nxt = jnp.minimum(step + 1, ATT_STEPS - 1)
    r, rn = step % STEPS_PER_BATCH, nxt % STEPS_PER_BATCH
    cur_diag = kv_ref[r] == qi_ref[r]
    nxt_diag = kv_ref[rn] == qi_ref[rn]
    cur_first = kv_ref[r] == 0
    nxt_first = kv_ref[rn] == 0
    s_bufs, p_bufs, a_bufs = _pipe_bufs(pipe_refs)
    T, MK = ATT_T, META_KEYS
    v_refs, vm_refs = (va_ref, vb_ref), (vma_ref, vmb_ref)
    cur_rows = lambda ref: (lambda u, g: ref[_head_tile(u, g)])
    nxt_rows = lambda refs: (lambda u, g: refs[u][g])

    def stage_qk(u, g, s_buf, q_of, k_of, diag):
        q = q_of(u, g)
        s = lax.dot_general(q, k_of(u, g), NT_DIMS, preferred_element_type=F32)
        if diag:
            s = s + _diag_bias(u, g, bias_ref)
            s_buf[:, T:T + MK] = lax.dot_general(q, km_ref[_head_tile(u, g)], NT_DIMS,
                                                 preferred_element_type=F32)
        s_buf[:, :T] = s

    def stage_softmax(u, g, s_buf, p_buf, a_buf, diag, fresh=None):
        w = DIAG_TK if diag else T
        ht = _head_tile(u, g)
        m_prev = m_ref[ht]
        if fresh is not None:
            m_prev = jnp.where(fresh, NEG, m_prev)
        m_ref[ht], a_buf[...], p_buf[:, :w] = _softmax_update(s_buf[:, :w], m_prev)

    def stage_pv(u, g, p_buf, a_buf, diag):
        vi = g >> 1 if u == 0 else g
        pv = jnp.dot(p_buf[:, :T], v_refs[u][vi], preferred_element_type=F32)
        if diag:
            pv = pv + jnp.dot(p_buf[:, T:T + MK], vm_refs[u][vi], preferred_element_type=F32)
        _acc_update(u, g, acc_a, acc_b, a_buf[...], pv)

    def body(g, par, cd, nd):
        for u in range(PIPE_U):
            if isinstance(g, int) and g + 2 >= N_GROUPS:
                stage_qk(u, g + 2 - N_GROUPS, s_bufs[par][u],
                         nxt_rows((qna_ref, qnb_ref)), nxt_rows((kna_ref, knb_ref)), nd)
            else:
                stage_qk(u, g + 2, s_bufs[par][u], cur_rows(q_ref), cur_rows(k_ref), cd)
        for u in range(PIPE_U):
            stage_pv(u, g, p_bufs[par][u], a_bufs[par][u], cd)
        for u in range(PIPE_U):
            bufs = (s_bufs[1 - par][u], p_bufs[1 - par][u], a_bufs[1 - par][u])
            if isinstance(g, int) and g + 1 >= N_GROUPS:
                stage_softmax(u, g + 1 - N_GROUPS, *bufs, nd, fresh=nxt_first)
            else:
                stage_softmax(u, g + 1, *bufs, cd)

    @pl.when(cur_first)
    def _():
        fresh_m = jnp.full((N_HM - 1,) + m_ref.shape[1:], NEG, F32)
        m_ref[1:N_HM] = fresh_m
        m_ref[N_HM + 1:] = fresh_m

    @pl.when(step == 0)
    def _():
        acc_a[...] = jnp.zeros(acc_a.shape, F32)
        acc_b[...] = jnp.zeros(acc_b.shape, F32)
        _fill_diag_bias(bias_ref, coef_ref, tq=T, tk=T, pos_off=0)
        for u in range(PIPE_U):
            m_ref[_head_tile(u, 0)] = jnp.full(m_ref.shape[1:], NEG, F32)
        for g in range(2):
            for u in range(PIPE_U):
                stage_qk(u, g, s_bufs[g][u], cur_rows(q_ref), cur_rows(k_ref), True)
        for u in range(PIPE_U):
            stage_softmax(u, 0, s_bufs[0][u], p_bufs[0][u], a_bufs[0][u], True)

    for cd in (False, True):
        for nd in (False, True):
            @pl.when(jnp.logical_and(cur_diag if cd else jnp.logical_not(cur_diag),
                                     nxt_diag if nd else jnp.logical_not(nxt_diag)))
            def _(cd=cd, nd=nd):
                def loop_body(j, c):
                    body(2 * j, 0, cd, nd)
                    body(2 * j + 1, 1, cd, nd)
                    return c

                lax.fori_loop(0, (N_GROUPS - 2) // 2, loop_body, 0)
                body(N_GROUPS - 2, 0, cd, nd)
                body(N_GROUPS - 1, 1, cd, nd)
                if cd:
                    _attn_finalize(lam_ref, g_ref, acc_a, acc_b, out_ref)


def _meta_attn_kernel(coef_ref, q_ref, k_ref, va_ref, vb_ref, lam_ref, g_ref, out_ref,
                      m_ref, acc_a, acc_b, bias_ref):
    _fill_diag_bias(bias_ref, coef_ref, tq=META_BLK, tk=META_BLK, pos_off=CHUNK - N_META)
    m_ref[...] = jnp.full(m_ref.shape, NEG, F32)
    acc_a[...] = jnp.zeros(acc_a.shape, F32)
    acc_b[...] = jnp.zeros(acc_b.shape, F32)

    def body(g, c):
        for u in range(PIPE_U):
            ht = _head_tile(u, g)
            s = lax.dot_general(q_ref[ht], k_ref[ht], NT_DIMS, preferred_element_type=F32)
            s = s + _diag_bias(u, g, bias_ref)
            m_ref[ht], alpha, p = _softmax_update(s, m_ref[ht])
            v = va_ref[g >> 1] if u == 0 else vb_ref[g]
            _acc_update(u, g, acc_a, acc_b, alpha, jnp.dot(p, v, preferred_element_type=F32))
        return c

    lax.fori_loop(0, N_GROUPS, body, 0)
    _attn_finalize(lam_ref, g_ref, acc_a, acc_b, out_ref)


N_PIPE_BUFS = 2 * PIPE_U


def _pipe_bufs(refs):
    n = N_PIPE_BUFS
    nest = lambda flat: [list(flat[:PIPE_U]), list(flat[PIPE_U:])]
    return nest(refs[:n]), nest(refs[n:2 * n]), nest(refs[2 * n:3 * n])


def _attn_scratch(tq, tk, buf_w=None):
    state = [pltpu.VMEM((N_HT, tq, 1), F32), pltpu.VMEM((N_HM, tq, VA_W), F32),
             pltpu.VMEM((FOX_HEADS, tq, VB_W), F32), pltpu.VMEM((DIFF_HEADS + 1, tq, tk), F32)]
    if buf_w is None:
        return state
    return (state + [pltpu.VMEM((tq, buf_w), F32)] * N_PIPE_BUFS
            + [pltpu.VMEM((tq, buf_w), BF16)] * N_PIPE_BUFS
            + [pltpu.VMEM((tq, 1), F32)] * N_PIPE_BUFS)


def _attention(q_all, k_all, va, vb, lam, gain_row):
    coef = jnp.asarray([-2.0 * SLOPES[hm // 2] * LOG2E for hm in range(N_HM)], F32)
    smem = pl.BlockSpec(memory_space=pltpu.SMEM)
    meta_blk = NF // META_BLK

    def meta_spec(heads, w):
        return pl.BlockSpec((heads, META_BLK, w), lambda *_: (0, meta_blk, 0))

    const2 = lambda shape: pl.BlockSpec(shape, lambda *_: (0, 0))

    out_meta = pl.pallas_call(
        _meta_attn_kernel,
        grid=(1,),
        in_specs=[smem, meta_spec(N_HT, LANES), meta_spec(N_HT, LANES),
                  meta_spec(DIFF_HEADS, VA_W), meta_spec(FOX_HEADS, VB_W),
                  const2((4, DIFF_HEAD_DIM)), const2((8, LANES))],
        out_specs=pl.BlockSpec((META_BLK, ATT_OUT_W), lambda s: (0, 0)),
        out_shape=jax.ShapeDtypeStruct((META_BLK, ATT_OUT_W), BF16),
        scratch_shapes=_attn_scratch(META_BLK, META_BLK),
        compiler_params=_cparams(("arbitrary",)),
        name="attn_meta",
    )(coef, q_all, k_all, va, vb, lam, gain_row)

    n_q = SEQ // ATT_T
    qi_tab = np.concatenate([np.full(q + 1, q, np.int32) for q in range(n_q)])
    kv_tab = np.concatenate([np.arange(q + 1, dtype=np.int32) for q in range(n_q)])

    assert len(qi_tab) == STEPS_PER_BATCH

    def row_block(tab_of, ahead):
        def index(s, qi, kv):
            s = jnp.minimum(s + ahead, ATT_STEPS - 1)
            return (s // STEPS_PER_BATCH) * n_q + tab_of(qi, kv)[s % STEPS_PER_BATCH]
        return index

    q_blk, q_nxt = row_block(lambda qi, kv: qi, 0), row_block(lambda qi, kv: qi, 1)
    k_blk, k_nxt = row_block(lambda qi, kv: kv, 0), row_block(lambda qi, kv: kv, 1)

    def tile_spec(heads, w, blk, head_blk=0):
        return pl.BlockSpec((heads, ATT_T, w), lambda s, qi, kv: (head_blk, blk(s, qi, kv), 0))

    def lookahead_specs(blk):
        n = 2
        return [tile_spec(n, LANES, blk), tile_spec(n, LANES, blk, head_blk=N_HM // n)]

    def meta_keys_spec(heads, w):
        return pl.BlockSpec((heads, META_KEYS, w), lambda *_: (0, NF // META_KEYS, 0))

    grid_spec = pltpu.PrefetchScalarGridSpec(
        num_scalar_prefetch=2,
        grid=(ATT_STEPS,),
        in_specs=[smem, tile_spec(N_HT, LANES, q_blk), *lookahead_specs(q_nxt),
                  tile_spec(N_HT, LANES, k_blk), *lookahead_specs(k_nxt),
                  tile_spec(DIFF_HEADS, VA_W, k_blk), tile_spec(FOX_HEADS, VB_W, k_blk),
                  meta_keys_spec(N_HT, LANES), meta_keys_spec(DIFF_HEADS, VA_W), meta_keys_spec(FOX_HEADS, VB_W),
                  const2((4, DIFF_HEAD_DIM)), const2((8, LANES))],
        out_specs=pl.BlockSpec((ATT_T, ATT_OUT_W), lambda s, qi, kv: (q_blk(s, qi, kv), 0)),
        scratch_shapes=_attn_scratch(ATT_T, ATT_T, DIAG_TK),
    )
    out_frames = pl.pallas_call(
        _attn_kernel,
        grid_spec=grid_spec,
        out_shape=jax.ShapeDtypeStruct((NF, ATT_OUT_W), BF16),
        compiler_params=_cparams(("arbitrary",)),
        name="attn_frames",
    )(jnp.asarray(qi_tab), jnp.asarray(kv_tab), coef,
      q_all, q_all, q_all, k_all, k_all, k_all, va, vb, k_all, va, vb, lam, gain_row)
    return out_frames, out_meta


def _log_gamma(h):
    return jnp.log1p(jnp.full((1, 1), -(2.0 ** (-5.0 - h)), F32))


def _retention_kernel(q_ref, kt_ref, v_ref, g_ref, ktm_ref, vm_ref, o_ref, state_ref, decay_ref):
    b = pl.program_id(0)
    t = pl.program_id(1)
    T = RET_T

    @pl.when(jnp.logical_and(b == 0, t == 0))
    def _():
        i = lax.broadcasted_iota(jnp.int32, (T, T), 0)
        j = lax.broadcasted_iota(jnp.int32, (T, T), 1)
        dist = jnp.abs(i - j).astype(F32)
        vis = (j >> 6) <= (i >> 6)
        for h in range(RET_HEADS):
            decay_ref[h] = jnp.where(vis, jnp.exp(_log_gamma(h) * dist), 0.0)

    @pl.when(t == 0)
    def _():
        m = lax.broadcasted_iota(jnp.int32, (1, META_BLK), 1)
        for h in range(RET_HEADS):
            w = jnp.exp(_log_gamma(h) * (N_META - 1 - m).astype(F32))
            kt = ktm_ref[h * RET_QK_DIM:(h + 1) * RET_QK_DIM, :].astype(F32)
            kd = jnp.where(m < N_META, kt * w, 0.0).astype(BF16)
            state_ref[h] = jnp.dot(kd, vm_ref[:, h * RET_V_DIM:(h + 1) * RET_V_DIM],
                                   preferred_element_type=F32)

    row = lax.broadcasted_iota(jnp.int32, (T, 1), 0).astype(F32)
    col = lax.broadcasted_iota(jnp.int32, (1, T), 1).astype(F32)
    for h in range(RET_HEADS):
        lg = _log_gamma(h)
        q = q_ref[:, h * RET_QK_DIM:(h + 1) * RET_QK_DIM]
        kt = kt_ref[h * RET_QK_DIM:(h + 1) * RET_QK_DIM, :]
        v = v_ref[:, h * RET_V_DIM:(h + 1) * RET_V_DIM]
        scores = jnp.dot(q, kt, preferred_element_type=F32) * decay_ref[h]
        o = jnp.dot(scores.astype(BF16), v, preferred_element_type=F32)
        qd = (q.astype(F32) * jnp.exp(lg * (row + 1.0))).astype(BF16)
        state = state_ref[h]
        o = o + jnp.dot(qd, state.astype(BF16), preferred_element_type=F32)
        kd = (kt.astype(F32) * jnp.exp(lg * (T - 1.0 - col))).astype(BF16)
        state_ref[h] = jnp.exp(lg * float(T)) * state + jnp.dot(kd, v, preferred_element_type=F32)
        o = o * lax.rsqrt(jnp.mean(o * o, axis=1, keepdims=True) + RMS_EPS)
        gate = g_ref[:, h * RET_V_DIM:(h + 1) * RET_V_DIM].astype(F32)
        o_ref[:, h * RET_V_DIM:(h + 1) * RET_V_DIM] = (gate * jax.nn.sigmoid(gate) * o).astype(BF16)


def _retention(q, kt, v, g):
    n_t = SEQ // RET_T
    qk_w = RET_HEADS * RET_QK_DIM
    v_w = RET_HEADS * RET_V_DIM
    meta_blk = NF // META_BLK
    return pl.pallas_call(
        _retention_kernel,
        grid=(BATCH, n_t),
        in_specs=[pl.BlockSpec((RET_T, qk_w), lambda b, t: (b * n_t + t, 0)),
                  pl.BlockSpec((qk_w, RET_T), lambda b, t: (0, b * n_t + t)),
                  pl.BlockSpec((RET_T, v_w), lambda b, t: (b * n_t + t, 0)),
                  pl.BlockSpec((RET_T, v_w), lambda b, t: (b * n_t + t, 0)),
                  pl.BlockSpec((qk_w, META_BLK), lambda b, t: (0, meta_blk)),
                  pl.BlockSpec((META_BLK, v_w), lambda b, t: (meta_blk, 0))],
        out_specs=pl.BlockSpec((RET_T, v_w), lambda b, t: (b * n_t + t, 0)),
        out_shape=jax.ShapeDtypeStruct((NF, v_w), BF16),
        scratch_shapes=[pltpu.VMEM((RET_HEADS, RET_QK_DIM, RET_V_DIM), F32),
                        pltpu.VMEM((RET_HEADS, RET_T, RET_T), F32)],
        compiler_params=_cparams(("arbitrary", "arbitrary")),
        name="retention",
    )(q, kt, v, g, kt, v)


def _rows8(*rows):
    n = rows[0].shape[-1]
    out = jnp.zeros((8, n), F32)
    for i, r in enumerate(rows):
        out = out.at[i].set(r.astype(F32))
    return out


def _unit_coef(n):
    return _coef_rows(n, {})


TM_ALL = 2064
TM_LN_ALL = 768
TM_FRAMES = 1024


def kernel(x, meta_tokens, even_w_in, even_f_bias, diff_lambda, diff_subln_g, even_w_out,
           ret_w_in, ret_w_out, ln_g, ln_b, ffn_w1, ffn_w2):
    frames = x.reshape(NF, D_MODEL)
    meta = jnp.pad(meta_tokens.astype(F32), ((0, META_BLK - N_META), (0, 0)))

    w_even, c_even = _even_weights(even_w_in[0]), _even_consts()
    f_bias_row = _rows8(jnp.pad(even_f_bias[0], (0, LANES - FOX_HEADS)))
    proj = _even_proj(frames, w_even, c_even, f_bias_row, tm=512, row0=0, meta=False, cover_tail=True,
                      name="proj_even")
    q_all, k_all, va, vb = _even_proj(meta, w_even, c_even, f_bias_row, tm=META_BLK, row0=NF, meta=True,
                                      into=proj, name="proj_even_meta")

    attn, attn_meta = _attention(q_all, k_all, va, vb, diff_lambda[0].astype(F32), _rows8(diff_subln_g[0]))

    w_out, gb = even_w_out[0].astype(BF16), _rows8(ln_g[0, 0], ln_b[0, 0])
    h1s = _mm_res_ln(attn, w_out, frames, gb, rows=NF, tm=TM_FRAMES, out_rows=NT, cover_tail=True,
                     name="even_out_ln")
    h1, h1_16 = _mm_res_ln(attn_meta, w_out, meta, gb, rows=META_BLK, tm=META_BLK, out_row0=NF, out_rows=NT,
                           into=h1s, name="even_out_ln_meta")
    f1 = _mm_relu2(h1_16, ffn_w1[0].astype(BF16), rows=NT, tm=TM_ALL, tn=1024, name="ffn0_up")
    h2, h2_16 = _mm_res_ln(f1, ffn_w2[0].astype(BF16), h1, _rows8(ln_g[0, 1], ln_b[0, 1]),
                           rows=NT, tm=TM_LN_ALL, name="ffn0_down_ln")

    qk_w = RET_HEADS * RET_QK_DIM
    v_w = RET_HEADS * RET_V_DIM
    rw = ret_w_in[0]
    w_qvg = jnp.concatenate([rw[:, :qk_w], rw[:, 2 * qk_w:]], axis=1).astype(BF16)
    rq, rv, rg = _proj_split(h2_16, w_qvg, (qk_w, v_w, v_w), tm=TM_LN_ALL, name="proj_ret_qvg")
    rkt = _proj_t((rw[:, qk_w:2 * qk_w] * RET_QK_DIM ** -0.5).T.astype(BF16), h2_16, tm=TM_LN_ALL)
    y = _retention(rq, rkt, rv, rg)

    h3, h3_16 = _mm_res_ln(y, ret_w_out[0].astype(BF16), h2, _rows8(ln_g[1, 0], ln_b[1, 0]),
                           rows=NF, tm=TM_FRAMES, name="ret_out_ln")
    f2 = _mm_relu2(h3_16, ffn_w1[1].astype(BF16), rows=NF, tm=2048, tn=1024, name="ffn1_up")
    out, _ = _mm_res_ln(f2, ffn_w2[1].astype(BF16), h3, _rows8(ln_g[1, 1], ln_b[1, 1]),
                        rows=NF, tm=TM_FRAMES, name="ffn1_down_ln")
    return out.reshape(BATCH, SEQ, D_MODEL)
```

```python
import functools
import math

import jax
import jax.numpy as jnp
import numpy as np
from jax import lax
from jax.experimental import pallas as pl
from jax.experimental.pallas import tpu as pltpu

F32 = jnp.float32
BF16 = jnp.bfloat16

D_MODEL = 1024
BATCH = 4
SEQ = 8192
DEPTH = 2
CHUNK = 64
N_META = 16
DIFF_HEADS = 4
DIFF_HEAD_DIM = 64
DIFF_V_DIM = 128
FOX_HEADS = 8
FOX_HEAD_DIM = 64
RET_HEADS = 4
RET_QK_DIM = 256
RET_V_DIM = 512
D_FF = 4 * D_MODEL
DEEPNORM_ALPHA = (2 * DEPTH) ** 0.25
LN_EPS = 1e-5
RMS_EPS = 1e-6
LAM_INIT0 = 0.8 - 0.6 * math.exp(-0.3 * 0)
LOG2E = math.log2(math.e)

NF = BATCH * SEQ
META_BLK = 256
NT = NF + META_BLK
LANES = 128
NEG = -1e30

ATT_T = 512
META_KEYS = 128
RET_T = 256
VMEM_LIMIT = 56 * 1024 * 1024


def _cparams(sem):
    return pltpu.CompilerParams(dimension_semantics=sem, vmem_limit_bytes=VMEM_LIMIT)


def _bf16_pieces(x, n=3):
    out = []
    r = np.float32(x)
    for _ in range(n):
        p = np.float32(np.asarray(r, dtype=BF16).astype(np.float32))
        out.append(float(p))
        r = np.float32(r - p)
    return out


LOG2E_PIECES = _bf16_pieces(LOG2E)


def _row_positions(i, tm, row0=0):
    r = row0 + i * tm + lax.broadcasted_iota(jnp.int32, (tm, 1), 0)
    is_frame = r < NF
    m = r - NF
    pos = jnp.where(is_frame, (r & (SEQ - 1)) + N_META, m)
    valid = jnp.logical_or(is_frame, m < N_META)
    return pos, valid


def _proj_kernel(a_ref, w_ref, c_ref, o_ref, *, tm, heads, head_w, use_pos):
    y = jnp.dot(a_ref[...], w_ref[...], preferred_element_type=F32)
    y = y * c_ref[4:5, :] + c_ref[0:1, :]
    if use_pos:
        pos, valid = _row_positions(pl.program_id(0), tm)
        hi = (pos >> 7).astype(F32)
        lo = (pos & 127).astype(F32)
        y = y + hi * c_ref[1:2, :] + lo * c_ref[2:3, :] + jnp.where(valid, 0.0, 1.0) * c_ref[3:4, :]
    if heads is None:
        o_ref[...] = y.astype(o_ref.dtype)
    else:
        for h in range(heads):
            o_ref[h] = y[:, h * head_w:(h + 1) * head_w].astype(o_ref.dtype)


def _proj(a, w, coef, *, tm, name, heads=None, head_w=None, use_pos=False, out_dtype=BF16):
    m, k = a.shape
    n = w.shape[1]
    if heads is None:
        out_shape = jax.ShapeDtypeStruct((m, n), out_dtype)
        out_spec = pl.BlockSpec((tm, n), lambda i: (i, 0))
    else:
        out_shape = jax.ShapeDtypeStruct((heads, m, head_w), out_dtype)
        out_spec = pl.BlockSpec((heads, tm, head_w), lambda i: (0, i, 0))
    return pl.pallas_call(
        functools.partial(_proj_kernel, tm=tm, heads=heads, head_w=head_w, use_pos=use_pos),
        grid=(m // tm,),
        in_specs=[pl.BlockSpec((tm, k), lambda i: (i, 0)),
                  pl.BlockSpec((k, n), lambda i: (0, 0), pipeline_mode=pl.Buffered(1)),
                  pl.BlockSpec((8, n), lambda i: (0, 0))],
        out_specs=out_spec,
        out_shape=out_shape,
        compiler_params=_cparams(("arbitrary",)),
        name=name,
    )(a, w, coef)


def _proj_t_kernel(wt_ref, a_ref, o_ref):
    o_ref[...] = lax.dot_general(wt_ref[...], a_ref[...], (((1,), (1,)), ((), ())),
                                 preferred_element_type=F32).astype(o_ref.dtype)


def _proj_t(wt, a, *, tm):
    n, k = wt.shape
    m = a.shape[0]
    return pl.pallas_call(
        _proj_t_kernel,
        grid=(m // tm,),
        in_specs=[pl.BlockSpec((n, k), lambda i: (0, 0)),
                  pl.BlockSpec((tm, k), lambda i: (i, 0))],
        out_specs=pl.BlockSpec((n, tm), lambda i: (0, i)),
        out_shape=jax.ShapeDtypeStruct((n, m), BF16),
        compiler_params=_cparams(("arbitrary",)),
        name="proj_ret_kt",
    )(wt, a)


def _relu2_kernel(a_ref, w_ref, o_ref):
    y = jnp.dot(a_ref[...], w_ref[...], preferred_element_type=F32)
    y = jnp.maximum(y, 0.0)
    o_ref[...] = (y * y).astype(o_ref.dtype)


def _mm_relu2(a, w, *, rows, tm, tn, name):
    k = a.shape[1]
    n = w.shape[1]
    return pl.pallas_call(
        _relu2_kernel,
        grid=(rows // tm, n // tn),
        in_specs=[pl.BlockSpec((tm, k), lambda i, j: (i, 0)),
                  pl.BlockSpec((k, tn), lambda i, j: (0, j))],
        out_specs=pl.BlockSpec((tm, tn), lambda i, j: (i, j)),
        out_shape=jax.ShapeDtypeStruct((rows, n), BF16),
        compiler_params=_cparams(("arbitrary", "arbitrary")),
        name=name,
    )(a, w)


def _res_ln_kernel(a_ref, w_ref, r_ref, gb_ref, *refs):
    o32_ref, o16_ref = refs[-2:]
    y = jnp.dot(a_ref[...], w_ref[...], preferred_element_type=F32)
    z = DEEPNORM_ALPHA * r_ref[...] + y
    mu = jnp.mean(z, axis=-1, keepdims=True)
    zc = z - mu
    var = jnp.mean(zc * zc, axis=-1, keepdims=True)
    out = zc * lax.rsqrt(var + LN_EPS) * gb_ref[0:1, :] + gb_ref[1:2, :]
    o32_ref[...] = out
    o16_ref[...] = out.astype(BF16)


def _mm_res_ln(a, w, res, gb, *, rows, tm, name, out_row0=0, out_rows=None, into=None, cover_tail=False):
    k = a.shape[1]
    n = w.shape[1]
    blk0 = out_row0 // tm
    steps = rows // tm
    out_rows = rows if out_rows is None else out_rows
    n_alias = 0 if into is None else len(into)
    src = lambda i: (jnp.minimum(i, steps - 1), 0)
    return pl.pallas_call(
        _res_ln_kernel,
        grid=(steps + int(cover_tail),),
        in_specs=[pl.BlockSpec((tm, k), src),
                  pl.BlockSpec((k, n), lambda i: (0, 0), pipeline_mode=pl.Buffered(1)),
                  pl.BlockSpec((tm, n), src),
                  pl.BlockSpec((8, n), lambda i: (0, 0))] + [pl.BlockSpec(memory_space=pl.ANY)] * n_alias,
        out_specs=[pl.BlockSpec((tm, n), lambda i: (blk0 + i, 0)),
                   pl.BlockSpec((tm, n), lambda i: (blk0 + i, 0))],
        out_shape=[jax.ShapeDtypeStruct((out_rows, n), F32),
                   jax.ShapeDtypeStruct((out_rows, n), BF16)],
        input_output_aliases={4 + j: j for j in range(n_alias)},
        compiler_params=_cparams(("arbitrary",)),
        name=name,
    )(a, w, res, gb, *(into or ()))


N_HM = 2 * DIFF_HEADS
N_HT = N_HM + FOX_HEADS
VA_W = 2 * LANES
VB_W = LANES
BIAS_LANE0 = DIFF_HEAD_DIM
DIFF_BETA_PIECES = 2
FOX_BETA_PIECES = 3
ATT_OUT_W = DIFF_HEADS * DIFF_V_DIM + FOX_HEADS * FOX_HEAD_DIM
SLOPES = [2.0 ** (-8.0 * (h + 1) / DIFF_HEADS) for h in range(DIFF_HEADS)]


def _coef_rows(n, rows):
    c = np.zeros((8, n), np.float32)
    c[4, :] = 1.0
    for r, vals in rows.items():
        for col, val in vals:
            c[r, col] = val
    return jnp.asarray(c)


EVEN_QK_W = N_HT * DIFF_HEAD_DIM
EVEN_VA_W = DIFF_HEADS * DIFF_V_DIM
EVEN_VB_W = FOX_HEADS * FOX_HEAD_DIM
EVEN_N = 2 * EVEN_QK_W + EVEN_VA_W + EVEN_VB_W + LANES
Q_SCALE = DIFF_HEAD_DIM ** -0.5 * LOG2E
ROW_QCONST, ROW_KHI, ROW_KLO, ROW_KPAD, ROW_VCONST = 0, N_HT, 2 * N_HT, 3 * N_HT, 4 * N_HT


def _even_weights(w_in):
    qk_w = DIFF_HEADS * 2 * DIFF_HEAD_DIM
    v_w = DIFF_HEADS * DIFF_V_DIM
    fox_w = FOX_HEADS * FOX_HEAD_DIM
    o = np.cumsum([0, qk_w, qk_w, v_w, fox_w, fox_w, fox_w, FOX_HEADS])
    sl = [w_in[:, o[i]:o[i + 1]] for i in range(7)]
    w_fb = jnp.pad(sl[6], ((0, 0), (0, LANES - FOX_HEADS)))
    return jnp.concatenate([sl[0], sl[3], sl[1], sl[4], sl[2], sl[5], w_fb], axis=1).astype(BF16)


def _even_consts():
    c = np.zeros((4 * N_HT + 8, LANES), np.float32)
    for ht in range(N_HT):
        pieces = DIFF_BETA_PIECES if ht < N_HM else FOX_BETA_PIECES
        for p in range(pieces):
            for r in range(3):
                c[ROW_QCONST + ht, BIAS_LANE0 + 3 * p + r] = LOG2E_PIECES[r]
        if ht < N_HM:
            slope = SLOPES[ht // 2]
            c[ROW_KHI + ht, BIAS_LANE0:BIAS_LANE0 + 3] = slope * 128.0
            c[ROW_KLO + ht, BIAS_LANE0 + 3:BIAS_LANE0 + 6] = slope
            c[ROW_KPAD + ht, BIAS_LANE0] = NEG
    c[ROW_VCONST, 0] = 1.0
    c[ROW_VCONST + 1, FOX_HEAD_DIM] = 1.0
    return jnp.asarray(c)


PIECE_LANES = 8


def _pack3(x):
    hi = x.astype(BF16).astype(F32)
    r1 = x - hi
    mid = r1.astype(BF16).astype(F32)
    lo = (r1 - mid).astype(BF16).astype(F32)
    return (hi + pltpu.roll(mid, PIECE_LANES, 1) + pltpu.roll(lo, 2 * PIECE_LANES, 1)).astype(BF16)


def _unpack3_sum(y):
    return y + pltpu.roll(y, LANES - PIECE_LANES, 1) + pltpu.roll(y, LANES - 2 * PIECE_LANES, 1)


def _forget_bias_lanes(logits, fbias_ref, tri_ref, sel_ref, carry_ref, *, tm, meta):
    @pl.when(pl.program_id(0) % (SEQ // tm) == 0)
    def _():
        carry_ref[...] = jnp.zeros_like(carry_ref)

    heads = lax.broadcasted_iota(jnp.int32, (1, LANES), 1) < FOX_HEADS
    x = logits + fbias_ref[0:1, :]
    logf = jnp.minimum(x, 0.0) - jnp.log1p(jnp.exp(-jnp.abs(x)))
    part = jnp.dot(tri_ref[...], _pack3(jnp.where(heads, logf, 0.0)), preferred_element_type=F32)
    cum = carry_ref[0:1, :] + _unpack3_sum(part)
    carry_ref[0:1, :] = cum[tm - 1:tm, :]
    if meta:
        row = lax.broadcasted_iota(jnp.int32, (tm, 1), 0)
        g = jnp.where(row < N_META, cum[N_META - 1:N_META, :] - cum, NEG)
    else:
        g = -cum
    return jnp.dot(_pack3(jnp.where(heads, g, 0.0)), sel_ref[...], preferred_element_type=F32)


def _even_proj_kernel(a_ref, w_ref, c_ref, fbias_ref, tri_ref, sel_ref, *refs, tm, row0, meta):
    q_ref, k_ref, va_ref, vb_ref, carry_ref = refs[-5:]
    a = a_ref[...].astype(BF16)
    logits = jnp.dot(a, w_ref[:, EVEN_N - LANES:], preferred_element_type=F32)
    fox_bias = _forget_bias_lanes(logits, fbias_ref, tri_ref, sel_ref, carry_ref, tm=tm, meta=meta)
    y = jnp.dot(a, w_ref[:, :EVEN_N - LANES], preferred_element_type=F32)
    lane = lax.broadcasted_iota(jnp.int32, (1, LANES), 1)
    real = lane < DIFF_HEAD_DIM
    pos, valid = _row_positions(pl.program_id(0), tm, row0)
    hi = (pos >> 7).astype(F32)
    lo = (pos & 127).astype(F32)
    pad = jnp.where(valid, 0.0, 1.0)

    def spread(pair, odd):
        return pltpu.roll(pair, DIFF_HEAD_DIM, 1) if odd else pair

    for j in range(N_HT // 2):
        qp = y[:, j * LANES:(j + 1) * LANES] * Q_SCALE
        kp = y[:, EVEN_QK_W + j * LANES:EVEN_QK_W + (j + 1) * LANES]
        for odd in range(2):
            ht = 2 * j + odd
            row = lambda r: c_ref[r + ht:r + ht + 1, :]
            q_ref[ht] = jnp.where(real, spread(qp, odd), row(ROW_QCONST)).astype(BF16)
            if ht < N_HM:
                k_bias = hi * row(ROW_KHI) + lo * row(ROW_KLO) + pad * row(ROW_KPAD)
            else:
                k_bias = fox_bias[:, (ht - N_HM) * LANES:(ht - N_HM + 1) * LANES]
            k_ref[ht] = jnp.where(real, spread(kp, odd), k_bias).astype(BF16)
    base = 2 * EVEN_QK_W
    ones_half = jnp.broadcast_to(c_ref[ROW_VCONST:ROW_VCONST + 1, :], (tm, LANES)).astype(BF16)
    for h in range(DIFF_HEADS):
        va_ref[h, :, :LANES] = y[:, base + h * LANES:base + (h + 1) * LANES].astype(BF16)
        va_ref[h, :, LANES:] = ones_half
    base += EVEN_VA_W
    for j in range(FOX_HEADS // 2):
        vp = y[:, base + j * LANES:base + (j + 1) * LANES]
        for odd in range(2):
            vb_ref[2 * j + odd] = jnp.where(real, spread(vp, odd),
                                            c_ref[ROW_VCONST + 1:ROW_VCONST + 2, :]).astype(BF16)


def _even_proj(a, w, consts, fbias_row, *, tm, row0, meta, into=None, cover_tail=False, name):
    m, k = a.shape
    blk0 = row0 // tm
    steps = m // tm
    n_alias = 0 if into is None else len(into)
    tri = jnp.tril(jnp.ones((tm, tm), F32)).astype(BF16)
    sel = np.zeros((LANES, FOX_HEADS * LANES), np.float32)
    for p in range(FOX_BETA_PIECES):
        for h in range(FOX_HEADS):
            for r in range(3):
                sel[PIECE_LANES * p + h, h * LANES + BIAS_LANE0 + 3 * p + r] = 1.0
    sel = jnp.asarray(sel, BF16)
    const2 = lambda arr: pl.BlockSpec(arr.shape, lambda i: (0, 0))
    n_in = 6
    return pl.pallas_call(
        functools.partial(_even_proj_kernel, tm=tm, row0=row0, meta=meta),
        grid=(steps + int(cover_tail),),
        in_specs=[pl.BlockSpec((tm, k), lambda i: (jnp.minimum(i, steps - 1), 0)),
                  pl.BlockSpec((k, EVEN_N), lambda i: (0, 0), pipeline_mode=pl.Buffered(1)),
                  const2(consts), const2(fbias_row), const2(tri), const2(sel)]
                 + [pl.BlockSpec(memory_space=pl.ANY)] * n_alias,
        out_specs=[pl.BlockSpec((N_HT, tm, LANES), lambda i: (0, blk0 + i, 0)),
                   pl.BlockSpec((N_HT, tm, LANES), lambda i: (0, blk0 + i, 0)),
                   pl.BlockSpec((DIFF_HEADS, tm, VA_W), lambda i: (0, blk0 + i, 0)),
                   pl.BlockSpec((FOX_HEADS, tm, VB_W), lambda i: (0, blk0 + i, 0))],
        out_shape=[jax.ShapeDtypeStruct((N_HT, NT, LANES), BF16),
                   jax.ShapeDtypeStruct((N_HT, NT, LANES), BF16),
                   jax.ShapeDtypeStruct((DIFF_HEADS, NT, VA_W), BF16),
                   jax.ShapeDtypeStruct((FOX_HEADS, NT, VB_W), BF16)],
        scratch_shapes=[pltpu.VMEM((8, LANES), F32)],
        input_output_aliases={n_in + j: j for j in range(n_alias)},
        compiler_params=_cparams(("arbitrary",)),
        name=name,
    )(a, w, consts, fbias_row, tri, sel, *(into or ()))


def _fill_diag_bias(bias_ref, coef_ref, *, tq, tk, pos_off):
    i = lax.broadcasted_iota(jnp.int32, (tq, tk), 0)
    j = lax.broadcasted_iota(jnp.int32, (tq, tk), 1)
    ahead = jnp.maximum(j - i, 0).astype(F32)
    chunk_mask = jnp.where(((j + pos_off) >> 6) > ((i + pos_off) >> 6), NEG, 0.0)
    for h in range(DIFF_HEADS):
        bias_ref[h] = ahead * coef_ref[2 * h] + chunk_mask
    bias_ref[DIFF_HEADS] = jnp.where(j > i, NEG, 0.0)


def _softmax_update(s, m_prev):
    m_new = jnp.maximum(m_prev, jnp.max(s, axis=1, keepdims=True))
    alpha = jnp.exp2(m_prev - m_new)
    p = jnp.exp2(s - m_new).astype(BF16)
    return m_new, alpha, p


PIPE_U = 2
N_GROUPS = N_HT // PIPE_U
NT_DIMS = (((1,), (1,)), ((), ()))


def _head_tile(u, g):
    return g + N_HM * u


def _diag_bias(u, g, bias_ref):
    return bias_ref[g >> 1 if u == 0 else DIFF_HEADS]


def _acc_update(u, g, acc_a, acc_b, alpha, pv):
    if u == 0:
        acc_a[g] = alpha * acc_a[g] + pv
    else:
        acc_b[g] = alpha * acc_b[g] + pv


def _attn_finalize(lam_ref, g_ref, acc_a, acc_b, out_ref):
    lv = lam_ref[...]
    lam = (jnp.exp(jnp.sum(lv[0:1] * lv[1:2], axis=1, keepdims=True))
           - jnp.exp(jnp.sum(lv[2:3] * lv[3:4], axis=1, keepdims=True)) + LAM_INIT0)
    gain = g_ref[0:1, :] * (1.0 - LAM_INIT0)
    for h in range(DIFF_HEADS):
        a0 = acc_a[2 * h]
        a1 = acc_a[2 * h + 1]
        o = (a0[:, :DIFF_V_DIM] * (1.0 / a0[:, DIFF_V_DIM:DIFF_V_DIM + 1])
             - a1[:, :DIFF_V_DIM] * (lam / a1[:, DIFF_V_DIM:DIFF_V_DIM + 1]))
        ms = jnp.mean(o * o, axis=1, keepdims=True)
        out_ref[:, h * DIFF_V_DIM:(h + 1) * DIFF_V_DIM] = (o * lax.rsqrt(ms + RMS_EPS) * gain).astype(BF16)
    base = DIFF_HEADS * DIFF_V_DIM
    low = lax.broadcasted_iota(jnp.int32, (1, LANES), 1) < FOX_HEAD_DIM
    for j in range(FOX_HEADS // 2):
        even, odd = acc_b[2 * j], acc_b[2 * j + 1]
        even = even * (1.0 / even[:, FOX_HEAD_DIM:FOX_HEAD_DIM + 1])
        odd = odd * (1.0 / odd[:, FOX_HEAD_DIM:FOX_HEAD_DIM + 1])
        out_ref[:, base + j * LANES:base + (j + 1) * LANES] = jnp.where(
            low, even, pltpu.roll(odd, FOX_HEAD_DIM, 1)).astype(BF16)


STEPS_PER_BATCH = (SEQ // ATT_T) * (SEQ // ATT_T + 1) // 2
ATT_STEPS = BATCH * STEPS_PER_BATCH
DIAG_TK = ATT_T + META_KEYS


def _attn_kernel(qi_ref, kv_ref, coef_ref,
                 q_ref, qna_ref, qnb_ref, k_ref, kna_ref, knb_ref, va_ref, vb_ref, km_ref, vma_ref, vmb_ref,
                 lam_ref, g_ref,
                 out_ref, m_ref, acc_a, acc_b, bias_ref, *pipe_refs):
    step = pl.program_id(0)
    nxt = jnp.minimum(step + 1, ATT_STEPS - 1)
    r, rn = step % STEPS_PER_BATCH, nxt % STEPS_PER_BATCH
    cur_diag = kv_ref[r] == qi_ref[r]
    nxt_diag = kv_ref[rn] == qi_ref[rn]
    cur_first = kv_ref[r] == 0
    nxt_first = kv_ref[rn] == 0
    s_bufs, p_bufs, a_bufs = _pipe_bufs(pipe_refs)
    T, MK = ATT_T, META_KEYS
    v_refs, vm_refs = (va_ref, vb_ref), (vma_ref, vmb_ref)
    cur_rows = lambda ref: (lambda u, g: ref[_head_tile(u, g)])
    nxt_rows = lambda refs: (lambda u, g: refs[u][g])

    def stage_qk(u, g, s_buf, q_of, k_of, diag):
        q = q_of(u, g)
        s = lax.dot_general(q, k_of(u, g), NT_DIMS, preferred_element_type=F32)
        if diag:
            s = s + _diag_bias(u, g, bias_ref)
            s_buf[:, T:T + MK] = lax.dot_general(q, km_ref[_head_tile(u, g)], NT_DIMS,
                                                 preferred_element_type=F32)
        s_buf[:, :T] = s

    def stage_softmax(u, g, s_buf, p_buf, a_buf, diag, fresh=None):
        w = DIAG_TK if diag else T
        ht = _head_tile(u, g)
        m_prev = m_ref[ht]
        if fresh is not None:
            m_prev = jnp.where(fresh, NEG, m_prev)
        m_ref[ht], a_buf[...], p_buf[:, :w] = _softmax_update(s_buf[:, :w], m_prev)

    def stage_pv(u, g, p_buf, a_buf, diag):
        vi = g >> 1 if u == 0 else g
        pv = jnp.dot(p_buf[:, :T], v_refs[u][vi], preferred_element_type=F32)
        if diag:
            pv = pv + jnp.dot(p_buf[:, T:T + MK], vm_refs[u][vi], preferred_element_type=F32)
        _acc_update(u, g, acc_a, acc_b, a_buf[...], pv)

    def body(g, par, cd, nd):
        for u in range(PIPE_U):
            if isinstance(g, int) and g + 2 >= N_GROUPS:
                stage_qk(u, g + 2 - N_GROUPS, s_bufs[par][u],
                         nxt_rows((qna_ref, qnb_ref)), nxt_rows((kna_ref, knb_ref)), nd)
            else:
                stage_qk(u, g + 2, s_bufs[par][u], cur_rows(q_ref), cur_rows(k_ref), cd)
        for u in range(PIPE_U):
            stage_pv(u, g, p_bufs[par][u], a_bufs[par][u], cd)
        for u in range(PIPE_U):
            bufs = (s_bufs[1 - par][u], p_bufs[1 - par][u], a_bufs[1 - par][u])
            if isinstance(g, int) and g + 1 >= N_GROUPS:
                stage_softmax(u, g + 1 - N_GROUPS, *bufs, nd, fresh=nxt_first)
            else:
                stage_softmax(u, g + 1, *bufs, cd)

    @pl.when(cur_first)
    def _():
        fresh_m = jnp.full((N_HM - 1,) + m_ref.shape[1:], NEG, F32)
        m_ref[1:N_HM] = fresh_m
        m_ref[N_HM + 1:] = fresh_m

    @pl.when(step == 0)
    def _():
        acc_a[...] = jnp.zeros(acc_a.shape, F32)
        acc_b[...] = jnp.zeros(acc_b.shape, F32)
        _fill_diag_bias(bias_ref, coef_ref, tq=T, tk=T, pos_off=0)
        for u in range(PIPE_U):
            m_ref[_head_tile(u, 0)] = jnp.full(m_ref.shape[1:], NEG, F32)
        for g in range(2):
            for u in range(PIPE_U):
                stage_qk(u, g, s_bufs[g][u], cur_rows(q_ref), cur_rows(k_ref), True)
        for u in range(PIPE_U):
            stage_softmax(u, 0, s_bufs[0][u], p_bufs[0][u], a_bufs[0][u], True)

    for cd in (False, True):
        for nd in (False, True):
            @pl.when(jnp.logical_and(cur_diag if cd else jnp.logical_not(cur_diag),
                                     nxt_diag if nd else jnp.logical_not(nxt_diag)))
            def _(cd=cd, nd=nd):
                def loop_body(j, c):
                    body(2 * j, 0, cd, nd)
                    body(2 * j + 1, 1, cd, nd)
                    return c

                lax.fori_loop(0, (N_GROUPS - 2) // 2, loop_body, 0)
                body(N_GROUPS - 2, 0, cd, nd)
                body(N_GROUPS - 1, 1, cd, nd)
                if cd:
                    _attn_finalize(lam_ref, g_ref, acc_a, acc_b, out_ref)


def _meta_attn_kernel(coef_ref, q_ref, k_ref, va_ref, vb_ref, lam_ref, g_ref, out_ref,
                      m_ref, acc_a, acc_b, bias_ref):
    _fill_diag_bias(bias_ref, coef_ref, tq=META_BLK, tk=META_BLK, pos_off=CHUNK - N_META)
    m_ref[...] = jnp.full(m_ref.shape, NEG, F32)
    acc_a[...] = jnp.zeros(acc_a.shape, F32)
    acc_b[...] = jnp.zeros(acc_b.shape, F32)

    def body(g, c):
        for u in range(PIPE_U):
            ht = _head_tile(u, g)
            s = lax.dot_general(q_ref[ht], k_ref[ht], NT_DIMS, preferred_element_type=F32)
            s = s + _diag_bias(u, g, bias_ref)
            m_ref[ht], alpha, p = _softmax_update(s, m_ref[ht])
            v = va_ref[g >> 1] if u == 0 else vb_ref[g]
            _acc_update(u, g, acc_a, acc_b, alpha, jnp.dot(p, v, preferred_element_type=F32))
        return c

    lax.fori_loop(0, N_GROUPS, body, 0)
    _attn_finalize(lam_ref, g_ref, acc_a, acc_b, out_ref)


N_PIPE_BUFS = 2 * PIPE_U


def _pipe_bufs(refs):
    n = N_PIPE_BUFS
    nest = lambda flat: [list(flat[:PIPE_U]), list(flat[PIPE_U:])]
    return nest(refs[:n]), nest(refs[n:2 * n]), nest(refs[2 * n:3 * n])


def _attn_scratch(tq, tk, buf_w=None):
    state = [pltpu.VMEM((N_HT, tq, 1), F32), pltpu.VMEM((N_HM, tq, VA_W), F32),
             pltpu.VMEM((FOX_HEADS, tq, VB_W), F32), pltpu.VMEM((DIFF_HEADS + 1, tq, tk), F32)]
    if buf_w is None:
        return state
    return (state + [pltpu.VMEM((tq, buf_w), F32)] * N_PIPE_BUFS
            + [pltpu.VMEM((tq, buf_w), BF16)] * N_PIPE_BUFS
            + [pltpu.VMEM((tq, 1), F32)] * N_PIPE_BUFS)


def _attention(q_all, k_all, va, vb, lam, gain_row):
    coef = jnp.asarray([-2.0 * SLOPES[hm // 2] * LOG2E for hm in range(N_HM)], F32)
    smem = pl.BlockSpec(memory_space=pltpu.SMEM)
    meta_blk = NF // META_BLK

    def meta_spec(heads, w):
        return pl.BlockSpec((heads, META_BLK, w), lambda *_: (0, meta_blk, 0))

    const2 = lambda shape: pl.BlockSpec(shape, lambda *_: (0, 0))

    out_meta = pl.pallas_call(
        _meta_attn_kernel,
        grid=(1,),
        in_specs=[smem, meta_spec(N_HT, LANES), meta_spec(N_HT, LANES),
                  meta_spec(DIFF_HEADS, VA_W), meta_spec(FOX_HEADS, VB_W),
                  const2((4, DIFF_HEAD_DIM)), const2((8, LANES))],
        out_specs=pl.BlockSpec((META_BLK, ATT_OUT_W), lambda s: (0, 0)),
        out_shape=jax.ShapeDtypeStruct((META_BLK, ATT_OUT_W), BF16),
        scratch_shapes=_attn_scratch(META_BLK, META_BLK),
        compiler_params=_cparams(("arbitrary",)),
        name="attn_meta",
    )(coef, q_all, k_all, va, vb, lam, gain_row)

    n_q = SEQ // ATT_T
    qi_tab = np.concatenate([np.full(q + 1, q, np.int32) for q in range(n_q)])
    kv_tab = np.concatenate([np.arange(q + 1, dtype=np.int32) for q in range(n_q)])

    assert len(qi_tab) == STEPS_PER_BATCH

    def row_block(tab_of, ahead):
        def index(s, qi, kv):
            s = jnp.minimum(s + ahead, ATT_STEPS - 1)
            return (s // STEPS_PER_BATCH) * n_q + tab_of(qi, kv)[s % STEPS_PER_BATCH]
        return index

    q_blk, q_nxt = row_block(lambda qi, kv: qi, 0), row_block(lambda qi, kv: qi, 1)
    k_blk, k_nxt = row_block(lambda qi, kv: kv, 0), row_block(lambda qi, kv: kv, 1)

    def tile_spec(heads, w, blk, head_blk=0):
        return pl.BlockSpec((heads, ATT_T, w), lambda s, qi, kv: (head_blk, blk(s, qi, kv), 0))

    def lookahead_specs(blk):
        n = 2
        return [tile_spec(n, LANES, blk), tile_spec(n, LANES, blk, head_blk=N_HM // n)]

    def meta_keys_spec(heads, w):
        return pl.BlockSpec((heads, META_KEYS, w), lambda *_: (0, NF // META_KEYS, 0))

    grid_spec = pltpu.PrefetchScalarGridSpec(
        num_scalar_prefetch=2,
        grid=(ATT_STEPS,),
        in_specs=[smem, tile_spec(N_HT, LANES, q_blk), *lookahead_specs(q_nxt),
                  tile_spec(N_HT, LANES, k_blk), *lookahead_specs(k_nxt),
                  tile_spec(DIFF_HEADS, VA_W, k_blk), tile_spec(FOX_HEADS, VB_W, k_blk),
                  meta_keys_spec(N_HT, LANES), meta_keys_spec(DIFF_HEADS, VA_W), meta_keys_spec(FOX_HEADS, VB_W),
                  const2((4, DIFF_HEAD_DIM)), const2((8, LANES))],
        out_specs=pl.BlockSpec((ATT_T, ATT_OUT_W), lambda s, qi, kv: (q_blk(s, qi, kv), 0)),
        scratch_shapes=_attn_scratch(ATT_T, ATT_T, DIAG_TK),
    )
    out_frames = pl.pallas_call(
        _attn_kernel,
        grid_spec=grid_spec,
        out_shape=jax.ShapeDtypeStruct((NF, ATT_OUT_W), BF16),
        compiler_params=_cparams(("arbitrary",)),
        name="attn_frames",
    )(jnp.asarray(qi_tab), jnp.asarray(kv_tab), coef,
      q_all, q_all, q_all, k_all, k_all, k_all, va, vb, k_all, va, vb, lam, gain_row)
    return out_frames, out_meta


def _log_gamma(h):
    return jnp.log1p(jnp.full((1, 1), -(2.0 ** (-5.0 - h)), F32))


def _retention_kernel(q_ref, kt_ref, v_ref, g_ref, ktm_ref, vm_ref, o_ref, state_ref, decay_ref):
    b = pl.program_id(0)
    t = pl.program_id(1)
    T = RET_T

    @pl.when(jnp.logical_and(b == 0, t == 0))
    def _():
        i = lax.broadcasted_iota(jnp.int32, (T, T), 0)
        j = lax.broadcasted_iota(jnp.int32, (T, T), 1)
        dist = jnp.abs(i - j).astype(F32)
        vis = (j >> 6) <= (i >> 6)
        for h in range(RET_HEADS):
            decay_ref[h] = jnp.where(vis, jnp.exp(_log_gamma(h) * dist), 0.0)

    @pl.when(t == 0)
    def _():
        m = lax.broadcasted_iota(jnp.int32, (1, META_BLK), 1)
        for h in range(RET_HEADS):
            w = jnp.exp(_log_gamma(h) * (N_META - 1 - m).astype(F32))
            kt = ktm_ref[h * RET_QK_DIM:(h + 1) * RET_QK_DIM, :].astype(F32)
            kd = jnp.where(m < N_META, kt * w, 0.0).astype(BF16)
            state_ref[h] = jnp.dot(kd, vm_ref[:, h * RET_V_DIM:(h + 1) * RET_V_DIM],
                                   preferred_element_type=F32)

    row = lax.broadcasted_iota(jnp.int32, (T, 1), 0).astype(F32)
    col = lax.broadcasted_iota(jnp.int32, (1, T), 1).astype(F32)
    for h in range(RET_HEADS):
        lg = _log_gamma(h)
        q = q_ref[:, h * RET_QK_DIM:(h + 1) * RET_QK_DIM]
        kt = kt_ref[h * RET_QK_DIM:(h + 1) * RET_QK_DIM, :]
        v = v_ref[:, h * RET_V_DIM:(h + 1) * RET_V_DIM]
        scores = jnp.dot(q, kt, preferred_element_type=F32) * decay_ref[h]
        o = jnp.dot(scores.astype(BF16), v, preferred_element_type=F32)
        qd = (q.astype(F32) * jnp.exp(lg * (row + 1.0))).astype(BF16)
        state = state_ref[h]
        o = o + jnp.dot(qd, state.astype(BF16), preferred_element_type=F32)
        kd = (kt.astype(F32) * jnp.exp(lg * (T - 1.0 - col))).astype(BF16)
        state_ref[h] = jnp.exp(lg * float(T)) * state + jnp.dot(kd, v, preferred_element_type=F32)
        o = o * lax.rsqrt(jnp.mean(o * o, axis=1, keepdims=True) + RMS_EPS)
        gate = g_ref[:, h * RET_V_DIM:(h + 1) * RET_V_DIM].astype(F32)
        o_ref[:, h * RET_V_DIM:(h + 1) * RET_V_DIM] = (gate * jax.nn.sigmoid(gate) * o).astype(BF16)


def _retention(q, kt, v, g):
    n_t = SEQ // RET_T
    qk_w = RET_HEADS * RET_QK_DIM
    v_w = RET_HEADS * RET_V_DIM
    meta_blk = NF // META_BLK
    return pl.pallas_call(
        _retention_kernel,
        grid=(BATCH, n_t),
        in_specs=[pl.BlockSpec((RET_T, qk_w), lambda b, t: (b * n_t + t, 0)),
                  pl.BlockSpec((qk_w, RET_T), lambda b, t: (0, b * n_t + t)),
                  pl.BlockSpec((RET_T, v_w), lambda b, t: (b * n_t + t, 0)),
                  pl.BlockSpec((RET_T, v_w), lambda b, t: (b * n_t + t, 0)),
                  pl.BlockSpec((qk_w, META_BLK), lambda b, t: (0, meta_blk)),
                  pl.BlockSpec((META_BLK, v_w), lambda b, t: (meta_blk, 0))],
        out_specs=pl.BlockSpec((RET_T, v_w), lambda b, t: (b * n_t + t, 0)),
        out_shape=jax.ShapeDtypeStruct((NF, v_w), BF16),
        scratch_shapes=[pltpu.VMEM((RET_HEADS, RET_QK_DIM, RET_V_DIM), F32),
                        pltpu.VMEM((RET_HEADS, RET_T, RET_T), F32)],
        compiler_params=_cparams(("arbitrary", "arbitrary")),
        name="retention",
    )(q, kt, v, g, kt, v)


def _rows8(*rows):
    n = rows[0].shape[-1]
    out = jnp.zeros((8, n), F32)
    for i, r in enumerate(rows):
        out = out.at[i].set(r.astype(F32))
    return out


def _unit_coef(n):
    return _coef_rows(n, {})


TM_ALL = 2064
TM_LN_ALL = 768
TM_FRAMES = 1024


def kernel(x, meta_tokens, even_w_in, even_f_bias, diff_lambda, diff_subln_g, even_w_out,
           ret_w_in, ret_w_out, ln_g, ln_b, ffn_w1, ffn_w2):
    frames = x.reshape(NF, D_MODEL)
    meta = jnp.pad(meta_tokens.astype(F32), ((0, META_BLK - N_META), (0, 0)))

    w_even, c_even = _even_weights(even_w_in[0]), _even_consts()
    f_bias_row = _rows8(jnp.pad(even_f_bias[0], (0, LANES - FOX_HEADS)))
    proj = _even_proj(frames, w_even, c_even, f_bias_row, tm=512, row0=0, meta=False, cover_tail=True,
                      name="proj_even")
    q_all, k_all, va, vb = _even_proj(meta, w_even, c_even, f_bias_row, tm=META_BLK, row0=NF, meta=True,
                                      into=proj, name="proj_even_meta")

    attn, attn_meta = _attention(q_all, k_all, va, vb, diff_lambda[0].astype(F32), _rows8(diff_subln_g[0]))

    w_out, gb = even_w_out[0].astype(BF16), _rows8(ln_g[0, 0], ln_b[0, 0])
    h1s = _mm_res_ln(attn, w_out, frames, gb, rows=NF, tm=TM_FRAMES, out_rows=NT, cover_tail=True,
                     name="even_out_ln")
    h1, h1_16 = _mm_res_ln(attn_meta, w_out, meta, gb, rows=META_BLK, tm=META_BLK, out_row0=NF, out_rows=NT,
                           into=h1s, name="even_out_ln_meta")
    f1 = _mm_relu2(h1_16, ffn_w1[0].astype(BF16), rows=NT, tm=TM_ALL, tn=1024, name="ffn0_up")
    h2, h2_16 = _mm_res_ln(f1, ffn_w2[0].astype(BF16), h1, _rows8(ln_g[0, 1], ln_b[0, 1]),
                           rows=NT, tm=TM_LN_ALL, name="ffn0_down_ln")

    qk_w = RET_HEADS * RET_QK_DIM
    v_w = RET_HEADS * RET_V_DIM
    rw = ret_w_in[0]
    rq = _proj(h2_16, rw[:, :qk_w].astype(BF16), _unit_coef(qk_w), tm=TM_ALL, name="proj_ret_q")
    rkt = _proj_t((rw[:, qk_w:2 * qk_w] * RET_QK_DIM ** -0.5).T.astype(BF16), h2_16, tm=TM_LN_ALL)
    rv = _proj(h2_16, rw[:, 2 * qk_w:2 * qk_w + v_w].astype(BF16), _unit_coef(v_w), tm=TM_ALL, name="proj_ret_v")
    rg = _proj(h2_16, rw[:, 2 * qk_w + v_w:].astype(BF16), _unit_coef(v_w), tm=TM_ALL, name="proj_ret_g")
    y = _retention(rq, rkt, rv, rg)

    h3, h3_16 = _mm_res_ln(y, ret_w_out[0].astype(BF16), h2, _rows8(ln_g[1, 0], ln_b[1, 0]),
                           rows=NF, tm=TM_FRAMES, name="ret_out_ln")
    f2 = _mm_relu2(h3_16, ffn_w1[1].astype(BF16), rows=NF, tm=2048, tn=2048, name="ffn1_up")
    out, _ = _mm_res_ln(f2, ffn_w2[1].astype(BF16), h3, _rows8(ln_g[1, 1], ln_b[1, 1]),
                        rows=NF, tm=TM_FRAMES, name="ffn1_down_ln")
    return out.reshape(BATCH, SEQ, D_MODEL)
```
